```python
import jax, jax.numpy as jnp
from jax import lax
import numpy as np

D_MODEL = 2048
BATCH = 1
SEQ = 16384
DEPTH = 4
DEC_BATCH = 2
DEC_SEQ = 16384
PAST_LEN = 128

GRID_W = 64
PLE_DIM = 256
BRANCH_W = 1024
N_BRANCH = 3
EPS = 1e-6
CONV_K = 31
N_Q_HEADS = 8
N_KV_HEADS = 2
HEAD_DIM = 128
ROPE_PAIRS_AXIS = HEAD_DIM // 4
ROPE_THETA = 10000.0
Q_BLOCK = 128
GLA_HEADS = 4
GLA_DK = 128
GLA_DV = 256
GLA_GATE_RANK = 16
GLA_GATE_NORM = 16.0
GLA_CHUNK = 64

IN_SIZES = (
    BRANCH_W, BRANCH_W, BRANCH_W,
    N_Q_HEADS * HEAD_DIM, N_KV_HEADS * HEAD_DIM, N_KV_HEADS * HEAD_DIM, BRANCH_W,
    GLA_HEADS * GLA_DK, GLA_HEADS * GLA_DK, GLA_HEADS * GLA_DV, 2 * GLA_GATE_RANK, BRANCH_W,
    N_BRANCH * D_MODEL,
)
IN_COLS = sum(IN_SIZES)

kernel_name = "hybrid_conv_gqa_gla_encoder"


def _split(x, sizes):
    out, start = [], 0
    for s in sizes:
        out.append(x[..., start:start + s])
        start += s
    return out


def rms_norm(x, g):
    xf = x.astype(jnp.float32)
    y = xf * lax.rsqrt(jnp.mean(xf * xf, axis=-1, keepdims=True) + EPS)
    return (y * g.astype(jnp.float32)).astype(x.dtype)


def layer_norm(x, g, b):
    xf = x.astype(jnp.float32)
    mu = jnp.mean(xf, axis=-1, keepdims=True)
    xc = xf - mu
    y = xc * lax.rsqrt(jnp.mean(xc * xc, axis=-1, keepdims=True) + EPS)
    return (y * g.astype(jnp.float32) + b.astype(jnp.float32)).astype(x.dtype)


def axial_rope_tables(T):
    n_rows = T // GRID_W
    row_idx = jnp.repeat(jnp.arange(n_rows), GRID_W).astype(jnp.float32)
    col_idx = jnp.tile(jnp.arange(GRID_W), n_rows).astype(jnp.float32)
    inv = ROPE_THETA ** (-jnp.arange(ROPE_PAIRS_AXIS, dtype=jnp.float32) / ROPE_PAIRS_AXIS)
    ang = jnp.concatenate([row_idx[:, None] * inv, col_idx[:, None] * inv], axis=-1)
    return jnp.cos(ang), jnp.sin(ang)


def _rotate(x, cos, sin):
    x1, x2 = x[..., :ROPE_PAIRS_AXIS], x[..., ROPE_PAIRS_AXIS:]
    return jnp.concatenate([x1 * cos - x2 * sin, x2 * cos + x1 * sin], axis=-1)


def apply_axial_rope(x, cos, sin):
    xf = x.astype(jnp.float32)
    half = HEAD_DIM // 2
    c_r, s_r = cos[:, None, :ROPE_PAIRS_AXIS], sin[:, None, :ROPE_PAIRS_AXIS]
    c_c, s_c = cos[:, None, ROPE_PAIRS_AXIS:], sin[:, None, ROPE_PAIRS_AXIS:]
    y = jnp.concatenate([_rotate(xf[..., :half], c_r, s_r), _rotate(xf[..., half:], c_c, s_c)], axis=-1)
    return y.astype(x.dtype)


def conv_module(a_val, a_glu, conv_w, conv_b, ln_g, ln_b):
    a = a_val * jax.nn.sigmoid(a_glu)
    c = lax.conv_general_dilated(
        a, conv_w[:, None, :].astype(a.dtype), window_strides=(1,),
        padding=[(CONV_K // 2, CONV_K // 2)],
        dimension_numbers=('NWC', 'WIO', 'NWC'),
        feature_group_count=BRANCH_W) + conv_b
    return jax.nn.silu(layer_norm(c, ln_g, ln_b))


def axial_gqa(q, k, v, q_norm_g, k_norm_g, cos, sin):
    B, T, _ = q.shape
    G = N_Q_HEADS // N_KV_HEADS
    q = q.reshape(B, T, N_Q_HEADS, HEAD_DIM)
    k = k.reshape(B, T, N_KV_HEADS, HEAD_DIM)
    v = v.reshape(B, T, N_KV_HEADS, HEAD_DIM)
    q = apply_axial_rope(rms_norm(q, q_norm_g), cos, sin)
    k = apply_axial_rope(rms_norm(k, k_norm_g), cos, sin)
    nb = T // Q_BLOCK
    qb = q.reshape(B, nb, Q_BLOCK, N_KV_HEADS, G, HEAD_DIM).transpose(1, 0, 2, 3, 4, 5)
    scale = HEAD_DIM ** -0.5

    def block(q_blk):
        s = jnp.einsum('bqhgd,bkhd->bhgqk', q_blk, k, preferred_element_type=jnp.float32) * scale
        p = jax.nn.softmax(s, axis=-1).astype(v.dtype)
        return jnp.einsum('bhgqk,bkhd->bqhgd', p, v)

    o = lax.map(block, qb)
    return o.transpose(1, 0, 2, 3, 4, 5).reshape(B, T, N_Q_HEADS * HEAD_DIM)


def gla_scan(q, k, v, g):
    B, T, H, DK = q.shape
    DV = v.shape[-1]
    nc = T // GLA_CHUNK

    def chunks(a):
        return a.reshape(B, nc, GLA_CHUNK, H, a.shape[-1]).transpose(1, 0, 3, 2, 4)

    mask = jnp.tril(jnp.ones((GLA_CHUNK, GLA_CHUNK), dtype=bool))

    def step(S, inp):
        qi, ki, vi, gi = inp
        b = jnp.cumsum(gi, axis=-2)
        b_last = b[..., -1:, :]
        q_t = qi * jnp.exp(b)
        k_t = ki * jnp.exp(-b)
        att = jnp.where(mask, jnp.einsum('bhid,bhjd->bhij', q_t, k_t), 0.0)
        o = jnp.einsum('bhij,bhjv->bhiv', att, vi) + jnp.einsum('bhid,bhdv->bhiv', q_t, S)
        S = jnp.exp(b_last[..., 0, :])[..., None] * S + jnp.einsum(
            'bhjd,bhjv->bhdv', ki * jnp.exp(b_last - b), vi)
        return S, o

    S0 = jnp.zeros((B, H, DK, DV), jnp.float32)
    _, o = lax.scan(step, S0, (chunks(q), chunks(k), chunks(v), chunks(g)))
    return o.transpose(1, 0, 3, 2, 4).reshape(B, T, H, DV)


def bidir_gla(q, k, v, gate_lr, gate_w2, gate_b, norm_g):
    B, T, _ = q.shape
    f32 = jnp.float32
    qf = q.astype(f32).reshape(B, T, GLA_HEADS, GLA_DK) * (GLA_DK ** -0.5)
    kf = k.astype(f32).reshape(B, T, GLA_HEADS, GLA_DK)
    vf = v.astype(f32).reshape(B, T, GLA_HEADS, GLA_DV)
    lr = gate_lr.astype(f32)

    def log_gate(lr_d, w2, b):
        z = jnp.einsum('btr,rk->btk', lr_d, w2.astype(f32)) + b.astype(f32)
        return (jax.nn.log_sigmoid(z) / GLA_GATE_NORM).reshape(B, T, GLA_HEADS, GLA_DK)

    g_fwd = log_gate(lr[..., :GLA_GATE_RANK], gate_w2[0], gate_b[0])
    g_bwd = log_gate(lr[..., GLA_GATE_RANK:], gate_w2[1], gate_b[1])
    o_fwd = gla_scan(qf, kf, vf, g_fwd)
    o_bwd = gla_scan(qf[:, ::-1], kf[:, ::-1], vf[:, ::-1], g_bwd[:, ::-1])[:, ::-1]
    o = rms_norm(o_fwd + o_bwd, norm_g)
    return o.reshape(B, T, GLA_HEADS * GLA_DV).astype(q.dtype)


def encoder_layer(h, p_i, cos, sin, pre_g, post_g, w_in, conv_w, conv_b, conv_ln_g, conv_ln_b,
                  q_norm_g, k_norm_g, gla_gate_w2, gla_gate_b, gla_norm_g,
                  w_branch, w_out, w_ple, w_ple_gate):
    xn = rms_norm(h, pre_g)
    u = jnp.einsum('btd,dc->btc', xn, w_in)
    (a_val, a_glu, a_z, b_q, b_k, b_v, b_z,
     c_q, c_k, c_v, c_lr, c_z, m_logits) = _split(u, IN_SIZES)
    o_a = conv_module(a_val, a_glu, conv_w, conv_b, conv_ln_g, conv_ln_b) * jax.nn.silu(a_z)
    o_b = axial_gqa(b_q, b_k, b_v, q_norm_g, k_norm_g, cos, sin) * jax.nn.silu(b_z)
    o_c = bidir_gla(c_q, c_k, c_v, c_lr, gla_gate_w2, gla_gate_b, gla_norm_g) * jax.nn.silu(c_z)
    merged = None
    for n, o in enumerate((o_a, o_b, o_c)):
        gate = jax.nn.sigmoid(m_logits[..., n * D_MODEL:(n + 1) * D_MODEL])
        term = gate * jnp.einsum('btw,wd->btd', o, w_branch[n])
        merged = term if merged is None else merged + term
    out = jnp.einsum('btd,de->bte', merged, w_out)
    h = h + rms_norm(out, post_g)
    e = jnp.einsum('btp,pd->btd', p_i, w_ple)
    h = h + jax.nn.sigmoid(jnp.einsum('btd,de->bte', h, w_ple_gate)) * e
    return h


def trunk(x, p, pre_norm_g, post_norm_g, w_in, conv_w, conv_b, conv_ln_g, conv_ln_b,
          q_norm_g, k_norm_g, gla_gate_w2, gla_gate_b, gla_norm_g, w_branch, w_out, w_ple, w_ple_gate):
    cos, sin = axial_rope_tables(x.shape[1])
    h = x
    for i in range(DEPTH):
        h = encoder_layer(h, p[i], cos, sin, pre_norm_g[i], post_norm_g[i], w_in[i], conv_w[i], conv_b[i],
                          conv_ln_g[i], conv_ln_b[i], q_norm_g[i], k_norm_g[i], gla_gate_w2[i],
                          gla_gate_b[i], gla_norm_g[i], w_branch[i], w_out[i], w_ple[i], w_ple_gate[i])
    return h


def setup_inputs(seed: int = 0) -> dict:
    key = jax.random.key(seed)
    ks = jax.random.split(key, 24)
    f32 = jnp.float32
    nrm = lambda k, shape, s: jax.random.normal(k, shape, f32) * s
    return {
        "x_prompt": nrm(ks[0], (BATCH, SEQ, D_MODEL), 1.0),
        "x_sample": nrm(ks[1], (DEC_BATCH, DEC_SEQ, D_MODEL), 1.0),
        "p_prompt": nrm(ks[2], (DEPTH, BATCH, SEQ, PLE_DIM), 1.0),
        "p_sample": nrm(ks[3], (DEPTH, DEC_BATCH, DEC_SEQ, PLE_DIM), 1.0),
        "pre_norm_g": 1.0 + nrm(ks[4], (DEPTH, D_MODEL), 0.05),
        "post_norm_g": 1.0 + nrm(ks[5], (DEPTH, D_MODEL), 0.05),
        "w_in": nrm(ks[6], (DEPTH, D_MODEL, IN_COLS), D_MODEL ** -0.5),
        "conv_w": nrm(ks[7], (DEPTH, CONV_K, BRANCH_W), CONV_K ** -0.5),
        "conv_b": nrm(ks[8], (DEPTH, BRANCH_W), 0.02),
        "conv_ln_g": 1.0 + nrm(ks[9], (DEPTH, BRANCH_W), 0.05),
        "conv_ln_b": nrm(ks[10], (DEPTH, BRANCH_W), 0.02),
        "q_norm_g": 1.0 + nrm(ks[11], (DEPTH, HEAD_DIM), 0.05),
        "k_norm_g": 1.0 + nrm(ks[12], (DEPTH, HEAD_DIM), 0.05),
        "gla_gate_w2": nrm(ks[13], (DEPTH, 2, GLA_GATE_RANK, GLA_HEADS * GLA_DK), GLA_GATE_RANK ** -0.5),
        "gla_gate_b": nrm(ks[14], (DEPTH, 2, GLA_HEADS * GLA_DK), 0.02),
        "gla_norm_g": 1.0 + nrm(ks[15], (DEPTH, GLA_DV), 0.05),
        "w_branch": nrm(ks[16], (DEPTH, N_BRANCH, BRANCH_W, D_MODEL), BRANCH_W ** -0.5),
        "w_out": nrm(ks[17], (DEPTH, D_MODEL, D_MODEL), D_MODEL ** -0.5),
        "w_ple": nrm(ks[18], (DEPTH, PLE_DIM, D_MODEL), PLE_DIM ** -0.5),
        "w_ple_gate": nrm(ks[19], (DEPTH, D_MODEL, D_MODEL), D_MODEL ** -0.5),
    }


def reference(x_prompt, x_sample, p_prompt, p_sample, pre_norm_g, post_norm_g, w_in, conv_w, conv_b,
              conv_ln_g, conv_ln_b, q_norm_g, k_norm_g, gla_gate_w2, gla_gate_b, gla_norm_g,
              w_branch, w_out, w_ple, w_ple_gate):
    y_prompt = trunk(x_prompt, p_prompt, pre_norm_g, post_norm_g, w_in, conv_w, conv_b, conv_ln_g,
                     conv_ln_b, q_norm_g, k_norm_g, gla_gate_w2, gla_gate_b, gla_norm_g,
                     w_branch, w_out, w_ple, w_ple_gate)
    y_sample = trunk(x_sample, p_sample, pre_norm_g, post_norm_g, w_in, conv_w, conv_b, conv_ln_g,
                     conv_ln_b, q_norm_g, k_norm_g, gla_gate_w2, gla_gate_b, gla_norm_g,
                     w_branch, w_out, w_ple, w_ple_gate)
    return (y_prompt, y_sample)
```

```python
import functools
import math

import jax
import jax.numpy as jnp
from jax import lax
from jax.experimental import pallas as pl
from jax.experimental.pallas import tpu as pltpu

F32 = jnp.float32
BF16 = jnp.bfloat16

D_MODEL = 2048
GRID_W = 64
PLE_DIM = 256
BRANCH_W = 1024
EPS = 1e-6
CONV_K = 31
CONV_HALO = 16
N_Q_HEADS = 8
N_KV_HEADS = 2
Q_PER_KV = N_Q_HEADS // N_KV_HEADS
HEAD_DIM = 128
ROPE_PAIRS = HEAD_DIM // 4
ROPE_THETA = 10000.0
GLA_HEADS = 4
GLA_DK = 128
GLA_DV = 256
GLA_RANK = 16
GLA_GATE_NORM = 16.0
GLA_CHUNK = 64

A_VAL, A_GLU, A_Z = 0, 1024, 2048
B_Q, B_Z = 3072, 4096
C_V, C_Z = 5120, 6144
C_Q, C_K = 7168, 7680
M_LOG = 8192
B_KV = 14336
C_LR = 14848
U_COLS = 14976

VMEM_LIMIT = 56 * 1024 * 1024


def _cparams(sem, vmem=VMEM_LIMIT):
    return pltpu.CompilerParams(dimension_semantics=sem, vmem_limit_bytes=vmem)


def _tile(n, cap, mult=8):
    t = min(cap, n)
    while t > mult and (n % t or t % mult):
        t -= mult
    assert n % t == 0, (n, cap, mult)
    return t


def _sigmoid(x):
    return 1.0 / (1.0 + jnp.exp(-x))


def _silu(x):
    return x * _sigmoid(x)


def _prenorm_kernel(x_ref, g_ref, o_ref):
    x = x_ref[...]
    ms = jnp.mean(x * x, axis=-1, keepdims=True)
    o_ref[...] = (x * lax.rsqrt(ms + EPS) * g_ref[...]).astype(BF16)


def _prenorm(x, g):
    n = x.shape[0]
    tm = _tile(n, 256)
    return pl.pallas_call(
        _prenorm_kernel,
        grid=(n // tm,),
        in_specs=[pl.BlockSpec((tm, D_MODEL), lambda i: (i, 0)),
                  pl.BlockSpec((1, D_MODEL), lambda i: (0, 0))],
        out_specs=pl.BlockSpec((tm, D_MODEL), lambda i: (i, 0)),
        out_shape=jax.ShapeDtypeStruct((n, D_MODEL), BF16),
        compiler_params=_cparams(("parallel",)),
        name="prenorm",
    )(x, g)


def _inproj_kernel(x_ref, w_ref, u_ref):
    u_ref[...] = jnp.dot(x_ref[...], w_ref[...], preferred_element_type=F32).astype(BF16)


def _inproj(xn, w):
    n = xn.shape[0]
    tm = _tile(n, 1024, 16)
    tn = 1152
    return pl.pallas_call(
        _inproj_kernel,
        grid=(n // tm, U_COLS // tn),
        in_specs=[pl.BlockSpec((tm, D_MODEL), lambda i, j: (i, 0)),
                  pl.BlockSpec((D_MODEL, tn), lambda i, j: (0, j))],
        out_specs=pl.BlockSpec((tm, tn), lambda i, j: (i, j)),
        out_shape=jax.ShapeDtypeStruct((n, U_COLS), BF16),
        compiler_params=_cparams(("parallel", "arbitrary")),
        name="inproj",
    )(xn, w)


def _conv_kernel(val_ref, glu_ref, z_ref, pval_ref, pglu_ref, nval_ref, nglu_ref,
                 w_ref, cb_ref, lg_ref, lb_ref, o_ref, abuf, cbuf):
    i = pl.program_id(1)
    n = pl.num_programs(1)
    tt = val_ref.shape[0]
    H = CONV_HALO

    def glu(v_ref, g_ref):
        return v_ref[...].astype(F32) * _sigmoid(g_ref[...].astype(F32))

    abuf[0:H, :] = glu(pval_ref, pglu_ref) * (i > 0).astype(F32)
    abuf[H:H + tt, :] = glu(val_ref, glu_ref)
    abuf[H + tt:2 * H + tt, :] = glu(nval_ref, nglu_ref) * (i < n - 1).astype(F32)

    rows = 64 if tt % 64 == 0 else tt
    base = H - CONV_K // 2
    for c in range(BRANCH_W // 128):
        cs = slice(c * 128, (c + 1) * 128)
        for r in range(tt // rows):
            acc = jnp.zeros((rows, 128), F32)
            for k in range(CONV_K):
                lo = base + r * rows + k
                acc = acc + abuf[lo:lo + rows, cs] * w_ref[k:k + 1, cs]
            cbuf[r * rows:(r + 1) * rows, cs] = acc + cb_ref[:, cs]

    x = cbuf[...]
    mu = jnp.mean(x, axis=-1, keepdims=True)
    xc = x - mu
    var = jnp.mean(xc * xc, axis=-1, keepdims=True)
    y = xc * lax.rsqrt(var + EPS) * lg_ref[...] + lb_ref[...]
    o_ref[...] = (_silu(y) * _silu(z_ref[...].astype(F32))).astype(BF16)


def _conv_branch(u3, conv_w, conv_b, ln_g, ln_b):
    B, T, _ = u3.shape
    tt = _tile(T, 128, 16)
    H = CONV_HALO
    hb = tt // H
    nh = T // H

    def cur(col):
        return pl.BlockSpec((None, tt, BRANCH_W), lambda b, i: (b, i, col))

    def prev(col):
        return pl.BlockSpec((None, H, BRANCH_W), lambda b, i: (b, jnp.maximum(i * hb - 1, 0), col))

    def nxt(col):
        return pl.BlockSpec((None, H, BRANCH_W), lambda b, i: (b, jnp.minimum((i + 1) * hb, nh - 1), col))

    vec = pl.BlockSpec((1, BRANCH_W), lambda b, i: (0, 0))
    return pl.pallas_call(
        _conv_kernel,
        grid=(B, T // tt),
        in_specs=[cur(A_VAL // BRANCH_W), cur(A_GLU // BRANCH_W), cur(A_Z // BRANCH_W),
                  prev(A_VAL // BRANCH_W), prev(A_GLU // BRANCH_W),
                  nxt(A_VAL // BRANCH_W), nxt(A_GLU // BRANCH_W),
                  pl.BlockSpec((CONV_K + 1, BRANCH_W), lambda b, i: (0, 0)), vec, vec, vec],
        out_specs=pl.BlockSpec((None, tt, BRANCH_W), lambda b, i: (b, i, 0)),
        out_shape=jax.ShapeDtypeStruct((B, T, BRANCH_W), BF16),
        scratch_shapes=[pltpu.VMEM((tt + 2 * H, BRANCH_W), F32), pltpu.VMEM((tt, BRANCH_W), F32)],
        compiler_params=_cparams(("parallel", "parallel")),
        name="conv",
    )(u3, u3, u3, u3, u3, u3, u3, conv_w, conv_b, ln_g, ln_b)


def _norm_rope(x, g, cos, sin, first_half):
    ms = jnp.mean(x * x, axis=-1, keepdims=True)
    y = x * lax.rsqrt(ms + EPS) * g
    swapped = jnp.where(first_half, pltpu.roll(y, HEAD_DIM - ROPE_PAIRS, 1), pltpu.roll(y, ROPE_PAIRS, 1))
    return y * cos + swapped * sin


def _qkprep_kernel(q_ref, kv_ref, cos_ref, sin_ref, gq_ref, gk_ref, qT_ref, k_ref, vT_ref):
    tt = q_ref.shape[0]
    cos = cos_ref[...]
    sin = sin_ref[...]
    lane = lax.broadcasted_iota(jnp.int32, (tt, HEAD_DIM), 1)
    first_half = (lane % (2 * ROPE_PAIRS)) < ROPE_PAIRS
    qscale = HEAD_DIM ** -0.5 * math.log2(math.e)
    for h in range(N_Q_HEADS):
        x = q_ref[:, h * HEAD_DIM:(h + 1) * HEAD_DIM].astype(F32)
        y = _norm_rope(x, gq_ref[...], cos, sin, first_half) * qscale
        qT_ref[h] = y.T.astype(BF16)
    for g in range(N_KV_HEADS):
        x = kv_ref[:, g * HEAD_DIM:(g + 1) * HEAD_DIM].astype(F32)
        k_ref[g] = _norm_rope(x, gk_ref[...], cos, sin, first_half).astype(BF16)
        v = kv_ref[:, (N_KV_HEADS + g) * HEAD_DIM:(N_KV_HEADS + g + 1) * HEAD_DIM].astype(F32)
        vT_ref[g, 0] = v.T.astype(BF16)


def _qkprep(u3, cos, sin, gq, gk, tk):
    B, T, _ = u3.shape
    qw = N_Q_HEADS * HEAD_DIM
    kvw = 2 * N_KV_HEADS * HEAD_DIM
    vec = pl.BlockSpec((1, HEAD_DIM), lambda b, i: (0, 0))
    tab = pl.BlockSpec((tk, HEAD_DIM), lambda b, i: (i, 0))
    return pl.pallas_call(
        _qkprep_kernel,
        grid=(B, T // tk),
        in_specs=[pl.BlockSpec((None, tk, qw), lambda b, i: (b, i, B_Q // qw)),
                  pl.BlockSpec((None, tk, kvw), lambda b, i: (b, i, B_KV // kvw)),
                  tab, tab, vec, vec],
        out_specs=[pl.BlockSpec((None, N_Q_HEADS, HEAD_DIM, tk), lambda b, i: (b, 0, 0, i)),
                   pl.BlockSpec((None, N_KV_HEADS, tk, HEAD_DIM), lambda b, i: (b, 0, i, 0)),
                   pl.BlockSpec((None, N_KV_HEADS, 1, HEAD_DIM, tk), lambda b, i: (b, 0, i, 0, 0))],
        out_shape=[jax.ShapeDtypeStruct((B, N_Q_HEADS, HEAD_DIM, T), BF16),
                   jax.ShapeDtypeStruct((B, N_KV_HEADS, T, HEAD_DIM), BF16),
                   jax.ShapeDtypeStruct((B, N_KV_HEADS, T // tk, HEAD_DIM, tk), BF16)],
        compiler_params=_cparams(("parallel", "parallel")),
        name="qkprep",
    )(u3, u3, cos, sin, gq, gk)


def _attn_kernel(qT_ref, k_ref, vT_ref, o_ref, acc_ref):
    nk, _, tk = vT_ref.shape
    tq = qT_ref.shape[2]
    acc_ref[...] = jnp.zeros_like(acc_ref)

    def body(j, carry):
        ms, ls = carry
        k = k_ref[pl.ds(pl.multiple_of(j * tk, tk), tk), :]
        vT = vT_ref[j]
        new_ms, new_ls = [], []
        for h in range(Q_PER_KV):
            sT = jnp.dot(k, qT_ref[h], preferred_element_type=F32)
            m_new = jnp.maximum(ms[h], jnp.max(sT, axis=0, keepdims=True))
            alpha = jnp.exp2(ms[h] - m_new)
            p = jnp.exp2(sT - m_new)
            new_ls.append(alpha * ls[h] + jnp.sum(p, axis=0, keepdims=True))
            pv = jnp.dot(vT, p.astype(BF16), preferred_element_type=F32)
            acc_ref[h] = alpha * acc_ref[h] + pv
            new_ms.append(m_new)
        return tuple(new_ms), tuple(new_ls)

    m0 = tuple(jnp.full((1, tq), -jnp.inf, F32) for _ in range(Q_PER_KV))
    l0 = tuple(jnp.zeros((1, tq), F32) for _ in range(Q_PER_KV))
    _, ls = lax.fori_loop(0, nk, body, (m0, l0))
    for h in range(Q_PER_KV):
        o = acc_ref[h] * (1.0 / ls[h])
        o_ref[:, h * HEAD_DIM:(h + 1) * HEAD_DIM] = o.T.astype(BF16)


def _attention(qT, k, vT):
    B, _, _, T = qT.shape
    nk, tk = vT.shape[2], vT.shape[4]
    tq = _tile(T, 256, 128)
    gw = Q_PER_KV * HEAD_DIM
    return pl.pallas_call(
        _attn_kernel,
        grid=(B, N_KV_HEADS, T // tq),
        in_specs=[pl.BlockSpec((None, Q_PER_KV, HEAD_DIM, tq), lambda b, g, i: (b, g, 0, i)),
                  pl.BlockSpec((None, None, T, HEAD_DIM), lambda b, g, i: (b, g, 0, 0)),
                  pl.BlockSpec((None, None, nk, HEAD_DIM, tk), lambda b, g, i: (b, g, 0, 0, 0))],
        out_specs=pl.BlockSpec((None, tq, gw), lambda b, g, i: (b, i, g)),
        out_shape=jax.ShapeDtypeStruct((B, T, N_Q_HEADS * HEAD_DIM), BF16),
        scratch_shapes=[pltpu.VMEM((Q_PER_KV, HEAD_DIM, tq), F32)],
        compiler_params=_cparams(("parallel", "parallel", "arbitrary")),
        name="attention",
    )(qT, k, vT)


def _log_sigmoid(z):
    return jnp.minimum(z, 0.0) - jnp.log(1.0 + jnp.exp(-jnp.abs(z)))


def _gla_kernel(qf_ref, kf_ref, vf_ref, lrf_ref, qb_ref, kb_ref, vb_ref, lrb_ref,
                w2_ref, gb_ref, of_ref, ob_ref, s_ref):
    i = pl.program_id(1)
    tb = qf_ref.shape[0]
    L = GLA_CHUNK
    nchunk = tb // L

    @pl.when(i == 0)
    def _():
        s_ref[...] = jnp.zeros_like(s_ref)

    row = lax.broadcasted_iota(jnp.int32, (L, L), 0)
    col = lax.broadcasted_iota(jnp.int32, (L, L), 1)
    masks = (row >= col, col >= row)
    ones = tuple(jnp.where(m, 1.0, 0.0).astype(BF16) for m in masks)
    qscale = GLA_DK ** -0.5
    dirs = ((qf_ref, kf_ref, vf_ref, lrf_ref, of_ref), (qb_ref, kb_ref, vb_ref, lrb_ref, ob_ref))

    log_gate = []
    for d in range(2):
        z = jnp.dot(dirs[d][3][...], w2_ref[d], preferred_element_type=F32) + gb_ref[d]
        log_gate.append(_log_sigmoid(z) * (1.0 / GLA_GATE_NORM))

    for step in range(nchunk):
        for d in range(2):
            q_ref, k_ref, v_ref, _, o_ref = dirs[d]
            c = step if d == 0 else nchunk - 1 - step
            rows = slice(c * L, (c + 1) * L)
            g = log_gate[d][rows, :]
            g_hi = g.astype(BF16)
            g_lo = (g - g_hi.astype(F32)).astype(BF16)
            b_all = (jnp.dot(ones[d], g_hi, preferred_element_type=F32)
                     + jnp.dot(ones[d], g_lo, preferred_element_type=F32))
            for h in range(GLA_HEADS):
                ks = slice(h * GLA_DK, (h + 1) * GLA_DK)
                vs = slice(h * GLA_DV, (h + 1) * GLA_DV)
                b = b_all[:, ks]
                b_tot = b[L - 1:L, :] if d == 0 else b[0:1, :]
                q = q_ref[rows, ks].astype(F32) * qscale
                k = k_ref[rows, ks].astype(F32)
                v = v_ref[rows, vs]
                q_t = (q * jnp.exp(b)).astype(BF16)
                k_t = (k * jnp.exp(-b)).astype(BF16)
                k_d = (k * jnp.exp(b_tot - b)).astype(BF16)
                att = lax.dot_general(q_t, k_t, (((1,), (1,)), ((), ())), preferred_element_type=F32)
                att = jnp.where(masks[d], att, 0.0).astype(BF16)
                sT = s_ref[d, h]
                o = (jnp.dot(att, v, preferred_element_type=F32)
                     + lax.dot_general(q_t, sT.astype(BF16), (((1,), (1,)), ((), ())),
                                       preferred_element_type=F32))
                o_ref[rows, vs] = o.astype(o_ref.dtype)
                dsT = lax.dot_general(v, k_d, (((0,), (0,)), ((), ())), preferred_element_type=F32)
                s_ref[d, h] = sT * jnp.exp(b_tot) + dsT


def _gla(u3, w2p, gb):
    B, T, _ = u3.shape
    tb = _tile(T, 256, GLA_CHUNK)
    nb = T // tb
    qk_w = GLA_HEADS * GLA_DK
    v_w = GLA_HEADS * GLA_DV

    def specs(idx):
        return [pl.BlockSpec((None, tb, qk_w), lambda b, i: (b, idx(i), C_Q // qk_w)),
                pl.BlockSpec((None, tb, qk_w), lambda b, i: (b, idx(i), C_K // qk_w)),
                pl.BlockSpec((None, tb, v_w), lambda b, i: (b, idx(i), C_V // v_w)),
                pl.BlockSpec((None, tb, 128), lambda b, i: (b, idx(i), C_LR // 128))]

    fwd = lambda i: i
    bwd = lambda i: nb - 1 - i
    out_sd = jax.ShapeDtypeStruct((B, T, v_w), BF16)
    return pl.pallas_call(
        _gla_kernel,
        grid=(B, nb),
        in_specs=specs(fwd) + specs(bwd) + [
            pl.BlockSpec((2, 128, qk_w), lambda b, i: (0, 0, 0)),
            pl.BlockSpec((2, 1, qk_w), lambda b, i: (0, 0, 0))],
        out_specs=[pl.BlockSpec((None, tb, v_w), lambda b, i: (b, i, 0)),
                   pl.BlockSpec((None, tb, v_w), lambda b, i: (b, nb - 1 - i, 0))],
        out_shape=[out_sd, out_sd],
        scratch_shapes=[pltpu.VMEM((2, GLA_HEADS, GLA_DV, GLA_DK), F32)],
        compiler_params=_cparams(("parallel", "arbitrary")),
        name="gla",
    )(u3, u3, u3, u3, u3, u3, u3, u3, w2p, gb)


def _merge_kernel(oa_ref, ob_ref, bz_ref, cf_ref, cb_ref, cz_ref, m0_ref, m1_ref, m2_ref, h_ref,
                  wb_ref, wo_ref, gn_ref, pg_ref, o_ref):
    o_a = oa_ref[...]
    o_b = (ob_ref[...].astype(F32) * _silu(bz_ref[...].astype(F32))).astype(BF16)
    parts = []
    for h in range(GLA_HEADS):
        vs = slice(h * GLA_DV, (h + 1) * GLA_DV)
        x = cf_ref[:, vs].astype(F32) + cb_ref[:, vs].astype(F32)
        ms = jnp.mean(x * x, axis=-1, keepdims=True)
        y = x * lax.rsqrt(ms + EPS) * gn_ref[...]
        parts.append((y * _silu(cz_ref[:, vs].astype(F32))).astype(BF16))
    o_c = jnp.concatenate(parts, axis=-1)

    merged = None
    for n, (o, m_ref) in enumerate(((o_a, m0_ref), (o_b, m1_ref), (o_c, m2_ref))):
        term = _sigmoid(m_ref[...].astype(F32)) * jnp.dot(o, wb_ref[n], preferred_element_type=F32)
        merged = term if merged is None else merged + term
    out = jnp.dot(merged.astype(BF16), wo_ref[...], preferred_element_type=F32)
    ms = jnp.mean(out * out, axis=-1, keepdims=True)
    o_ref[...] = h_ref[...] + out * lax.rsqrt(ms + EPS) * pg_ref[...]


def _merge(o_a, o_b, o_cf, o_cb, u, h, wb, wo, gla_g, post_g):
    n = h.shape[0]
    tm = _tile(n, 256, 16)
    W = BRANCH_W

    def rows(width, col=0):
        return pl.BlockSpec((tm, width), lambda i: (i, col))

    def const(shape):
        return pl.BlockSpec(shape, lambda i: (0,) * len(shape), pipeline_mode=pl.Buffered(1))

    return pl.pallas_call(
        _merge_kernel,
        grid=(n // tm,),
        in_specs=[rows(W), rows(W), rows(W, B_Z // W), rows(W), rows(W), rows(W, C_Z // W),
                  rows(D_MODEL, M_LOG // D_MODEL), rows(D_MODEL, M_LOG // D_MODEL + 1),
                  rows(D_MODEL, M_LOG // D_MODEL + 2), rows(D_MODEL),
                  const((3, W, D_MODEL)), const((D_MODEL, D_MODEL)),
                  const((1, GLA_DV)), const((1, D_MODEL))],
        out_specs=rows(D_MODEL),
        out_shape=jax.ShapeDtypeStruct((n, D_MODEL), F32),
        compiler_params=_cparams(("parallel",)),
        name="merge",
    )(o_a, o_b, u, o_cf, o_cb, u, u, u, u, h, wb, wo, gla_g, post_g)


def _ple_kernel(h_ref, p_ref, wp_ref, wg_ref, ng_ref, o_ref, xn_ref):
    h = h_ref[...]
    e = jnp.dot(p_ref[...].astype(BF16), wp_ref[...], preferred_element_type=F32)
    gate = _sigmoid(jnp.dot(h.astype(BF16), wg_ref[...], preferred_element_type=F32))
    h2 = h + gate * e
    o_ref[...] = h2
    ms = jnp.mean(h2 * h2, axis=-1, keepdims=True)
    xn_ref[...] = (h2 * lax.rsqrt(ms + EPS) * ng_ref[...]).astype(BF16)


def _ple(h, p, wp, wg, next_g):
    n = h.shape[0]
    tm = _tile(n, 512, 16)

    def rows(width):
        return pl.BlockSpec((tm, width), lambda i: (i, 0))

    def const(shape):
        return pl.BlockSpec(shape, lambda i: (0,) * len(shape), pipeline_mode=pl.Buffered(1))

    return pl.pallas_call(
        _ple_kernel,
        grid=(n // tm,),
        in_specs=[rows(D_MODEL), rows(PLE_DIM), const((PLE_DIM, D_MODEL)),
                  const((D_MODEL, D_MODEL)), const((1, D_MODEL))],
        out_specs=[rows(D_MODEL), rows(D_MODEL)],
        out_shape=[jax.ShapeDtypeStruct((n, D_MODEL), F32), jax.ShapeDtypeStruct((n, D_MODEL), BF16)],
        compiler_params=_cparams(("parallel",)),
        name="ple",
    )(h, p, wp, wg, next_g)


def _reorder_w_in(w_in):
    sizes = (1024, 1024, 1024, 1024, 256, 256, 1024, 512, 512, 1024, 2 * GLA_RANK, 1024, 3 * D_MODEL)
    names = ("a_val", "a_glu", "a_z", "b_q", "b_k", "b_v", "b_z", "c_q", "c_k", "c_v", "c_lr", "c_z", "m")
    parts, start = {}, 0
    for name, s in zip(names, sizes):
        parts[name] = w_in[..., start:start + s]
        start += s
    pad = jnp.zeros(w_in.shape[:-1] + (128 - 2 * GLA_RANK,), w_in.dtype)
    order = ("a_val", "a_glu", "a_z", "b_q", "b_z", "c_v", "c_z", "c_q", "c_k", "m", "b_k", "b_v", "c_lr")
    w = jnp.concatenate([parts[k] for k in order] + [pad], axis=-1)
    assert w.shape[-1] == U_COLS
    return w.astype(BF16)


def _rope_tables(T):
    n_rows = T // GRID_W
    row_idx = jnp.repeat(jnp.arange(n_rows), GRID_W).astype(F32)
    col_idx = jnp.tile(jnp.arange(GRID_W), n_rows).astype(F32)
    inv = ROPE_THETA ** (-jnp.arange(ROPE_PAIRS, dtype=F32) / ROPE_PAIRS)
    ang_r = row_idx[:, None] * inv
    ang_c = col_idx[:, None] * inv
    cos = jnp.concatenate([jnp.cos(ang_r), jnp.cos(ang_r), jnp.cos(ang_c), jnp.cos(ang_c)], axis=-1)
    sin = jnp.concatenate([-jnp.sin(ang_r), jnp.sin(ang_r), -jnp.sin(ang_c), jnp.sin(ang_c)], axis=-1)
    return cos, sin


def _gla_gate_weights(gate_w2):
    depth = gate_w2.shape[0]
    w = jnp.zeros((depth, 2, 128, GLA_HEADS * GLA_DK), F32)
    for d in range(2):
        w = w.at[:, d, d * GLA_RANK:(d + 1) * GLA_RANK, :].set(gate_w2[:, d])
    return w.astype(BF16)


def kernel(x_prompt, x_sample, p_prompt, p_sample, pre_norm_g, post_norm_g, w_in, conv_w, conv_b, conv_ln_g, conv_ln_b, q_norm_g, k_norm_g, gla_gate_w2, gla_gate_b, gla_norm_g, w_branch, w_out, w_ple, w_ple_gate):
    depth = w_in.shape[0]
    nb_prompt = x_prompt.shape[0]
    x = jnp.concatenate([x_prompt, x_sample], axis=0)
    B, T, _ = x.shape
    n = B * T
    p = jnp.concatenate([p_prompt, p_sample], axis=1).reshape(depth, n, PLE_DIM)

    w_in_r = _reorder_w_in(w_in)
    w_branch_b = w_branch.astype(BF16)
    w_out_b = w_out.astype(BF16)
    w_ple_b = w_ple.astype(BF16)
    w_gate_b = w_ple_gate.astype(BF16)
    w2p = _gla_gate_weights(gla_gate_w2)
    conv_w_p = jnp.pad(conv_w, ((0, 0), (0, 1), (0, 0)))
    cos, sin = _rope_tables(T)
    tk = _tile(T, 512, 128)

    h = x.reshape(n, D_MODEL)
    xn = _prenorm(h, pre_norm_g[0][None])
    for l in range(depth):
        u = _inproj(xn, w_in_r[l])
        u3 = u.reshape(B, T, U_COLS)
        o_a = _conv_branch(u3, conv_w_p[l], conv_b[l][None], conv_ln_g[l][None], conv_ln_b[l][None])
        qT, k, vT = _qkprep(u3, cos, sin, q_norm_g[l][None], k_norm_g[l][None], tk)
        o_b = _attention(qT, k, vT)
        o_cf, o_cb = _gla(u3, w2p[l], gla_gate_b[l][:, None, :])
        h = _merge(o_a.reshape(n, BRANCH_W), o_b.reshape(n, BRANCH_W), o_cf.reshape(n, BRANCH_W),
                   o_cb.reshape(n, BRANCH_W), u, h, w_branch_b[l], w_out_b[l],
                   gla_norm_g[l][None], post_norm_g[l][None])
        next_g = pre_norm_g[(l + 1) % depth][None]
        h, xn = _ple(h, p[l], w_ple_b[l], w_gate_b[l], next_g)
    y = h.reshape(B, T, D_MODEL)
    return (y[:nb_prompt], y[nb_prompt:])
```

```python
import functools
import math

import jax
import jax.numpy as jnp
from jax import lax
from jax.experimental import pallas as pl
from jax.experimental.pallas import tpu as pltpu

F32 = jnp.float32
BF16 = jnp.bfloat16

D_MODEL = 2048
GRID_W = 64
PLE_DIM = 256
BRANCH_W = 1024
EPS = 1e-6
CONV_K = 31
CONV_HALO = 16
N_Q_HEADS = 8
N_KV_HEADS = 2
Q_PER_KV = N_Q_HEADS // N_KV_HEADS
HEAD_DIM = 128
ROPE_PAIRS = HEAD_DIM // 4
ROPE_THETA = 10000.0
GLA_HEADS = 4
GLA_DK = 128
GLA_DV = 256
GLA_RANK = 16
GLA_GATE_NORM = 16.0
GLA_CHUNK = 64

A_VAL, A_GLU, A_Z = 0, 1024, 2048
B_Q, B_Z = 3072, 4096
C_V, C_Z = 5120, 6144
C_Q, C_K = 7168, 7680
M_LOG = 8192
B_KV = 14336
C_LR = 14848
U_COLS = 14976

VMEM_LIMIT = 56 * 1024 * 1024


def _cparams(sem, vmem=VMEM_LIMIT):
    return pltpu.CompilerParams(dimension_semantics=sem, vmem_limit_bytes=vmem)


def _tile(n, cap, mult=8):
    t = min(cap, n)
    while t > mult and (n % t or t % mult):
        t -= mult
    assert n % t == 0, (n, cap, mult)
    return t


def _sigmoid(x):
    return 1.0 / (1.0 + jnp.exp(-x))


def _silu(x):
    return x * _sigmoid(x)


def _prenorm_kernel(x_ref, g_ref, o_ref):
    x = x_ref[...]
    ms = jnp.mean(x * x, axis=-1, keepdims=True)
    o_ref[...] = (x * lax.rsqrt(ms + EPS) * g_ref[...]).astype(BF16)


def _prenorm(x, g):
    n = x.shape[0]
    tm = _tile(n, 256)
    return pl.pallas_call(
        _prenorm_kernel,
        grid=(n // tm,),
        in_specs=[pl.BlockSpec((tm, D_MODEL), lambda i: (i, 0)),
                  pl.BlockSpec((1, D_MODEL), lambda i: (0, 0))],
        out_specs=pl.BlockSpec((tm, D_MODEL), lambda i: (i, 0)),
        out_shape=jax.ShapeDtypeStruct((n, D_MODEL), BF16),
        compiler_params=_cparams(("parallel",)),
        name="prenorm",
    )(x, g)


def _inproj_kernel(x_ref, w_ref, u_ref):
    u_ref[...] = jnp.dot(x_ref[...], w_ref[...], preferred_element_type=F32).astype(BF16)


def _inproj(xn, w):
    n = xn.shape[0]
    tm = _tile(n, 1024, 16)
    tn = 1152
    return pl.pallas_call(
        _inproj_kernel,
        grid=(n // tm, U_COLS // tn),
        in_specs=[pl.BlockSpec((tm, D_MODEL), lambda i, j: (i, 0)),
                  pl.BlockSpec((D_MODEL, tn), lambda i, j: (0, j))],
        out_specs=pl.BlockSpec((tm, tn), lambda i, j: (i, j)),
        out_shape=jax.ShapeDtypeStruct((n, U_COLS), BF16),
        compiler_params=_cparams(("parallel", "arbitrary")),
        name="inproj",
    )(xn, w)


def _conv_kernel(val_ref, glu_ref, z_ref, pval_ref, pglu_ref, nval_ref, nglu_ref,
                 w_ref, cb_ref, lg_ref, lb_ref, o_ref, abuf, cbuf):
    i = pl.program_id(1)
    n = pl.num_programs(1)
    tt = val_ref.shape[0]
    H = CONV_HALO

    def glu(v_ref, g_ref):
        return v_ref[...].astype(F32) * _sigmoid(g_ref[...].astype(F32))

    abuf[0:H, :] = glu(pval_ref, pglu_ref) * (i > 0).astype(F32)
    abuf[H:H + tt, :] = glu(val_ref, glu_ref)
    abuf[H + tt:2 * H + tt, :] = glu(nval_ref, nglu_ref) * (i < n - 1).astype(F32)

    rows = 64 if tt % 64 == 0 else tt
    base = H - CONV_K // 2
    for c in range(BRANCH_W // 128):
        cs = slice(c * 128, (c + 1) * 128)
        for r in range(tt // rows):
            acc = jnp.zeros((rows, 128), F32)
            for k in range(CONV_K):
                lo = base + r * rows + k
                acc = acc + abuf[lo:lo + rows, cs] * w_ref[k:k + 1, cs]
            cbuf[r * rows:(r + 1) * rows, cs] = acc + cb_ref[:, cs]

    x = cbuf[...]
    mu = jnp.mean(x, axis=-1, keepdims=True)
    xc = x - mu
    var = jnp.mean(xc * xc, axis=-1, keepdims=True)
    y = xc * lax.rsqrt(var + EPS) * lg_ref[...] + lb_ref[...]
    o_ref[...] = (_silu(y) * _silu(z_ref[...].astype(F32))).astype(BF16)


def _conv_branch(u3, conv_w, conv_b, ln_g, ln_b):
    B, T, _ = u3.shape
    tt = _tile(T, 128, 16)
    H = CONV_HALO
    hb = tt // H
    nh = T // H

    def cur(col):
        return pl.BlockSpec((None, tt, BRANCH_W), lambda b, i: (b, i, col))

    def prev(col):
        return pl.BlockSpec((None, H, BRANCH_W), lambda b, i: (b, jnp.maximum(i * hb - 1, 0), col))

    def nxt(col):
        return pl.BlockSpec((None, H, BRANCH_W), lambda b, i: (b, jnp.minimum((i + 1) * hb, nh - 1), col))

    vec = pl.BlockSpec((1, BRANCH_W), lambda b, i: (0, 0))
    return pl.pallas_call(
        _conv_kernel,
        grid=(B, T // tt),
        in_specs=[cur(A_VAL // BRANCH_W), cur(A_GLU // BRANCH_W), cur(A_Z // BRANCH_W),
                  prev(A_VAL // BRANCH_W), prev(A_GLU // BRANCH_W),
                  nxt(A_VAL // BRANCH_W), nxt(A_GLU // BRANCH_W),
                  pl.BlockSpec((CONV_K + 1, BRANCH_W), lambda b, i: (0, 0)), vec, vec, vec],
        out_specs=pl.BlockSpec((None, tt, BRANCH_W), lambda b, i: (b, i, 0)),
        out_shape=jax.ShapeDtypeStruct((B, T, BRANCH_W), BF16),
        scratch_shapes=[pltpu.VMEM((tt + 2 * H, BRANCH_W), F32), pltpu.VMEM((tt, BRANCH_W), F32)],
        compiler_params=_cparams(("parallel", "parallel")),
        name="conv",
    )(u3, u3, u3, u3, u3, u3, u3, conv_w, conv_b, ln_g, ln_b)


def _norm_rope(x, g, cos, sin, first_half):
    ms = jnp.mean(x * x, axis=-1, keepdims=True)
    y = x * lax.rsqrt(ms + EPS) * g
    swapped = jnp.where(first_half, pltpu.roll(y, HEAD_DIM - ROPE_PAIRS, 1), pltpu.roll(y, ROPE_PAIRS, 1))
    return y * cos + swapped * sin


def _qkprep_kernel(q_ref, kv_ref, cos_ref, sin_ref, gq_ref, gk_ref, qT_ref, k_ref, vT_ref):
    tt = q_ref.shape[0]
    cos = cos_ref[...]
    sin = sin_ref[...]
    lane = lax.broadcasted_iota(jnp.int32, (tt, HEAD_DIM), 1)
    first_half = (lane % (2 * ROPE_PAIRS)) < ROPE_PAIRS
    qscale = HEAD_DIM ** -0.5 * math.log2(math.e)
    for h in range(N_Q_HEADS):
        x = q_ref[:, h * HEAD_DIM:(h + 1) * HEAD_DIM].astype(F32)
        y = _norm_rope(x, gq_ref[...], cos, sin, first_half) * qscale
        qT_ref[h] = y.T.astype(BF16)
    for g in range(N_KV_HEADS):
        x = kv_ref[:, g * HEAD_DIM:(g + 1) * HEAD_DIM].astype(F32)
        k_ref[g] = _norm_rope(x, gk_ref[...], cos, sin, first_half).astype(BF16)
        v = kv_ref[:, (N_KV_HEADS + g) * HEAD_DIM:(N_KV_HEADS + g + 1) * HEAD_DIM].astype(F32)
        vT_ref[g, 0] = v.T.astype(BF16)


def _qkprep(u3, cos, sin, gq, gk, tk):
    B, T, _ = u3.shape
    qw = N_Q_HEADS * HEAD_DIM
    kvw = 2 * N_KV_HEADS * HEAD_DIM
    vec = pl.BlockSpec((1, HEAD_DIM), lambda b, i: (0, 0))
    tab = pl.BlockSpec((tk, HEAD_DIM), lambda b, i: (i, 0))
    return pl.pallas_call(
        _qkprep_kernel,
        grid=(B, T // tk),
        in_specs=[pl.BlockSpec((None, tk, qw), lambda b, i: (b, i, B_Q // qw)),
                  pl.BlockSpec((None, tk, kvw), lambda b, i: (b, i, B_KV // kvw)),
                  tab, tab, vec, vec],
        out_specs=[pl.BlockSpec((None, N_Q_HEADS, HEAD_DIM, tk), lambda b, i: (b, 0, 0, i)),
                   pl.BlockSpec((None, N_KV_HEADS, tk, HEAD_DIM), lambda b, i: (b, 0, i, 0)),
                   pl.BlockSpec((None, N_KV_HEADS, 1, HEAD_DIM, tk), lambda b, i: (b, 0, i, 0, 0))],
        out_shape=[jax.ShapeDtypeStruct((B, N_Q_HEADS, HEAD_DIM, T), BF16),
                   jax.ShapeDtypeStruct((B, N_KV_HEADS, T, HEAD_DIM), BF16),
                   jax.ShapeDtypeStruct((B, N_KV_HEADS, T // tk, HEAD_DIM, tk), BF16)],
        compiler_params=_cparams(("parallel", "parallel")),
        name="qkprep",
    )(u3, u3, cos, sin, gq, gk)


def _col_reduce(op, x):
    rows, cols = x.shape
    if rows % 64 == 0:
        x = op(x.reshape(8, rows // 8, cols), axis=0)
    return op(x, axis=0, keepdims=True)


def _attn_kernel(qT_ref, k_ref, vT_ref, o_ref, acc_ref, s_ref):
    nk, _, tk = vT_ref.shape
    tq = qT_ref.shape[2]
    unroll = 4 if nk % 4 == 0 else 2
    acc_ref[...] = jnp.zeros_like(acc_ref)

    def scores(j, slot):
        k = k_ref[pl.ds(pl.multiple_of(j * tk, tk), tk), :]
        tile_max = []
        for h in range(Q_PER_KV):
            sT = jnp.dot(k, qT_ref[h], preferred_element_type=F32)
            s_ref[slot, h] = sT
            tile_max.append(_col_reduce(jnp.max, sT))
        return tuple(tile_max)

    def consume(j, slot, tile_max, ms, ls):
        vT = vT_ref[j]
        new_ms, new_ls = [], []
        for h in range(Q_PER_KV):
            m_new = jnp.maximum(ms[h], tile_max[h])
            alpha = jnp.exp2(ms[h] - m_new)
            p = jnp.exp2(s_ref[slot, h] - m_new)
            new_ls.append(alpha * ls[h] + _col_reduce(jnp.sum, p))
            pv = jnp.dot(vT, p.astype(BF16), preferred_element_type=F32)
            acc_ref[h] = alpha * acc_ref[h] + pv
            new_ms.append(m_new)
        return tuple(new_ms), tuple(new_ls)

    def body(i, carry):
        tile_max, ms, ls = carry
        j = unroll * i
        for u in range(unroll):
            nxt = j + u + 1
            if u == unroll - 1:
                nxt = jnp.minimum(nxt, nk - 1)
            next_max = scores(nxt, (u + 1) % 2)
            ms, ls = consume(j + u, u % 2, tile_max, ms, ls)
            tile_max = next_max
        return tile_max, ms, ls

    m0 = tuple(jnp.full((1, tq), -jnp.inf, F32) for _ in range(Q_PER_KV))
    l0 = tuple(jnp.zeros((1, tq), F32) for _ in range(Q_PER_KV))
    _, _, ls = lax.fori_loop(0, nk // unroll, body, (scores(0, 0), m0, l0))
    for h in range(Q_PER_KV):
        o = acc_ref[h] * (1.0 / ls[h])
        o_ref[:, h * HEAD_DIM:(h + 1) * HEAD_DIM] = o.T.astype(BF16)


def _attention(qT, k, vT):
    B, _, _, T = qT.shape
    nk, tk = vT.shape[2], vT.shape[4]
    tq = _tile(T, 256, 128)
    gw = Q_PER_KV * HEAD_DIM
    return pl.pallas_call(
        _attn_kernel,
        grid=(B, N_KV_HEADS, T // tq),
        in_specs=[pl.BlockSpec((None, Q_PER_KV, HEAD_DIM, tq), lambda b, g, i: (b, g, 0, i)),
                  pl.BlockSpec((None, None, T, HEAD_DIM), lambda b, g, i: (b, g, 0, 0)),
                  pl.BlockSpec((None, None, nk, HEAD_DIM, tk), lambda b, g, i: (b, g, 0, 0, 0))],
        out_specs=pl.BlockSpec((None, tq, gw), lambda b, g, i: (b, i, g)),
        out_shape=jax.ShapeDtypeStruct((B, T, N_Q_HEADS * HEAD_DIM), BF16),
        scratch_shapes=[pltpu.VMEM((Q_PER_KV, HEAD_DIM, tq), F32),
                        pltpu.VMEM((2, Q_PER_KV, tk, tq), F32)],
        compiler_params=_cparams(("parallel", "parallel", "arbitrary")),
        name="attention",
    )(qT, k, vT)


def _log_sigmoid(z):
    return jnp.minimum(z, 0.0) - jnp.log(1.0 + jnp.exp(-jnp.abs(z)))


def _gla_kernel(qf_ref, kf_ref, vf_ref, lrf_ref, qb_ref, kb_ref, vb_ref, lrb_ref,
                w2_ref, gb_ref, of_ref, ob_ref, s_ref):
    i = pl.program_id(1)
    tb = qf_ref.shape[0]
    L = GLA_CHUNK
    nchunk = tb // L

    @pl.when(i == 0)
    def _():
        s_ref[...] = jnp.zeros_like(s_ref)

    row = lax.broadcasted_iota(jnp.int32, (L, L), 0)
    col = lax.broadcasted_iota(jnp.int32, (L, L), 1)
    masks = (row >= col, col >= row)
    ones = tuple(jnp.where(m, 1.0, 0.0).astype(BF16) for m in masks)
    qscale = GLA_DK ** -0.5
    dirs = ((qf_ref, kf_ref, vf_ref, lrf_ref, of_ref), (qb_ref, kb_ref, vb_ref, lrb_ref, ob_ref))

    log_gate = []
    for d in range(2):
        z = jnp.dot(dirs[d][3][...], w2_ref[d], preferred_element_type=F32) + gb_ref[d]
        log_gate.append(_log_sigmoid(z) * (1.0 / GLA_GATE_NORM))

    for step in range(nchunk):
        for d in range(2):
            q_ref, k_ref, v_ref, _, o_ref = dirs[d]
            c = step if d == 0 else nchunk - 1 - step
            rows = slice(c * L, (c + 1) * L)
            g = log_gate[d][rows, :]
            g_hi = g.astype(BF16)
            g_lo = (g - g_hi.astype(F32)).astype(BF16)
            b_all = (jnp.dot(ones[d], g_hi, preferred_element_type=F32)
                     + jnp.dot(ones[d], g_lo, preferred_element_type=F32))
            for h in range(GLA_HEADS):
                ks = slice(h * GLA_DK, (h + 1) * GLA_DK)
                vs = slice(h * GLA_DV, (h + 1) * GLA_DV)
                b = b_all[:, ks]
                b_tot = b[L - 1:L, :] if d == 0 else b[0:1, :]
                q = q_ref[rows, ks].astype(F32) * qscale
                k = k_ref[rows, ks].astype(F32)
                v = v_ref[rows, vs]
                q_t = (q * jnp.exp(b)).astype(BF16)
                k_t = (k * jnp.exp(-b)).astype(BF16)
                k_d = (k * jnp.exp(b_tot - b)).astype(BF16)
                att = lax.dot_general(q_t, k_t, (((1,), (1,)), ((), ())), preferred_element_type=F32)
                att = jnp.where(masks[d], att, 0.0).astype(BF16)
                sT = s_ref[d, h]
                o = (jnp.dot(att, v, preferred_element_type=F32)
                     + lax.dot_general(q_t, sT.astype(BF16), (((1,), (1,)), ((), ())),
                                       preferred_element_type=F32))
                o_ref[rows, vs] = o.astype(o_ref.dtype)
                dsT = lax.dot_general(v, k_d, (((0,), (0,)), ((), ())), preferred_element_type=F32)
                s_ref[d, h] = sT * jnp.exp(b_tot) + dsT


def _gla(u3, w2p, gb):
    B, T, _ = u3.shape
    tb = _tile(T, 256, GLA_CHUNK)
    nb = T // tb
    qk_w = GLA_HEADS * GLA_DK
    v_w = GLA_HEADS * GLA_DV

    def specs(idx):
        return [pl.BlockSpec((None, tb, qk_w), lambda b, i: (b, idx(i), C_Q // qk_w)),
                pl.BlockSpec((None, tb, qk_w), lambda b, i: (b, idx(i), C_K // qk_w)),
                pl.BlockSpec((None, tb, v_w), lambda b, i: (b, idx(i), C_V // v_w)),
                pl.BlockSpec((None, tb, 128), lambda b, i: (b, idx(i), C_LR // 128))]

    fwd = lambda i: i
    bwd = lambda i: nb - 1 - i
    out_sd = jax.ShapeDtypeStruct((B, T, v_w), BF16)
    return pl.pallas_call(
        _gla_kernel,
        grid=(B, nb),
        in_specs=specs(fwd) + specs(bwd) + [
            pl.BlockSpec((2, 128, qk_w), lambda b, i: (0, 0, 0)),
            pl.BlockSpec((2, 1, qk_w), lambda b, i: (0, 0, 0))],
        out_specs=[pl.BlockSpec((None, tb, v_w), lambda b, i: (b, i, 0)),
                   pl.BlockSpec((None, tb, v_w), lambda b, i: (b, nb - 1 - i, 0))],
        out_shape=[out_sd, out_sd],
        scratch_shapes=[pltpu.VMEM((2, GLA_HEADS, GLA_DV, GLA_DK), F32)],
        compiler_params=_cparams(("parallel", "arbitrary")),
        name="gla",
    )(u3, u3, u3, u3, u3, u3, u3, u3, w2p, gb)


def _merge_kernel(oa_ref, ob_ref, bz_ref, cf_ref, cb_ref, cz_ref, m0_ref, m1_ref, m2_ref, h_ref,
                  wb_ref, wo_ref, gn_ref, pg_ref, o_ref):
    o_a = oa_ref[...]
    o_b = (ob_ref[...].astype(F32) * _silu(bz_ref[...].astype(F32))).astype(BF16)
    parts = []
    for h in range(GLA_HEADS):
        vs = slice(h * GLA_DV, (h + 1) * GLA_DV)
        x = cf_ref[:, vs].astype(F32) + cb_ref[:, vs].astype(F32)
        ms = jnp.mean(x * x, axis=-1, keepdims=True)
        y = x * lax.rsqrt(ms + EPS) * gn_ref[...]
        parts.append((y * _silu(cz_ref[:, vs].astype(F32))).astype(BF16))
    o_c = jnp.concatenate(parts, axis=-1)

    merged = None
    for n, (o, m_ref) in enumerate(((o_a, m0_ref), (o_b, m1_ref), (o_c, m2_ref))):
        term = _sigmoid(m_ref[...].astype(F32)) * jnp.dot(o, wb_ref[n], preferred_element_type=F32)
        merged = term if merged is None else merged + term
    out = jnp.dot(merged.astype(BF16), wo_ref[...], preferred_element_type=F32)
    ms = jnp.mean(out * out, axis=-1, keepdims=True)
    o_ref[...] = h_ref[...] + out * lax.rsqrt(ms + EPS) * pg_ref[...]


def _merge(o_a, o_b, o_cf, o_cb, u, h, wb, wo, gla_g, post_g):
    n = h.shape[0]
    tm = _tile(n, 256, 16)
    W = BRANCH_W

    def rows(width, col=0):
        return pl.BlockSpec((tm, width), lambda i: (i, col))

    def const(shape):
        return pl.BlockSpec(shape, lambda i: (0,) * len(shape), pipeline_mode=pl.Buffered(1))

    return pl.pallas_call(
        _merge_kernel,
        grid=(n // tm,),
        in_specs=[rows(W), rows(W), rows(W, B_Z // W), rows(W), rows(W), rows(W, C_Z // W),
                  rows(D_MODEL, M_LOG // D_MODEL), rows(D_MODEL, M_LOG // D_MODEL + 1),
                  rows(D_MODEL, M_LOG // D_MODEL + 2), rows(D_MODEL),
                  const((3, W, D_MODEL)), const((D_MODEL, D_MODEL)),
                  const((1, GLA_DV)), const((1, D_MODEL))],
        out_specs=rows(D_MODEL),
        out_shape=jax.ShapeDtypeStruct((n, D_MODEL), F32),
        compiler_params=_cparams(("parallel",)),
        name="merge",
    )(o_a, o_b, u, o_cf, o_cb, u, u, u, u, h, wb, wo, gla_g, post_g)


def _ple_kernel(h_ref, p_ref, wp_ref, wg_ref, ng_ref, o_ref, xn_ref):
    h = h_ref[...]
    e = jnp.dot(p_ref[...].astype(BF16), wp_ref[...], preferred_element_type=F32)
    gate = _sigmoid(jnp.dot(h.astype(BF16), wg_ref[...], preferred_element_type=F32))
    h2 = h + gate * e
    o_ref[...] = h2
    ms = jnp.mean(h2 * h2, axis=-1, keepdims=True)
    xn_ref[...] = (h2 * lax.rsqrt(ms + EPS) * ng_ref[...]).astype(BF16)


def _ple(h, p, wp, wg, next_g):
    n = h.shape[0]
    tm = _tile(n, 512, 16)

    def rows(width):
        return pl.BlockSpec((tm, width), lambda i: (i, 0))

    def const(shape):
        return pl.BlockSpec(shape, lambda i: (0,) * len(shape), pipeline_mode=pl.Buffered(1))

    return pl.pallas_call(
        _ple_kernel,
        grid=(n // tm,),
        in_specs=[rows(D_MODEL), rows(PLE_DIM), const((PLE_DIM, D_MODEL)),
                  const((D_MODEL, D_MODEL)), const((1, D_MODEL))],
        out_specs=[rows(D_MODEL), rows(D_MODEL)],
        out_shape=[jax.ShapeDtypeStruct((n, D_MODEL), F32), jax.ShapeDtypeStruct((n, D_MODEL), BF16)],
        compiler_params=_cparams(("parallel",)),
        name="ple",
    )(h, p, wp, wg, next_g)


def _reorder_w_in(w_in):
    sizes = (1024, 1024, 1024, 1024, 256, 256, 1024, 512, 512, 1024, 2 * GLA_RANK, 1024, 3 * D_MODEL)
    names = ("a_val", "a_glu", "a_z", "b_q", "b_k", "b_v", "b_z", "c_q", "c_k", "c_v", "c_lr", "c_z", "m")
    parts, start = {}, 0
    for name, s in zip(names, sizes):
        parts[name] = w_in[..., start:start + s]
        start += s
    pad = jnp.zeros(w_in.shape[:-1] + (128 - 2 * GLA_RANK,), w_in.dtype)
    order = ("a_val", "a_glu", "a_z", "b_q", "b_z", "c_v", "c_z", "c_q", "c_k", "m", "b_k", "b_v", "c_lr")
    w = jnp.concatenate([parts[k] for k in order] + [pad], axis=-1)
    assert w.shape[-1] == U_COLS
    return w.astype(BF16)


def _rope_tables(T):
    n_rows = T // GRID_W
    row_idx = jnp.repeat(jnp.arange(n_rows), GRID_W).astype(F32)
    col_idx = jnp.tile(jnp.arange(GRID_W), n_rows).astype(F32)
    inv = ROPE_THETA ** (-jnp.arange(ROPE_PAIRS, dtype=F32) / ROPE_PAIRS)
    ang_r = row_idx[:, None] * inv
    ang_c = col_idx[:, None] * inv
    cos = jnp.concatenate([jnp.cos(ang_r), jnp.cos(ang_r), jnp.cos(ang_c), jnp.cos(ang_c)], axis=-1)
    sin = jnp.concatenate([-jnp.sin(ang_r), jnp.sin(ang_r), -jnp.sin(ang_c), jnp.sin(ang_c)], axis=-1)
    return cos, sin


def _gla_gate_weights(gate_w2):
    depth = gate_w2.shape[0]
    w = jnp.zeros((depth, 2, 128, GLA_HEADS * GLA_DK), F32)
    for d in range(2):
        w = w.at[:, d, d * GLA_RANK:(d + 1) * GLA_RANK, :].set(gate_w2[:, d])
    return w.astype(BF16)


def kernel(x_prompt, x_sample, p_prompt, p_sample, pre_norm_g, post_norm_g, w_in, conv_w, conv_b, conv_ln_g, conv_ln_b, q_norm_g, k_norm_g, gla_gate_w2, gla_gate_b, gla_norm_g, w_branch, w_out, w_ple, w_ple_gate):
    depth = w_in.shape[0]
    nb_prompt = x_prompt.shape[0]
    x = jnp.concatenate([x_prompt, x_sample], axis=0)
    B, T, _ = x.shape
    n = B * T
    p = jnp.concatenate([p_prompt, p_sample], axis=1).reshape(depth, n, PLE_DIM)

    w_in_r = _reorder_w_in(w_in)
    w_branch_b = w_branch.astype(BF16)
    w_out_b = w_out.astype(BF16)
    w_ple_b = w_ple.astype(BF16)
    w_gate_b = w_ple_gate.astype(BF16)
    w2p = _gla_gate_weights(gla_gate_w2)
    conv_w_p = jnp.pad(conv_w, ((0, 0), (0, 1), (0, 0)))
    cos, sin = _rope_tables(T)
    tk = _tile(T // 2, 512, 128)

    h = x.reshape(n, D_MODEL)
    xn = _prenorm(h, pre_norm_g[0][None])
    for l in range(depth):
        u = _inproj(xn, w_in_r[l])
        u3 = u.reshape(B, T, U_COLS)
        o_a = _conv_branch(u3, conv_w_p[l], conv_b[l][None], conv_ln_g[l][None], conv_ln_b[l][None])
        qT, k, vT = _qkprep(u3, cos, sin, q_norm_g[l][None], k_norm_g[l][None], tk)
        o_b = _attention(qT, k, vT)
        o_cf, o_cb = _gla(u3, w2p[l], gla_gate_b[l][:, None, :])
        h = _merge(o_a.reshape(n, BRANCH_W), o_b.reshape(n, BRANCH_W), o_cf.reshape(n, BRANCH_W),
                   o_cb.reshape(n, BRANCH_W), u, h, w_branch_b[l], w_out_b[l],
                   gla_norm_g[l][None], post_norm_g[l][None])
        next_g = pre_norm_g[(l + 1) % depth][None]
        h, xn = _ple(h, p[l], w_ple_b[l], w_gate_b[l], next_g)
    y = h.reshape(B, T, D_MODEL)
    return (y[:nb_prompt], y[nb_prompt:])
```

```python
import functools
import math

import jax
import jax.numpy as jnp
from jax import lax
from jax.experimental import pallas as pl
from jax.experimental.pallas import tpu as pltpu

F32 = jnp.float32
BF16 = jnp.bfloat16

D_MODEL = 2048
GRID_W = 64
PLE_DIM = 256
BRANCH_W = 1024
EPS = 1e-6
CONV_K = 31
CONV_HALO = 16
N_Q_HEADS = 8
N_KV_HEADS = 2
Q_PER_KV = N_Q_HEADS // N_KV_HEADS
HEAD_DIM = 128
V_PAD_ROWS = 16
ROPE_PAIRS = HEAD_DIM // 4
ROPE_THETA = 10000.0
GLA_HEADS = 4
GLA_DK = 128
GLA_DV = 256
GLA_RANK = 16
GLA_GATE_NORM = 16.0
GLA_CHUNK = 64

A_VAL, A_GLU, A_Z = 0, 1024, 2048
B_Q, B_Z = 3072, 4096
C_V, C_Z = 5120, 6144
C_Q, C_K = 7168, 7680
M_LOG = 8192
B_KV = 14336
C_LR = 14848
U_COLS = 14976

VMEM_LIMIT = 56 * 1024 * 1024


def _cparams(sem, vmem=VMEM_LIMIT):
    return pltpu.CompilerParams(dimension_semantics=sem, vmem_limit_bytes=vmem)


def _tile(n, cap, mult=8):
    t = min(cap, n)
    while t > mult and (n % t or t % mult):
        t -= mult
    assert n % t == 0, (n, cap, mult)
    return t


def _sigmoid(x):
    return 1.0 / (1.0 + jnp.exp(-x))


def _silu(x):
    return x * _sigmoid(x)


def _prenorm_kernel(x_ref, g_ref, o_ref):
    x = x_ref[...]
    ms = jnp.mean(x * x, axis=-1, keepdims=True)
    o_ref[...] = (x * lax.rsqrt(ms + EPS) * g_ref[...]).astype(BF16)


def _prenorm(x, g):
    n = x.shape[0]
    tm = _tile(n, 256)
    return pl.pallas_call(
        _prenorm_kernel,
        grid=(n // tm,),
        in_specs=[pl.BlockSpec((tm, D_MODEL), lambda i: (i, 0)),
                  pl.BlockSpec((1, D_MODEL), lambda i: (0, 0))],
        out_specs=pl.BlockSpec((tm, D_MODEL), lambda i: (i, 0)),
        out_shape=jax.ShapeDtypeStruct((n, D_MODEL), BF16),
        compiler_params=_cparams(("parallel",)),
        name="prenorm",
    )(x, g)


def _inproj_kernel(x_ref, w_ref, u_ref):
    u_ref[...] = jnp.dot(x_ref[...], w_ref[...], preferred_element_type=F32).astype(BF16)


def _inproj(xn, w):
    n = xn.shape[0]
    tm = _tile(n, 1024, 16)
    tn = 1152
    return pl.pallas_call(
        _inproj_kernel,
        grid=(n // tm, U_COLS // tn),
        in_specs=[pl.BlockSpec((tm, D_MODEL), lambda i, j: (i, 0)),
                  pl.BlockSpec((D_MODEL, tn), lambda i, j: (0, j))],
        out_specs=pl.BlockSpec((tm, tn), lambda i, j: (i, j)),
        out_shape=jax.ShapeDtypeStruct((n, U_COLS), BF16),
        compiler_params=_cparams(("parallel", "arbitrary")),
        name="inproj",
    )(xn, w)


def _conv_kernel(val_ref, glu_ref, z_ref, pval_ref, pglu_ref, nval_ref, nglu_ref,
                 w_ref, cb_ref, lg_ref, lb_ref, o_ref, abuf, cbuf):
    i = pl.program_id(1)
    n = pl.num_programs(1)
    tt = val_ref.shape[0]
    H = CONV_HALO

    def glu(v_ref, g_ref):
        return v_ref[...].astype(F32) * _sigmoid(g_ref[...].astype(F32))

    abuf[0:H, :] = glu(pval_ref, pglu_ref) * (i > 0).astype(F32)
    abuf[H:H + tt, :] = glu(val_ref, glu_ref)
    abuf[H + tt:2 * H + tt, :] = glu(nval_ref, nglu_ref) * (i < n - 1).astype(F32)

    assert H - CONV_K // 2 == 1 and CONV_K < 32
    for c in range(BRANCH_W // 128):
        cs = slice(c * 128, (c + 1) * 128)
        acc = None
        for r in range(8):
            part = None
            for q in range(4):
                j = 8 * q + r
                if j == 0:
                    continue
                term = abuf[8 * q:8 * q + tt + 8, cs] * w_ref[j - 1:j, cs]
                part = term if part is None else part + term
            shifted = part[r:r + tt, :]
            acc = shifted if acc is None else acc + shifted
        cbuf[:, cs] = acc + cb_ref[:, cs]

    x = cbuf[...]
    mu = jnp.mean(x, axis=-1, keepdims=True)
    xc = x - mu
    var = jnp.mean(xc * xc, axis=-1, keepdims=True)
    y = xc * lax.rsqrt(var + EPS) * lg_ref[...] + lb_ref[...]
    o_ref[...] = (_silu(y) * _silu(z_ref[...].astype(F32))).astype(BF16)


def _conv_branch(u3, conv_w, conv_b, ln_g, ln_b):
    B, T, _ = u3.shape
    tt = _tile(T, 128, 16)
    H = CONV_HALO
    hb = tt // H
    nh = T // H

    def cur(col):
        return pl.BlockSpec((None, tt, BRANCH_W), lambda b, i: (b, i, col))

    def prev(col):
        return pl.BlockSpec((None, H, BRANCH_W), lambda b, i: (b, jnp.maximum(i * hb - 1, 0), col))

    def nxt(col):
        return pl.BlockSpec((None, H, BRANCH_W), lambda b, i: (b, jnp.minimum((i + 1) * hb, nh - 1), col))

    vec = pl.BlockSpec((1, BRANCH_W), lambda b, i: (0, 0))
    return pl.pallas_call(
        _conv_kernel,
        grid=(B, T // tt),
        in_specs=[cur(A_VAL // BRANCH_W), cur(A_GLU // BRANCH_W), cur(A_Z // BRANCH_W),
                  prev(A_VAL // BRANCH_W), prev(A_GLU // BRANCH_W),
                  nxt(A_VAL // BRANCH_W), nxt(A_GLU // BRANCH_W),
                  pl.BlockSpec((CONV_K + 1, BRANCH_W), lambda b, i: (0, 0)), vec, vec, vec],
        out_specs=pl.BlockSpec((None, tt, BRANCH_W), lambda b, i: (b, i, 0)),
        out_shape=jax.ShapeDtypeStruct((B, T, BRANCH_W), BF16),
        scratch_shapes=[pltpu.VMEM((tt + 2 * H, BRANCH_W), F32), pltpu.VMEM((tt, BRANCH_W), F32)],
        compiler_params=_cparams(("parallel", "parallel")),
        name="conv",
    )(u3, u3, u3, u3, u3, u3, u3, conv_w, conv_b, ln_g, ln_b)


def _norm_rope(x, g, cos, sin, first_half):
    ms = jnp.mean(x * x, axis=-1, keepdims=True)
    y = x * lax.rsqrt(ms + EPS) * g
    swapped = jnp.where(first_half, pltpu.roll(y, HEAD_DIM - ROPE_PAIRS, 1), pltpu.roll(y, ROPE_PAIRS, 1))
    return y * cos + swapped * sin


def _qkprep_kernel(q_ref, kv_ref, cos_ref, sin_ref, gq_ref, gk_ref, qT_ref, k_ref, vT_ref):
    tt = q_ref.shape[0]
    cos = cos_ref[...]
    sin = sin_ref[...]
    lane = lax.broadcasted_iota(jnp.int32, (tt, HEAD_DIM), 1)
    first_half = (lane % (2 * ROPE_PAIRS)) < ROPE_PAIRS
    qscale = HEAD_DIM ** -0.5 * math.log2(math.e)
    for h in range(N_Q_HEADS):
        x = q_ref[:, h * HEAD_DIM:(h + 1) * HEAD_DIM].astype(F32)
        y = _norm_rope(x, gq_ref[...], cos, sin, first_half) * qscale
        qT_ref[h] = y.T.astype(BF16)
    for g in range(N_KV_HEADS):
        x = kv_ref[:, g * HEAD_DIM:(g + 1) * HEAD_DIM].astype(F32)
        k_ref[g] = _norm_rope(x, gk_ref[...], cos, sin, first_half).astype(BF16)
        v = kv_ref[:, (N_KV_HEADS + g) * HEAD_DIM:(N_KV_HEADS + g + 1) * HEAD_DIM].astype(F32)
        vT_ref[g, 0, :HEAD_DIM, :] = v.T.astype(BF16)
        pad_row = lax.broadcasted_iota(jnp.int32, (V_PAD_ROWS, tt), 0)
        vT_ref[g, 0, HEAD_DIM:, :] = jnp.where(pad_row == 0, 1.0, 0.0).astype(BF16)


def _qkprep(u3, cos, sin, gq, gk, tk):
    B, T, _ = u3.shape
    qw = N_Q_HEADS * HEAD_DIM
    kvw = 2 * N_KV_HEADS * HEAD_DIM
    vec = pl.BlockSpec((1, HEAD_DIM), lambda b, i: (0, 0))
    tab = pl.BlockSpec((tk, HEAD_DIM), lambda b, i: (i, 0))
    return pl.pallas_call(
        _qkprep_kernel,
        grid=(B, T // tk),
        in_specs=[pl.BlockSpec((None, tk, qw), lambda b, i: (b, i, B_Q // qw)),
                  pl.BlockSpec((None, tk, kvw), lambda b, i: (b, i, B_KV // kvw)),
                  tab, tab, vec, vec],
        out_specs=[pl.BlockSpec((None, N_Q_HEADS, HEAD_DIM, tk), lambda b, i: (b, 0, 0, i)),
                   pl.BlockSpec((None, N_KV_HEADS, tk, HEAD_DIM), lambda b, i: (b, 0, i, 0)),
                   pl.BlockSpec((None, N_KV_HEADS, 1, HEAD_DIM + V_PAD_ROWS, tk), lambda b, i: (b, 0, i, 0, 0))],
        out_shape=[jax.ShapeDtypeStruct((B, N_Q_HEADS, HEAD_DIM, T), BF16),
                   jax.ShapeDtypeStruct((B, N_KV_HEADS, T, HEAD_DIM), BF16),
                   jax.ShapeDtypeStruct((B, N_KV_HEADS, T // tk, HEAD_DIM + V_PAD_ROWS, tk), BF16)],
        compiler_params=_cparams(("parallel", "parallel")),
        name="qkprep",
    )(u3, u3, cos, sin, gq, gk)


def _col_reduce(op, x):
    rows, cols = x.shape
    if rows % 64 == 0:
        x = op(x.reshape(8, rows // 8, cols), axis=0)
    return op(x, axis=0, keepdims=True)


def _attn_kernel(qT_ref, k_ref, vT_ref, o_ref, acc_ref, s_ref):
    nk, _, tk = vT_ref.shape
    tq = qT_ref.shape[2]
    unroll = 4 if nk % 4 == 0 else 2
    acc_ref[...] = jnp.zeros_like(acc_ref)

    def scores(j, slot):
        k = k_ref[pl.ds(pl.multiple_of(j * tk, tk), tk), :]
        tile_max = []
        for h in range(Q_PER_KV):
            sT = jnp.dot(k, qT_ref[h], preferred_element_type=F32)
            s_ref[slot, h] = sT
            tile_max.append(_col_reduce(jnp.max, sT))
        return tuple(tile_max)

    def consume(j, slot, tile_max, ms):
        vT = vT_ref[j]
        new_ms = []
        for h in range(Q_PER_KV):
            m_new = jnp.maximum(ms[h], tile_max[h])
            alpha = jnp.exp2(ms[h] - m_new)
            p = jnp.exp2(s_ref[slot, h] - m_new).astype(BF16)
            pv = jnp.dot(vT, p, preferred_element_type=F32)
            acc_ref[h] = alpha * acc_ref[h] + pv
            new_ms.append(m_new)
        return tuple(new_ms)

    def body(i, carry):
        tile_max, ms = carry
        j = unroll * i
        for u in range(unroll):
            nxt = j + u + 1
            if u == unroll - 1:
                nxt = jnp.minimum(nxt, nk - 1)
            next_max = scores(nxt, (u + 1) % 2)
            ms = consume(j + u, u % 2, tile_max, ms)
            tile_max = next_max
        return tile_max, ms

    m0 = tuple(jnp.full((1, tq), -jnp.inf, F32) for _ in range(Q_PER_KV))
    lax.fori_loop(0, nk // unroll, body, (scores(0, 0), m0))
    for h in range(Q_PER_KV):
        a = acc_ref[h]
        o = a[:HEAD_DIM] * (1.0 / a[HEAD_DIM:HEAD_DIM + 1])
        o_ref[:, h * HEAD_DIM:(h + 1) * HEAD_DIM] = o.T.astype(BF16)


def _attention(qT, k, vT):
    B, _, _, T = qT.shape
    nk, hv, tk = vT.shape[2], vT.shape[3], vT.shape[4]
    tq = _tile(T, 256, 128)
    gw = Q_PER_KV * HEAD_DIM
    return pl.pallas_call(
        _attn_kernel,
        grid=(B, N_KV_HEADS, T // tq),
        in_specs=[pl.BlockSpec((None, Q_PER_KV, HEAD_DIM, tq), lambda b, g, i: (b, g, 0, i)),
                  pl.BlockSpec((None, None, T, HEAD_DIM), lambda b, g, i: (b, g, 0, 0)),
                  pl.BlockSpec((None, None, nk, hv, tk), lambda b, g, i: (b, g, 0, 0, 0))],
        out_specs=pl.BlockSpec((None, tq, gw), lambda b, g, i: (b, i, g)),
        out_shape=jax.ShapeDtypeStruct((B, T, N_Q_HEADS * HEAD_DIM), BF16),
        scratch_shapes=[pltpu.VMEM((Q_PER_KV, hv, tq), F32),
                        pltpu.VMEM((2, Q_PER_KV, tk, tq), F32)],
        compiler_params=_cparams(("parallel", "parallel", "arbitrary")),
        name="attention",
    )(qT, k, vT)


def _log_sigmoid(z):
    return jnp.minimum(z, 0.0) - jnp.log(1.0 + jnp.exp(-jnp.abs(z)))


def _gla_kernel(qf_ref, kf_ref, vf_ref, lrf_ref, qb_ref, kb_ref, vb_ref, lrb_ref,
                w2_ref, gb_ref, of_ref, ob_ref, s_ref):
    i = pl.program_id(1)
    tb = qf_ref.shape[0]
    L = GLA_CHUNK
    nchunk = tb // L

    @pl.when(i == 0)
    def _():
        s_ref[...] = jnp.zeros_like(s_ref)

    row = lax.broadcasted_iota(jnp.int32, (L, L), 0)
    col = lax.broadcasted_iota(jnp.int32, (L, L), 1)
    masks = (row >= col, col >= row)
    ones = tuple(jnp.where(m, 1.0, 0.0).astype(BF16) for m in masks)
    qscale = GLA_DK ** -0.5
    dirs = ((qf_ref, kf_ref, vf_ref, lrf_ref, of_ref), (qb_ref, kb_ref, vb_ref, lrb_ref, ob_ref))

    log_gate = []
    for d in range(2):
        z = jnp.dot(dirs[d][3][...], w2_ref[d], preferred_element_type=F32) + gb_ref[d]
        log_gate.append(_log_sigmoid(z) * (1.0 / GLA_GATE_NORM))

    cum = {}
    for step in range(nchunk):
        for d in range(2):
            c = step if d == 0 else nchunk - 1 - step
            g = log_gate[d][c * L:(c + 1) * L, :]
            g_hi = g.astype(BF16)
            g_lo = (g - g_hi.astype(F32)).astype(BF16)
            cum[step, d] = (jnp.dot(ones[d], g_hi, preferred_element_type=F32)
                            + jnp.dot(ones[d], g_lo, preferred_element_type=F32))
    scaled = {}
    for step in range(nchunk):
        for d in range(2):
            q_ref, k_ref = dirs[d][0], dirs[d][1]
            c = step if d == 0 else nchunk - 1 - step
            rows = slice(c * L, (c + 1) * L)
            for h in range(GLA_HEADS):
                ks = slice(h * GLA_DK, (h + 1) * GLA_DK)
                b = cum[step, d][:, ks]
                b_tot = b[L - 1:L, :] if d == 0 else b[0:1, :]
                q = q_ref[rows, ks].astype(F32) * qscale
                k = k_ref[rows, ks].astype(F32)
                q_t = (q * jnp.exp(b)).astype(BF16)
                k_t = (k * jnp.exp(-b)).astype(BF16)
                k_d = (k * jnp.exp(b_tot - b)).astype(BF16)
                scaled[step, d, h] = (q_t, k_t, k_d, jnp.exp(b_tot))
    pre = {}
    for step in range(nchunk):
        for d in range(2):
            for h in range(GLA_HEADS):
                q_t, k_t, k_d, e_tot = scaled[step, d, h]
                att = lax.dot_general(q_t, k_t, (((1,), (1,)), ((), ())), preferred_element_type=F32)
                att = jnp.where(masks[d], att, 0.0).astype(BF16)
                pre[step, d, h] = (q_t, k_d, att, e_tot)

    for step in range(nchunk):
        for d in range(2):
            v_ref, o_ref = dirs[d][2], dirs[d][4]
            c = step if d == 0 else nchunk - 1 - step
            rows = slice(c * L, (c + 1) * L)
            for h in range(GLA_HEADS):
                vs = slice(h * GLA_DV, (h + 1) * GLA_DV)
                q_t, k_d, att, e_tot = pre[step, d, h]
                v = v_ref[rows, vs]
                sT = s_ref[d, h]
                o = (jnp.dot(att, v, preferred_element_type=F32)
                     + lax.dot_general(q_t, sT.astype(BF16), (((1,), (1,)), ((), ())),
                                       preferred_element_type=F32))
                o_ref[rows, vs] = o.astype(o_ref.dtype)
                dsT = lax.dot_general(v, k_d, (((0,), (0,)), ((), ())), preferred_element_type=F32)
                s_ref[d, h] = sT * e_tot + dsT


def _gla(u3, w2p, gb):
    B, T, _ = u3.shape
    tb = _tile(T, 256, GLA_CHUNK)
    nb = T // tb
    qk_w = GLA_HEADS * GLA_DK
    v_w = GLA_HEADS * GLA_DV

    def specs(idx):
        return [pl.BlockSpec((None, tb, qk_w), lambda b, i: (b, idx(i), C_Q // qk_w)),
                pl.BlockSpec((None, tb, qk_w), lambda b, i: (b, idx(i), C_K // qk_w)),
                pl.BlockSpec((None, tb, v_w), lambda b, i: (b, idx(i), C_V // v_w)),
                pl.BlockSpec((None, tb, 128), lambda b, i: (b, idx(i), C_LR // 128))]

    fwd = lambda i: i
    bwd = lambda i: nb - 1 - i
    out_sd = jax.ShapeDtypeStruct((B, T, v_w), BF16)
    return pl.pallas_call(
        _gla_kernel,
        grid=(B, nb),
        in_specs=specs(fwd) + specs(bwd) + [
            pl.BlockSpec((2, 128, qk_w), lambda b, i: (0, 0, 0)),
            pl.BlockSpec((2, 1, qk_w), lambda b, i: (0, 0, 0))],
        out_specs=[pl.BlockSpec((None, tb, v_w), lambda b, i: (b, i, 0)),
                   pl.BlockSpec((None, tb, v_w), lambda b, i: (b, nb - 1 - i, 0))],
        out_shape=[out_sd, out_sd],
        scratch_shapes=[pltpu.VMEM((2, GLA_HEADS, GLA_DV, GLA_DK), F32)],
        compiler_params=_cparams(("parallel", "arbitrary")),
        name="gla",
    )(u3, u3, u3, u3, u3, u3, u3, u3, w2p, gb)


def _merge_kernel(oa_ref, ob_ref, bz_ref, cf_ref, cb_ref, cz_ref, m0_ref, m1_ref, m2_ref, h_ref,
                  wb_ref, wo_ref, gn_ref, pg_ref, o_ref):
    o_a = oa_ref[...]
    o_b = (ob_ref[...].astype(F32) * _silu(bz_ref[...].astype(F32))).astype(BF16)
    parts = []
    for h in range(GLA_HEADS):
        vs = slice(h * GLA_DV, (h + 1) * GLA_DV)
        x = cf_ref[:, vs].astype(F32) + cb_ref[:, vs].astype(F32)
        ms = jnp.mean(x * x, axis=-1, keepdims=True)
        y = x * lax.rsqrt(ms + EPS) * gn_ref[...]
        parts.append((y * _silu(cz_ref[:, vs].astype(F32))).astype(BF16))
    o_c = jnp.concatenate(parts, axis=-1)

    merged = None
    for n, (o, m_ref) in enumerate(((o_a, m0_ref), (o_b, m1_ref), (o_c, m2_ref))):
        term = _sigmoid(m_ref[...].astype(F32)) * jnp.dot(o, wb_ref[n], preferred_element_type=F32)
        merged = term if merged is None else merged + term
    out = jnp.dot(merged.astype(BF16), wo_ref[...], preferred_element_type=F32)
    ms = jnp.mean(out * out, axis=-1, keepdims=True)
    o_ref[...] = h_ref[...] + out * lax.rsqrt(ms + EPS) * pg_ref[...]


def _merge(o_a, o_b, o_cf, o_cb, u, h, wb, wo, gla_g, post_g):
    n = h.shape[0]
    tm = _tile(n, 256, 16)
    W = BRANCH_W

    def rows(width, col=0):
        return pl.BlockSpec((tm, width), lambda i: (i, col))

    def const(shape):
        return pl.BlockSpec(shape, lambda i: (0,) * len(shape), pipeline_mode=pl.Buffered(1))

    return pl.pallas_call(
        _merge_kernel,
        grid=(n // tm,),
        in_specs=[rows(W), rows(W), rows(W, B_Z // W), rows(W), rows(W), rows(W, C_Z // W),
                  rows(D_MODEL, M_LOG // D_MODEL), rows(D_MODEL, M_LOG // D_MODEL + 1),
                  rows(D_MODEL, M_LOG // D_MODEL + 2), rows(D_MODEL),
                  const((3, W, D_MODEL)), const((D_MODEL, D_MODEL)),
                  const((1, GLA_DV)), const((1, D_MODEL))],
        out_specs=rows(D_MODEL),
        out_shape=jax.ShapeDtypeStruct((n, D_MODEL), F32),
        compiler_params=_cparams(("parallel",)),
        name="merge",
    )(o_a, o_b, u, o_cf, o_cb, u, u, u, u, h, wb, wo, gla_g, post_g)


def _ple_kernel(h_ref, p_ref, wp_ref, wg_ref, ng_ref, o_ref, xn_ref):
    h = h_ref[...]
    e = jnp.dot(p_ref[...].astype(BF16), wp_ref[...], preferred_element_type=F32)
    gate = _sigmoid(jnp.dot(h.astype(BF16), wg_ref[...], preferred_element_type=F32))
    h2 = h + gate * e
    o_ref[...] = h2
    ms = jnp.mean(h2 * h2, axis=-1, keepdims=True)
    xn_ref[...] = (h2 * lax.rsqrt(ms + EPS) * ng_ref[...]).astype(BF16)


def _ple(h, p, wp, wg, next_g):
    n = h.shape[0]
    tm = _tile(n, 512, 16)

    def rows(width):
        return pl.BlockSpec((tm, width), lambda i: (i, 0))

    def const(shape):
        return pl.BlockSpec(shape, lambda i: (0,) * len(shape), pipeline_mode=pl.Buffered(1))

    return pl.pallas_call(
        _ple_kernel,
        grid=(n // tm,),
        in_specs=[rows(D_MODEL), rows(PLE_DIM), const((PLE_DIM, D_MODEL)),
                  const((D_MODEL, D_MODEL)), const((1, D_MODEL))],
        out_specs=[rows(D_MODEL), rows(D_MODEL)],
        out_shape=[jax.ShapeDtypeStruct((n, D_MODEL), F32), jax.ShapeDtypeStruct((n, D_MODEL), BF16)],
        compiler_params=_cparams(("parallel",)),
        name="ple",
    )(h, p, wp, wg, next_g)


def _reorder_w_in(w_in):
    sizes = (1024, 1024, 1024, 1024, 256, 256, 1024, 512, 512, 1024, 2 * GLA_RANK, 1024, 3 * D_MODEL)
    names = ("a_val", "a_glu", "a_z", "b_q", "b_k", "b_v", "b_z", "c_q", "c_k", "c_v", "c_lr", "c_z", "m")
    parts, start = {}, 0
    for name, s in zip(names, sizes):
        parts[name] = w_in[..., start:start + s]
        start += s
    pad = jnp.zeros(w_in.shape[:-1] + (128 - 2 * GLA_RANK,), w_in.dtype)
    order = ("a_val", "a_glu", "a_z", "b_q", "b_z", "c_v", "c_z", "c_q", "c_k", "m", "b_k", "b_v", "c_lr")
    w = jnp.concatenate([parts[k] for k in order] + [pad], axis=-1)
    assert w.shape[-1] == U_COLS
    return w.astype(BF16)


def _rope_tables(T):
    n_rows = T // GRID_W
    row_idx = jnp.repeat(jnp.arange(n_rows), GRID_W).astype(F32)
    col_idx = jnp.tile(jnp.arange(GRID_W), n_rows).astype(F32)
    inv = ROPE_THETA ** (-jnp.arange(ROPE_PAIRS, dtype=F32) / ROPE_PAIRS)
    ang_r = row_idx[:, None] * inv
    ang_c = col_idx[:, None] * inv
    cos = jnp.concatenate([jnp.cos(ang_r), jnp.cos(ang_r), jnp.cos(ang_c), jnp.cos(ang_c)], axis=-1)
    sin = jnp.concatenate([-jnp.sin(ang_r), jnp.sin(ang_r), -jnp.sin(ang_c), jnp.sin(ang_c)], axis=-1)
    return cos, sin


def _gla_gate_weights(gate_w2):
    depth = gate_w2.shape[0]
    w = jnp.zeros((depth, 2, 128, GLA_HEADS * GLA_DK), F32)
    for d in range(2):
        w = w.at[:, d, d * GLA_RANK:(d + 1) * GLA_RANK, :].set(gate_w2[:, d])
    return w.astype(BF16)


def kernel(x_prompt, x_sample, p_prompt, p_sample, pre_norm_g, post_norm_g, w_in, conv_w, conv_b, conv_ln_g, conv_ln_b, q_norm_g, k_norm_g, gla_gate_w2, gla_gate_b, gla_norm_g, w_branch, w_out, w_ple, w_ple_gate):
    depth = w_in.shape[0]
    nb_prompt = x_prompt.shape[0]
    x = jnp.concatenate([x_prompt, x_sample], axis=0)
    B, T, _ = x.shape
    n = B * T
    p = jnp.concatenate([p_prompt, p_sample], axis=1).reshape(depth, n, PLE_DIM)

    w_in_r = _reorder_w_in(w_in)
    w_branch_b = w_branch.astype(BF16)
    w_out_b = w_out.astype(BF16)
    w_ple_b = w_ple.astype(BF16)
    w_gate_b = w_ple_gate.astype(BF16)
    w2p = _gla_gate_weights(gla_gate_w2)
    conv_w_p = jnp.pad(conv_w, ((0, 0), (0, 1), (0, 0)))
    cos, sin = _rope_tables(T)
    tk = _tile(T // 2, 512, 128)

    h = x.reshape(n, D_MODEL)
    xn = _prenorm(h, pre_norm_g[0][None])
    for l in range(depth):
        u = _inproj(xn, w_in_r[l])
        u3 = u.reshape(B, T, U_COLS)
        o_a = _conv_branch(u3, conv_w_p[l], conv_b[l][None], conv_ln_g[l][None], conv_ln_b[l][None])
        qT, k, vT = _qkprep(u3, cos, sin, q_norm_g[l][None], k_norm_g[l][None], tk)
        o_b = _attention(qT, k, vT)
        o_cf, o_cb = _gla(u3, w2p[l], gla_gate_b[l][:, None, :])
        h = _merge(o_a.reshape(n, BRANCH_W), o_b.reshape(n, BRANCH_W), o_cf.reshape(n, BRANCH_W),
                   o_cb.reshape(n, BRANCH_W), u, h, w_branch_b[l], w_out_b[l],
                   gla_norm_g[l][None], post_norm_g[l][None])
        next_g = pre_norm_g[(l + 1) % depth][None]
        h, xn = _ple(h, p[l], w_ple_b[l], w_gate_b[l], next_g)
    y = h.reshape(B, T, D_MODEL)
    return (y[:nb_prompt], y[nb_prompt:])
```

```python
import functools
import math

import jax
import jax.numpy as jnp
from jax import lax
from jax.experimental import pallas as pl
from jax.experimental.pallas import tpu as pltpu

F32 = jnp.float32
BF16 = jnp.bfloat16

D_MODEL = 2048
GRID_W = 64
PLE_DIM = 256
BRANCH_W = 1024
EPS = 1e-6
CONV_K = 31
CONV_HALO = 16
N_Q_HEADS = 8
N_KV_HEADS = 2
Q_PER_KV = N_Q_HEADS // N_KV_HEADS
HEAD_DIM = 128
V_PAD_ROWS = 16
ROPE_PAIRS = HEAD_DIM // 4
ROPE_THETA = 10000.0
GLA_HEADS = 4
GLA_DK = 128
GLA_DV = 256
GLA_RANK = 16
GLA_GATE_NORM = 16.0
GLA_CHUNK = 64

A_VAL, A_GLU, A_Z = 0, 1024, 2048
B_Q, B_Z = 3072, 4096
C_V, C_Z = 5120, 6144
C_Q, C_K = 7168, 7680
M_LOG = 8192
B_KV = 14336
C_LR = 14848
MXU_COLS = 256
INPROJ_TN = 6 * MXU_COLS
U_COLS = 15360

VMEM_LIMIT = 56 * 1024 * 1024


def _cparams(sem, vmem=VMEM_LIMIT):
    return pltpu.CompilerParams(dimension_semantics=sem, vmem_limit_bytes=vmem)


def _tile(n, cap, mult=8):
    t = min(cap, n)
    while t > mult and (n % t or t % mult):
        t -= mult
    assert n % t == 0, (n, cap, mult)
    return t


def _sigmoid(x):
    return 1.0 / (1.0 + jnp.exp(-x))


def _silu(x):
    return x * _sigmoid(x)


def _prenorm_kernel(x_ref, g_ref, o_ref):
    x = x_ref[...]
    ms = jnp.mean(x * x, axis=-1, keepdims=True)
    o_ref[...] = (x * lax.rsqrt(ms + EPS) * g_ref[...]).astype(BF16)


def _prenorm(x, g):
    n = x.shape[0]
    tm = _tile(n, 256)
    return pl.pallas_call(
        _prenorm_kernel,
        grid=(n // tm,),
        in_specs=[pl.BlockSpec((tm, D_MODEL), lambda i: (i, 0)),
                  pl.BlockSpec((1, D_MODEL), lambda i: (0, 0))],
        out_specs=pl.BlockSpec((tm, D_MODEL), lambda i: (i, 0)),
        out_shape=jax.ShapeDtypeStruct((n, D_MODEL), BF16),
        compiler_params=_cparams(("parallel",)),
        name="prenorm",
    )(x, g)


def _inproj_kernel(x_ref, w_ref, u_ref):
    u_ref[...] = jnp.dot(x_ref[...], w_ref[...], preferred_element_type=F32).astype(BF16)


def _inproj(xn, w):
    n = xn.shape[0]
    tm = _tile(n, 1024, 16)
    tn = INPROJ_TN
    return pl.pallas_call(
        _inproj_kernel,
        grid=(n // tm, U_COLS // tn),
        in_specs=[pl.BlockSpec((tm, D_MODEL), lambda i, j: (i, 0)),
                  pl.BlockSpec((D_MODEL, tn), lambda i, j: (0, j))],
        out_specs=pl.BlockSpec((tm, tn), lambda i, j: (i, j)),
        out_shape=jax.ShapeDtypeStruct((n, U_COLS), BF16),
        compiler_params=_cparams(("parallel", "arbitrary")),
        name="inproj",
    )(xn, w)


def _conv_kernel(val_ref, glu_ref, z_ref, pval_ref, pglu_ref, nval_ref, nglu_ref,
                 w_ref, cb_ref, lg_ref, lb_ref, o_ref, abuf, cbuf):
    i = pl.program_id(1)
    n = pl.num_programs(1)
    tt = val_ref.shape[0]
    H = CONV_HALO

    def glu(v_ref, g_ref):
        return v_ref[...].astype(F32) * _sigmoid(g_ref[...].astype(F32))

    abuf[0:H, :] = glu(pval_ref, pglu_ref) * (i > 0).astype(F32)
    abuf[H:H + tt, :] = glu(val_ref, glu_ref)
    abuf[H + tt:2 * H + tt, :] = glu(nval_ref, nglu_ref) * (i < n - 1).astype(F32)

    assert H - CONV_K // 2 == 1 and CONV_K < 32
    for c in range(BRANCH_W // 128):
        cs = slice(c * 128, (c + 1) * 128)
        acc = None
        for r in range(8):
            part = None
            for q in range(4):
                j = 8 * q + r
                if j == 0:
                    continue
                term = abuf[8 * q:8 * q + tt + 8, cs] * w_ref[j - 1:j, cs]
                part = term if part is None else part + term
            shifted = part[r:r + tt, :]
            acc = shifted if acc is None else acc + shifted
        cbuf[:, cs] = acc + cb_ref[:, cs]

    x = cbuf[...]
    mu = jnp.mean(x, axis=-1, keepdims=True)
    xc = x - mu
    var = jnp.mean(xc * xc, axis=-1, keepdims=True)
    y = xc * lax.rsqrt(var + EPS) * lg_ref[...] + lb_ref[...]
    o_ref[...] = (_silu(y) * _silu(z_ref[...].astype(F32))).astype(BF16)


def _conv_branch(u3, conv_w, conv_b, ln_g, ln_b):
    B, T, _ = u3.shape
    tt = _tile(T, 128, 16)
    H = CONV_HALO
    hb = tt // H
    nh = T // H

    def cur(col):
        return pl.BlockSpec((None, tt, BRANCH_W), lambda b, i: (b, i, col))

    def prev(col):
        return pl.BlockSpec((None, H, BRANCH_W), lambda b, i: (b, jnp.maximum(i * hb - 1, 0), col))

    def nxt(col):
        return pl.BlockSpec((None, H, BRANCH_W), lambda b, i: (b, jnp.minimum((i + 1) * hb, nh - 1), col))

    vec = pl.BlockSpec((1, BRANCH_W), lambda b, i: (0, 0))
    return pl.pallas_call(
        _conv_kernel,
        grid=(B, T // tt),
        in_specs=[cur(A_VAL // BRANCH_W), cur(A_GLU // BRANCH_W), cur(A_Z // BRANCH_W),
                  prev(A_VAL // BRANCH_W), prev(A_GLU // BRANCH_W),
                  nxt(A_VAL // BRANCH_W), nxt(A_GLU // BRANCH_W),
                  pl.BlockSpec((CONV_K + 1, BRANCH_W), lambda b, i: (0, 0)), vec, vec, vec],
        out_specs=pl.BlockSpec((None, tt, BRANCH_W), lambda b, i: (b, i, 0)),
        out_shape=jax.ShapeDtypeStruct((B, T, BRANCH_W), BF16),
        scratch_shapes=[pltpu.VMEM((tt + 2 * H, BRANCH_W), F32), pltpu.VMEM((tt, BRANCH_W), F32)],
        compiler_params=_cparams(("parallel", "parallel")),
        name="conv",
    )(u3, u3, u3, u3, u3, u3, u3, conv_w, conv_b, ln_g, ln_b)


def _norm_rope(x, g, cos, sin, first_half):
    ms = jnp.mean(x * x, axis=-1, keepdims=True)
    y = x * lax.rsqrt(ms + EPS) * g
    swapped = jnp.where(first_half, pltpu.roll(y, HEAD_DIM - ROPE_PAIRS, 1), pltpu.roll(y, ROPE_PAIRS, 1))
    return y * cos + swapped * sin


def _qkprep_kernel(q_ref, kv_ref, cos_ref, sin_ref, gq_ref, gk_ref, qo_ref, k_ref, vT_ref):
    tt = q_ref.shape[0]
    cos = cos_ref[...]
    sin = sin_ref[...]
    lane = lax.broadcasted_iota(jnp.int32, (tt, HEAD_DIM), 1)
    first_half = (lane % (2 * ROPE_PAIRS)) < ROPE_PAIRS
    qscale = HEAD_DIM ** -0.5 * math.log2(math.e)
    for h in range(N_Q_HEADS):
        x = q_ref[:, h * HEAD_DIM:(h + 1) * HEAD_DIM].astype(F32)
        y = _norm_rope(x, gq_ref[...], cos, sin, first_half) * qscale
        qo_ref[h] = y.astype(BF16)
    for g in range(N_KV_HEADS):
        x = kv_ref[:, g * HEAD_DIM:(g + 1) * HEAD_DIM].astype(F32)
        k_ref[g] = _norm_rope(x, gk_ref[...], cos, sin, first_half).astype(BF16)
        v = kv_ref[:, (N_KV_HEADS + g) * HEAD_DIM:(N_KV_HEADS + g + 1) * HEAD_DIM].astype(F32)
        vT_ref[g, 0, :HEAD_DIM, :] = v.T.astype(BF16)
        pad_row = lax.broadcasted_iota(jnp.int32, (V_PAD_ROWS, tt), 0)
        vT_ref[g, 0, HEAD_DIM:, :] = jnp.where(pad_row == 0, 1.0, 0.0).astype(BF16)


def _qkprep(u3, cos, sin, gq, gk, tk):
    B, T, _ = u3.shape
    qw = N_Q_HEADS * HEAD_DIM
    kvw = 2 * N_KV_HEADS * HEAD_DIM
    vec = pl.BlockSpec((1, HEAD_DIM), lambda b, i: (0, 0))
    tab = pl.BlockSpec((tk, HEAD_DIM), lambda b, i: (i, 0))
    return pl.pallas_call(
        _qkprep_kernel,
        grid=(B, T // tk),
        in_specs=[pl.BlockSpec((None, tk, qw), lambda b, i: (b, i, B_Q // qw)),
                  pl.BlockSpec((None, tk, kvw), lambda b, i: (b, i, B_KV // kvw)),
                  tab, tab, vec, vec],
        out_specs=[pl.BlockSpec((None, N_Q_HEADS, tk, HEAD_DIM), lambda b, i: (b, 0, i, 0)),
                   pl.BlockSpec((None, N_KV_HEADS, tk, HEAD_DIM), lambda b, i: (b, 0, i, 0)),
                   pl.BlockSpec((None, N_KV_HEADS, 1, HEAD_DIM + V_PAD_ROWS, tk), lambda b, i: (b, 0, i, 0, 0))],
        out_shape=[jax.ShapeDtypeStruct((B, N_Q_HEADS, T, HEAD_DIM), BF16),
                   jax.ShapeDtypeStruct((B, N_KV_HEADS, T, HEAD_DIM), BF16),
                   jax.ShapeDtypeStruct((B, N_KV_HEADS, T // tk, HEAD_DIM + V_PAD_ROWS, tk), BF16)],
        compiler_params=_cparams(("parallel", "parallel")),
        name="qkprep",
    )(u3, u3, cos, sin, gq, gk)


def _col_reduce(op, x):
    rows, cols = x.shape
    if rows % 64 == 0:
        x = op(x.reshape(8, rows // 8, cols), axis=0)
    return op(x, axis=0, keepdims=True)


def _attn_kernel(q_ref, k_ref, vT_ref, o_ref, acc_ref, s_ref):
    nk, _, tk = vT_ref.shape
    tq = q_ref.shape[1]
    unroll = next(u for u in (8, 4, 2) if nk % u == 0)
    acc_ref[...] = jnp.zeros_like(acc_ref)

    def scores(j, slot):
        k = k_ref[pl.ds(pl.multiple_of(j * tk, tk), tk), :]
        tile_max = []
        for h in range(Q_PER_KV):
            sT = lax.dot_general(k, q_ref[h], (((1,), (1,)), ((), ())),
                                 preferred_element_type=F32)
            s_ref[slot, h] = sT
            tile_max.append(_col_reduce(jnp.max, sT))
        return tuple(tile_max)

    def consume(j, slot, tile_max, ms):
        vT = vT_ref[j]
        new_ms = []
        for h in range(Q_PER_KV):
            m_new = jnp.maximum(ms[h], tile_max[h])
            alpha = jnp.exp2(ms[h] - m_new)
            p = jnp.exp2(s_ref[slot, h] - m_new).astype(BF16)
            pv = jnp.dot(vT, p, preferred_element_type=F32)
            acc_ref[h] = alpha * acc_ref[h] + pv
            new_ms.append(m_new)
        return tuple(new_ms)

    def body(i, carry):
        tile_max, ms = carry
        j = unroll * i
        for u in range(unroll):
            nxt = j + u + 1
            if u == unroll - 1:
                nxt = jnp.minimum(nxt, nk - 1)
            next_max = scores(nxt, (u + 1) % 2)
            ms = consume(j + u, u % 2, tile_max, ms)
            tile_max = next_max
        return tile_max, ms

    m0 = tuple(jnp.full((1, tq), -jnp.inf, F32) for _ in range(Q_PER_KV))
    lax.fori_loop(0, nk // unroll, body, (scores(0, 0), m0))
    for h in range(Q_PER_KV):
        a = acc_ref[h]
        o = a[:HEAD_DIM] * (1.0 / a[HEAD_DIM:HEAD_DIM + 1])
        o_ref[:, h * HEAD_DIM:(h + 1) * HEAD_DIM] = o.T.astype(BF16)


def _attention(q, k, vT):
    B, _, T, _ = q.shape
    nk, hv, tk = vT.shape[2], vT.shape[3], vT.shape[4]
    tq = _tile(T, 256, 128)
    gw = Q_PER_KV * HEAD_DIM
    return pl.pallas_call(
        _attn_kernel,
        grid=(B, N_KV_HEADS, T // tq),
        in_specs=[pl.BlockSpec((None, Q_PER_KV, tq, HEAD_DIM), lambda b, g, i: (b, g, i, 0)),
                  pl.BlockSpec((None, None, T, HEAD_DIM), lambda b, g, i: (b, g, 0, 0)),
                  pl.BlockSpec((None, None, nk, hv, tk), lambda b, g, i: (b, g, 0, 0, 0))],
        out_specs=pl.BlockSpec((None, tq, gw), lambda b, g, i: (b, i, g)),
        out_shape=jax.ShapeDtypeStruct((B, T, N_Q_HEADS * HEAD_DIM), BF16),
        scratch_shapes=[pltpu.VMEM((Q_PER_KV, hv, tq), F32),
                        pltpu.VMEM((2, Q_PER_KV, tk, tq), F32)],
        compiler_params=_cparams(("parallel", "parallel", "arbitrary")),
        name="attention",
    )(q, k, vT)


def _log_sigmoid(z):
    return jnp.minimum(z, 0.0) - jnp.log(1.0 + jnp.exp(-jnp.abs(z)))


def _gla_kernel(qf_ref, kf_ref, vf_ref, lrf_ref, qb_ref, kb_ref, vb_ref, lrb_ref,
                w2_ref, gb_ref, of_ref, ob_ref, s_ref):
    i = pl.program_id(1)
    tb = qf_ref.shape[0]
    L = GLA_CHUNK
    nchunk = tb // L

    @pl.when(i == 0)
    def _():
        s_ref[...] = jnp.zeros_like(s_ref)

    row = lax.broadcasted_iota(jnp.int32, (L, L), 0)
    col = lax.broadcasted_iota(jnp.int32, (L, L), 1)
    masks = (row >= col, col >= row)
    ones = tuple(jnp.where(m, 1.0, 0.0).astype(BF16) for m in masks)
    qscale = GLA_DK ** -0.5
    dirs = ((qf_ref, kf_ref, vf_ref, lrf_ref, of_ref), (qb_ref, kb_ref, vb_ref, lrb_ref, ob_ref))

    log_gate = []
    for d in range(2):
        z = jnp.dot(dirs[d][3][...], w2_ref[d], preferred_element_type=F32) + gb_ref[d]
        log_gate.append(_log_sigmoid(z) * (1.0 / GLA_GATE_NORM))

    cum = {}
    for step in range(nchunk):
        for d in range(2):
            c = step if d == 0 else nchunk - 1 - step
            g = log_gate[d][c * L:(c + 1) * L, :]
            g_hi = g.astype(BF16)
            g_lo = (g - g_hi.astype(F32)).astype(BF16)
            cum[step, d] = (jnp.dot(ones[d], g_hi, preferred_element_type=F32)
                            + jnp.dot(ones[d], g_lo, preferred_element_type=F32))
    scaled = {}
    for step in range(nchunk):
        for d in range(2):
            q_ref, k_ref = dirs[d][0], dirs[d][1]
            c = step if d == 0 else nchunk - 1 - step
            rows = slice(c * L, (c + 1) * L)
            for h in range(GLA_HEADS):
                ks = slice(h * GLA_DK, (h + 1) * GLA_DK)
                b = cum[step, d][:, ks]
                b_tot = b[L - 1:L, :] if d == 0 else b[0:1, :]
                q = q_ref[rows, ks].astype(F32) * qscale
                k = k_ref[rows, ks].astype(F32)
                q_t = (q * jnp.exp(b)).astype(BF16)
                k_t = (k * jnp.exp(-b)).astype(BF16)
                k_d = (k * jnp.exp(b_tot - b)).astype(BF16)
                scaled[step, d, h] = (q_t, k_t, k_d, jnp.exp(b_tot))
    pre = {}
    for step in range(nchunk):
        for d in range(2):
            for h in range(GLA_HEADS):
                q_t, k_t, k_d, e_tot = scaled[step, d, h]
                att = lax.dot_general(q_t, k_t, (((1,), (1,)), ((), ())), preferred_element_type=F32)
                att = jnp.where(masks[d], att, 0.0).astype(BF16)
                pre[step, d, h] = (q_t, k_d, att, e_tot)

    for step in range(nchunk):
        for d in range(2):
            v_ref, o_ref = dirs[d][2], dirs[d][4]
            c = step if d == 0 else nchunk - 1 - step
            rows = slice(c * L, (c + 1) * L)
            for h in range(GLA_HEADS):
                vs = slice(h * GLA_DV, (h + 1) * GLA_DV)
                q_t, k_d, att, e_tot = pre[step, d, h]
                v = v_ref[rows, vs]
                sT = s_ref[d, h]
                o = (jnp.dot(att, v, preferred_element_type=F32)
                     + lax.dot_general(q_t, sT.astype(BF16), (((1,), (1,)), ((), ())),
                                       preferred_element_type=F32))
                o_ref[rows, vs] = o.astype(o_ref.dtype)
                dsT = lax.dot_general(v, k_d, (((0,), (0,)), ((), ())), preferred_element_type=F32)
                s_ref[d, h] = sT * e_tot + dsT


def _gla(u3, w2p, gb):
    B, T, _ = u3.shape
    tb = _tile(T, 256, GLA_CHUNK)
    nb = T // tb
    qk_w = GLA_HEADS * GLA_DK
    v_w = GLA_HEADS * GLA_DV

    def specs(idx):
        return [pl.BlockSpec((None, tb, qk_w), lambda b, i: (b, idx(i), C_Q // qk_w)),
                pl.BlockSpec((None, tb, qk_w), lambda b, i: (b, idx(i), C_K // qk_w)),
                pl.BlockSpec((None, tb, v_w), lambda b, i: (b, idx(i), C_V // v_w)),
                pl.BlockSpec((None, tb, 128), lambda b, i: (b, idx(i), C_LR // 128))]

    fwd = lambda i: i
    bwd = lambda i: nb - 1 - i
    out_sd = jax.ShapeDtypeStruct((B, T, v_w), BF16)
    return pl.pallas_call(
        _gla_kernel,
        grid=(B, nb),
        in_specs=specs(fwd) + specs(bwd) + [
            pl.BlockSpec((2, 128, qk_w), lambda b, i: (0, 0, 0)),
            pl.BlockSpec((2, 1, qk_w), lambda b, i: (0, 0, 0))],
        out_specs=[pl.BlockSpec((None, tb, v_w), lambda b, i: (b, i, 0)),
                   pl.BlockSpec((None, tb, v_w), lambda b, i: (b, nb - 1 - i, 0))],
        out_shape=[out_sd, out_sd],
        scratch_shapes=[pltpu.VMEM((2, GLA_HEADS, GLA_DV, GLA_DK), F32)],
        compiler_params=_cparams(("parallel", "arbitrary")),
        name="gla",
    )(u3, u3, u3, u3, u3, u3, u3, u3, w2p, gb)


def _merge_kernel(oa_ref, ob_ref, bz_ref, cf_ref, cb_ref, cz_ref, m0_ref, m1_ref, m2_ref, h_ref,
                  wb_ref, wo_ref, gn_ref, pg_ref, o_ref):
    o_a = oa_ref[...]
    o_b = (ob_ref[...].astype(F32) * _silu(bz_ref[...].astype(F32))).astype(BF16)
    parts = []
    for h in range(GLA_HEADS):
        vs = slice(h * GLA_DV, (h + 1) * GLA_DV)
        x = cf_ref[:, vs].astype(F32) + cb_ref[:, vs].astype(F32)
        ms = jnp.mean(x * x, axis=-1, keepdims=True)
        y = x * lax.rsqrt(ms + EPS) * gn_ref[...]
        parts.append((y * _silu(cz_ref[:, vs].astype(F32))).astype(BF16))
    o_c = jnp.concatenate(parts, axis=-1)

    merged = None
    for n, (o, m_ref) in enumerate(((o_a, m0_ref), (o_b, m1_ref), (o_c, m2_ref))):
        term = _sigmoid(m_ref[...].astype(F32)) * jnp.dot(o, wb_ref[n], preferred_element_type=F32)
        merged = term if merged is None else merged + term
    out = jnp.dot(merged.astype(BF16), wo_ref[...], preferred_element_type=F32)
    ms = jnp.mean(out * out, axis=-1, keepdims=True)
    o_ref[...] = h_ref[...] + out * lax.rsqrt(ms + EPS) * pg_ref[...]


def _merge(o_a, o_b, o_cf, o_cb, u, h, wb, wo, gla_g, post_g):
    n = h.shape[0]
    tm = _tile(n, 256, 16)
    W = BRANCH_W

    def rows(width, col=0):
        return pl.BlockSpec((tm, width), lambda i: (i, col))

    def const(shape):
        return pl.BlockSpec(shape, lambda i: (0,) * len(shape), pipeline_mode=pl.Buffered(1))

    return pl.pallas_call(
        _merge_kernel,
        grid=(n // tm,),
        in_specs=[rows(W), rows(W), rows(W, B_Z // W), rows(W), rows(W), rows(W, C_Z // W),
                  rows(D_MODEL, M_LOG // D_MODEL), rows(D_MODEL, M_LOG // D_MODEL + 1),
                  rows(D_MODEL, M_LOG // D_MODEL + 2), rows(D_MODEL),
                  const((3, W, D_MODEL)), const((D_MODEL, D_MODEL)),
                  const((1, GLA_DV)), const((1, D_MODEL))],
        out_specs=rows(D_MODEL),
        out_shape=jax.ShapeDtypeStruct((n, D_MODEL), F32),
        compiler_params=_cparams(("parallel",)),
        name="merge",
    )(o_a, o_b, u, o_cf, o_cb, u, u, u, u, h, wb, wo, gla_g, post_g)


def _ple_kernel(h_ref, p_ref, wp_ref, wg_ref, ng_ref, o_ref, xn_ref):
    h = h_ref[...]
    e = jnp.dot(p_ref[...].astype(BF16), wp_ref[...], preferred_element_type=F32)
    gate = _sigmoid(jnp.dot(h.astype(BF16), wg_ref[...], preferred_element_type=F32))
    h2 = h + gate * e
    o_ref[...] = h2
    ms = jnp.mean(h2 * h2, axis=-1, keepdims=True)
    xn_ref[...] = (h2 * lax.rsqrt(ms + EPS) * ng_ref[...]).astype(BF16)


def _ple(h, p, wp, wg, next_g):
    n = h.shape[0]
    tm = _tile(n, 512, 16)

    def rows(width):
        return pl.BlockSpec((tm, width), lambda i: (i, 0))

    def const(shape):
        return pl.BlockSpec(shape, lambda i: (0,) * len(shape), pipeline_mode=pl.Buffered(1))

    return pl.pallas_call(
        _ple_kernel,
        grid=(n // tm,),
        in_specs=[rows(D_MODEL), rows(PLE_DIM), const((PLE_DIM, D_MODEL)),
                  const((D_MODEL, D_MODEL)), const((1, D_MODEL))],
        out_specs=[rows(D_MODEL), rows(D_MODEL)],
        out_shape=[jax.ShapeDtypeStruct((n, D_MODEL), F32), jax.ShapeDtypeStruct((n, D_MODEL), BF16)],
        compiler_params=_cparams(("parallel",)),
        name="ple",
    )(h, p, wp, wg, next_g)


def _reorder_w_in(w_in):
    sizes = (1024, 1024, 1024, 1024, 256, 256, 1024, 512, 512, 1024, 2 * GLA_RANK, 1024, 3 * D_MODEL)
    names = ("a_val", "a_glu", "a_z", "b_q", "b_k", "b_v", "b_z", "c_q", "c_k", "c_v", "c_lr", "c_z", "m")
    parts, start = {}, 0
    for name, s in zip(names, sizes):
        parts[name] = w_in[..., start:start + s]
        start += s
    pad = jnp.zeros(w_in.shape[:-1] + (U_COLS - C_LR - 2 * GLA_RANK,), w_in.dtype)
    order = ("a_val", "a_glu", "a_z", "b_q", "b_z", "c_v", "c_z", "c_q", "c_k", "m", "b_k", "b_v", "c_lr")
    w = jnp.concatenate([parts[k] for k in order] + [pad], axis=-1)
    assert w.shape[-1] == U_COLS
    return w.astype(BF16)


def _rope_tables(T):
    n_rows = T // GRID_W
    row_idx = jnp.repeat(jnp.arange(n_rows), GRID_W).astype(F32)
    col_idx = jnp.tile(jnp.arange(GRID_W), n_rows).astype(F32)
    inv = ROPE_THETA ** (-jnp.arange(ROPE_PAIRS, dtype=F32) / ROPE_PAIRS)
    ang_r = row_idx[:, None] * inv
    ang_c = col_idx[:, None] * inv
    cos = jnp.concatenate([jnp.cos(ang_r), jnp.cos(ang_r), jnp.cos(ang_c), jnp.cos(ang_c)], axis=-1)
    sin = jnp.concatenate([-jnp.sin(ang_r), jnp.sin(ang_r), -jnp.sin(ang_c), jnp.sin(ang_c)], axis=-1)
    return cos, sin


def _gla_gate_weights(gate_w2):
    depth = gate_w2.shape[0]
    w = jnp.zeros((depth, 2, 128, GLA_HEADS * GLA_DK), F32)
    for d in range(2):
        w = w.at[:, d, d * GLA_RANK:(d + 1) * GLA_RANK, :].set(gate_w2[:, d])
    return w.astype(BF16)


def kernel(x_prompt, x_sample, p_prompt, p_sample, pre_norm_g, post_norm_g, w_in, conv_w, conv_b, conv_ln_g, conv_ln_b, q_norm_g, k_norm_g, gla_gate_w2, gla_gate_b, gla_norm_g, w_branch, w_out, w_ple, w_ple_gate):
    depth = w_in.shape[0]
    nb_prompt = x_prompt.shape[0]
    x = jnp.concatenate([x_prompt, x_sample], axis=0)
    B, T, _ = x.shape
    n = B * T
    p = jnp.concatenate([p_prompt, p_sample], axis=1).reshape(depth, n, PLE_DIM)

    w_in_r = _reorder_w_in(w_in)
    w_branch_b = w_branch.astype(BF16)
    w_out_b = w_out.astype(BF16)
    w_ple_b = w_ple.astype(BF16)
    w_gate_b = w_ple_gate.astype(BF16)
    w2p = _gla_gate_weights(gla_gate_w2)
    conv_w_p = jnp.pad(conv_w, ((0, 0), (0, 1), (0, 0)))
    cos, sin = _rope_tables(T)
    tk = _tile(T // 2, 512, 128)

    h = x.reshape(n, D_MODEL)
    xn = _prenorm(h, pre_norm_g[0][None])
    for l in range(depth):
        u = _inproj(xn, w_in_r[l])
        u3 = u.reshape(B, T, U_COLS)
        o_a = _conv_branch(u3, conv_w_p[l], conv_b[l][None], conv_ln_g[l][None], conv_ln_b[l][None])
        q, k, vT = _qkprep(u3, cos, sin, q_norm_g[l][None], k_norm_g[l][None], tk)
        o_b = _attention(q, k, vT)
        o_cf, o_cb = _gla(u3, w2p[l], gla_gate_b[l][:, None, :])
        h = _merge(o_a.reshape(n, BRANCH_W), o_b.reshape(n, BRANCH_W), o_cf.reshape(n, BRANCH_W),
                   o_cb.reshape(n, BRANCH_W), u, h, w_branch_b[l], w_out_b[l],
                   gla_norm_g[l][None], post_norm_g[l][None])
        next_g = pre_norm_g[(l + 1) % depth][None]
        h, xn = _ple(h, p[l], w_ple_b[l], w_gate_b[l], next_g)
    y = h.reshape(B, T, D_MODEL)
    return (y[:nb_prompt], y[nb_prompt:])
```

```python
import functools
import math

import jax
import jax.numpy as jnp
from jax import lax
from jax.experimental import pallas as pl
from jax.experimental.pallas import tpu as pltpu

F32 = jnp.float32
BF16 = jnp.bfloat16

D_MODEL = 2048
GRID_W = 64
PLE_DIM = 256
BRANCH_W = 1024
EPS = 1e-6
CONV_K = 31
CONV_HALO = 16
N_Q_HEADS = 8
N_KV_HEADS = 2
Q_PER_KV = N_Q_HEADS // N_KV_HEADS
HEAD_DIM = 128
V_PAD_ROWS = 16
ROPE_PAIRS = HEAD_DIM // 4
ROPE_THETA = 10000.0
GLA_HEADS = 4
GLA_DK = 128
GLA_DV = 256
GLA_RANK = 16
GLA_GATE_NORM = 16.0
GLA_CHUNK = 64

A_VAL, A_GLU, A_Z = 0, 1024, 2048
B_Q, B_Z = 3072, 4096
C_V, C_Z = 5120, 6144
C_Q, C_K = 7168, 7680
M_LOG = 8192
B_KV = 14336
C_LR = 14848
MXU_COLS = 256
INPROJ_TN = 6 * MXU_COLS
U_COLS = 15360

VMEM_LIMIT = 56 * 1024 * 1024


def _cparams(sem, vmem=VMEM_LIMIT):
    return pltpu.CompilerParams(dimension_semantics=sem, vmem_limit_bytes=vmem)


def _tile(n, cap, mult=8):
    t = min(cap, n)
    while t > mult and (n % t or t % mult):
        t -= mult
    assert n % t == 0, (n, cap, mult)
    return t


def _sigmoid(x):
    return 1.0 / (1.0 + jnp.exp(-x))


def _silu(x):
    return x * _sigmoid(x)


def _split_specs(tm, width, tiles_a, a_first=0, b_first=0, col=0):
    a = pl.BlockSpec((tm, width), lambda i: (a_first + jnp.minimum(i, tiles_a - 1), col))
    b = pl.BlockSpec((tm, width), lambda i: (b_first + jnp.maximum(i - tiles_a, 0), col))
    return a, b


def _pick(tiles_a, a_ref, b_ref):
    return jnp.where(pl.program_id(0) < tiles_a, a_ref[...], b_ref[...])


def _prenorm_kernel(tiles_a, xa_ref, xb_ref, g_ref, o_ref):
    x = _pick(tiles_a, xa_ref, xb_ref)
    ms = jnp.mean(x * x, axis=-1, keepdims=True)
    o_ref[...] = (x * lax.rsqrt(ms + EPS) * g_ref[...]).astype(BF16)


def _prenorm(xa, xb, g, tm):
    n = xa.shape[0] + xb.shape[0]
    tiles_a = xa.shape[0] // tm
    return pl.pallas_call(
        functools.partial(_prenorm_kernel, tiles_a),
        grid=(n // tm,),
        in_specs=[*_split_specs(tm, D_MODEL, tiles_a),
                  pl.BlockSpec((1, D_MODEL), lambda i: (0, 0))],
        out_specs=pl.BlockSpec((tm, D_MODEL), lambda i: (i, 0)),
        out_shape=jax.ShapeDtypeStruct((n, D_MODEL), BF16),
        compiler_params=_cparams(("arbitrary",)),
        name="prenorm",
    )(xa, xb, g)


def _inproj_kernel(x_ref, w_ref, u_ref):
    u_ref[...] = jnp.dot(x_ref[...], w_ref[...], preferred_element_type=F32).astype(BF16)


def _inproj(xn, w):
    n = xn.shape[0]
    tm = _tile(n, 1024, 16)
    tn = INPROJ_TN
    return pl.pallas_call(
        _inproj_kernel,
        grid=(n // tm, U_COLS // tn),
        in_specs=[pl.BlockSpec((tm, D_MODEL), lambda i, j: (i, 0)),
                  pl.BlockSpec((D_MODEL, tn), lambda i, j: (0, j))],
        out_specs=pl.BlockSpec((tm, tn), lambda i, j: (i, j)),
        out_shape=jax.ShapeDtypeStruct((n, U_COLS), BF16),
        compiler_params=_cparams(("parallel", "arbitrary")),
        name="inproj",
    )(xn, w)


def _conv_kernel(val_ref, glu_ref, z_ref, pval_ref, pglu_ref, nval_ref, nglu_ref,
                 w_ref, cb_ref, lg_ref, lb_ref, o_ref, abuf, cbuf):
    i = pl.program_id(1)
    n = pl.num_programs(1)
    tt = val_ref.shape[0]
    H = CONV_HALO

    def glu(v_ref, g_ref):
        return v_ref[...].astype(F32) * _sigmoid(g_ref[...].astype(F32))

    abuf[0:H, :] = glu(pval_ref, pglu_ref) * (i > 0).astype(F32)
    abuf[H:H + tt, :] = glu(val_ref, glu_ref)
    abuf[H + tt:2 * H + tt, :] = glu(nval_ref, nglu_ref) * (i < n - 1).astype(F32)

    assert H - CONV_K // 2 == 1 and CONV_K < 32
    for c in range(BRANCH_W // 128):
        cs = slice(c * 128, (c + 1) * 128)
        acc = None
        for r in range(8):
            part = None
            for q in range(4):
                j = 8 * q + r
                if j == 0:
                    continue
                term = abuf[8 * q:8 * q + tt + 8, cs] * w_ref[j - 1:j, cs]
                part = term if part is None else part + term
            shifted = part[r:r + tt, :]
            acc = shifted if acc is None else acc + shifted
        cbuf[:, cs] = acc + cb_ref[:, cs]

    x = cbuf[...]
    mu = jnp.mean(x, axis=-1, keepdims=True)
    xc = x - mu
    var = jnp.mean(xc * xc, axis=-1, keepdims=True)
    y = xc * lax.rsqrt(var + EPS) * lg_ref[...] + lb_ref[...]
    o_ref[...] = (_silu(y) * _silu(z_ref[...].astype(F32))).astype(BF16)


def _conv_branch(u3, conv_w, conv_b, ln_g, ln_b):
    B, T, _ = u3.shape
    tt = _tile(T, 128, 16)
    H = CONV_HALO
    hb = tt // H
    nh = T // H

    def cur(col):
        return pl.BlockSpec((None, tt, BRANCH_W), lambda b, i: (b, i, col))

    def prev(col):
        return pl.BlockSpec((None, H, BRANCH_W), lambda b, i: (b, jnp.maximum(i * hb - 1, 0), col))

    def nxt(col):
        return pl.BlockSpec((None, H, BRANCH_W), lambda b, i: (b, jnp.minimum((i + 1) * hb, nh - 1), col))

    vec = pl.BlockSpec((1, BRANCH_W), lambda b, i: (0, 0))
    return pl.pallas_call(
        _conv_kernel,
        grid=(B, T // tt),
        in_specs=[cur(A_VAL // BRANCH_W), cur(A_GLU // BRANCH_W), cur(A_Z // BRANCH_W),
                  prev(A_VAL // BRANCH_W), prev(A_GLU // BRANCH_W),
                  nxt(A_VAL // BRANCH_W), nxt(A_GLU // BRANCH_W),
                  pl.BlockSpec((CONV_K + 1, BRANCH_W), lambda b, i: (0, 0)), vec, vec, vec],
        out_specs=pl.BlockSpec((None, tt, BRANCH_W), lambda b, i: (b, i, 0)),
        out_shape=jax.ShapeDtypeStruct((B, T, BRANCH_W), BF16),
        scratch_shapes=[pltpu.VMEM((tt + 2 * H, BRANCH_W), F32), pltpu.VMEM((tt, BRANCH_W), F32)],
        compiler_params=_cparams(("parallel", "parallel")),
        name="conv",
    )(u3, u3, u3, u3, u3, u3, u3, conv_w, conv_b, ln_g, ln_b)


def _norm_rope(x, g, cos, sin):
    ms = jnp.mean(x * x, axis=-1, keepdims=True)
    y = x * lax.rsqrt(ms + EPS) * g
    return y * cos + pltpu.roll(y, HEAD_DIM // 2, 1) * sin


def _qkprep_kernel(q_ref, kv_ref, cos_ref, sin_ref, gq_ref, gk_ref, qo_ref, k_ref, vT_ref):
    tt = q_ref.shape[0]
    cos = cos_ref[...]
    sin = sin_ref[...]
    qscale = HEAD_DIM ** -0.5 * math.log2(math.e)
    for h in range(N_Q_HEADS):
        x = q_ref[:, h * HEAD_DIM:(h + 1) * HEAD_DIM].astype(F32)
        y = _norm_rope(x, gq_ref[...], cos, sin) * qscale
        qo_ref[h] = y.astype(BF16)
    for g in range(N_KV_HEADS):
        x = kv_ref[:, g * HEAD_DIM:(g + 1) * HEAD_DIM].astype(F32)
        k_ref[g] = _norm_rope(x, gk_ref[...], cos, sin).astype(BF16)
        v = kv_ref[:, (N_KV_HEADS + g) * HEAD_DIM:(N_KV_HEADS + g + 1) * HEAD_DIM].astype(F32)
        vT_ref[g, 0, :HEAD_DIM, :] = v.T.astype(BF16)
        pad_row = lax.broadcasted_iota(jnp.int32, (V_PAD_ROWS, tt), 0)
        vT_ref[g, 0, HEAD_DIM:, :] = jnp.where(pad_row == 0, 1.0, 0.0).astype(BF16)


def _qkprep(u3, cos, sin, gq, gk, tk):
    B, T, _ = u3.shape
    qw = N_Q_HEADS * HEAD_DIM
    kvw = 2 * N_KV_HEADS * HEAD_DIM
    vec = pl.BlockSpec((1, HEAD_DIM), lambda b, i: (0, 0))
    tab = pl.BlockSpec((tk, HEAD_DIM), lambda b, i: (i, 0))
    return pl.pallas_call(
        _qkprep_kernel,
        grid=(B, T // tk),
        in_specs=[pl.BlockSpec((None, tk, qw), lambda b, i: (b, i, B_Q // qw)),
                  pl.BlockSpec((None, tk, kvw), lambda b, i: (b, i, B_KV // kvw)),
                  tab, tab, vec, vec],
        out_specs=[pl.BlockSpec((None, N_Q_HEADS, tk, HEAD_DIM), lambda b, i: (b, 0, i, 0)),
                   pl.BlockSpec((None, N_KV_HEADS, tk, HEAD_DIM), lambda b, i: (b, 0, i, 0)),
                   pl.BlockSpec((None, N_KV_HEADS, 1, HEAD_DIM + V_PAD_ROWS, tk), lambda b, i: (b, 0, i, 0, 0))],
        out_shape=[jax.ShapeDtypeStruct((B, N_Q_HEADS, T, HEAD_DIM), BF16),
                   jax.ShapeDtypeStruct((B, N_KV_HEADS, T, HEAD_DIM), BF16),
                   jax.ShapeDtypeStruct((B, N_KV_HEADS, T // tk, HEAD_DIM + V_PAD_ROWS, tk), BF16)],
        compiler_params=_cparams(("parallel", "parallel")),
        name="qkprep",
    )(u3, u3, cos, sin, gq, gk)


def _col_reduce(op, x):
    rows, cols = x.shape
    if rows % 64 == 0:
        x = op(x.reshape(8, rows // 8, cols), axis=0)
    return op(x, axis=0, keepdims=True)


def _attn_kernel(q_ref, k_ref, vT_ref, o_ref, acc_ref, s_ref):
    nk, _, tk = vT_ref.shape
    tq = q_ref.shape[1]
    unroll = next(u for u in (8, 4, 2) if nk % u == 0)
    acc_ref[...] = jnp.zeros_like(acc_ref)

    def scores(j, slot):
        k = k_ref[pl.ds(pl.multiple_of(j * tk, tk), tk), :]
        tile_max = []
        for h in range(Q_PER_KV):
            sT = lax.dot_general(k, q_ref[h], (((1,), (1,)), ((), ())),
                                 preferred_element_type=F32)
            s_ref[slot, h] = sT
            tile_max.append(_col_reduce(jnp.max, sT))
        return tuple(tile_max)

    def consume(j, slot, tile_max, ms):
        vT = vT_ref[j]
        new_ms = []
        for h in range(Q_PER_KV):
            m_new = jnp.maximum(ms[h], tile_max[h])
            alpha = jnp.exp2(ms[h] - m_new)
            p = jnp.exp2(s_ref[slot, h] - m_new).astype(BF16)
            pv = jnp.dot(vT, p, preferred_element_type=F32)
            acc_ref[h] = alpha * acc_ref[h] + pv
            new_ms.append(m_new)
        return tuple(new_ms)

    def body(i, carry):
        tile_max, ms = carry
        j = unroll * i
        for u in range(unroll):
            nxt = j + u + 1
            if u == unroll - 1:
                nxt = jnp.minimum(nxt, nk - 1)
            next_max = scores(nxt, (u + 1) % 2)
            ms = consume(j + u, u % 2, tile_max, ms)
            tile_max = next_max
        return tile_max, ms

    m0 = tuple(jnp.full((1, tq), -jnp.inf, F32) for _ in range(Q_PER_KV))
    lax.fori_loop(0, nk // unroll, body, (scores(0, 0), m0))
    for h in range(Q_PER_KV):
        a = acc_ref[h]
        o = a[:HEAD_DIM] * (1.0 / a[HEAD_DIM:HEAD_DIM + 1])
        o_ref[:, h * HEAD_DIM:(h + 1) * HEAD_DIM] = o.T.astype(BF16)


def _attention(q, k, vT):
    B, _, T, _ = q.shape
    nk, hv, tk = vT.shape[2], vT.shape[3], vT.shape[4]
    tq = _tile(T, 256, 128)
    gw = Q_PER_KV * HEAD_DIM
    return pl.pallas_call(
        _attn_kernel,
        grid=(B, N_KV_HEADS, T // tq),
        in_specs=[pl.BlockSpec((None, Q_PER_KV, tq, HEAD_DIM), lambda b, g, i: (b, g, i, 0)),
                  pl.BlockSpec((None, None, T, HEAD_DIM), lambda b, g, i: (b, g, 0, 0)),
                  pl.BlockSpec((None, None, nk, hv, tk), lambda b, g, i: (b, g, 0, 0, 0))],
        out_specs=pl.BlockSpec((None, tq, gw), lambda b, g, i: (b, i, g)),
        out_shape=jax.ShapeDtypeStruct((B, T, N_Q_HEADS * HEAD_DIM), BF16),
        scratch_shapes=[pltpu.VMEM((Q_PER_KV, hv, tq), F32),
                        pltpu.VMEM((2, Q_PER_KV, tk, tq), F32)],
        compiler_params=_cparams(("parallel", "parallel", "arbitrary")),
        name="attention",
    )(q, k, vT)


def _log_sigmoid(z):
    return jnp.minimum(z, 0.0) - jnp.log(1.0 + jnp.exp(-jnp.abs(z)))


def _gla_kernel(qf_ref, kf_ref, vf_ref, lrf_ref, qb_ref, kb_ref, vb_ref, lrb_ref,
                w2_ref, gb_ref, of_ref, ob_ref, s_ref):
    i = pl.program_id(1)
    tb = qf_ref.shape[0]
    L = GLA_CHUNK
    nchunk = tb // L

    @pl.when(i == 0)
    def _():
        s_ref[...] = jnp.zeros_like(s_ref)

    row = lax.broadcasted_iota(jnp.int32, (L, L), 0)
    col = lax.broadcasted_iota(jnp.int32, (L, L), 1)
    masks = (row >= col, col >= row)
    ones = tuple(jnp.where(m, 1.0, 0.0).astype(BF16) for m in masks)
    qscale = GLA_DK ** -0.5
    dirs = ((qf_ref, kf_ref, vf_ref, lrf_ref, of_ref), (qb_ref, kb_ref, vb_ref, lrb_ref, ob_ref))

    log_gate = []
    for d in range(2):
        z = jnp.dot(dirs[d][3][...], w2_ref[d], preferred_element_type=F32) + gb_ref[d]
        log_gate.append(_log_sigmoid(z) * (1.0 / GLA_GATE_NORM))

    cum = {}
    for step in range(nchunk):
        for d in range(2):
            c = step if d == 0 else nchunk - 1 - step
            g = log_gate[d][c * L:(c + 1) * L, :]
            g_hi = g.astype(BF16)
            g_lo = (g - g_hi.astype(F32)).astype(BF16)
            cum[step, d] = (jnp.dot(ones[d], g_hi, preferred_element_type=F32)
                            + jnp.dot(ones[d], g_lo, preferred_element_type=F32))
    scaled = {}
    for step in range(nchunk):
        for d in range(2):
            q_ref, k_ref = dirs[d][0], dirs[d][1]
            c = step if d == 0 else nchunk - 1 - step
            rows = slice(c * L, (c + 1) * L)
            for h in range(GLA_HEADS):
                ks = slice(h * GLA_DK, (h + 1) * GLA_DK)
                b = cum[step, d][:, ks]
                b_tot = b[L - 1:L, :] if d == 0 else b[0:1, :]
                q = q_ref[rows, ks].astype(F32) * qscale
                k = k_ref[rows, ks].astype(F32)
                q_t = (q * jnp.exp(b)).astype(BF16)
                k_t = (k * jnp.exp(-b)).astype(BF16)
                k_d = (k * jnp.exp(b_tot - b)).astype(BF16)
                scaled[step, d, h] = (q_t, k_t, k_d, jnp.exp(b_tot))
    pre = {}
    for step in range(nchunk):
        for d in range(2):
            for h in range(GLA_HEADS):
                q_t, k_t, k_d, e_tot = scaled[step, d, h]
                att = lax.dot_general(q_t, k_t, (((1,), (1,)), ((), ())), preferred_element_type=F32)
                att = jnp.where(masks[d], att, 0.0).astype(BF16)
                pre[step, d, h] = (q_t, k_d, att, e_tot)

    for step in range(nchunk):
        for d in range(2):
            v_ref, o_ref = dirs[d][2], dirs[d][4]
            c = step if d == 0 else nchunk - 1 - step
            rows = slice(c * L, (c + 1) * L)
            for h in range(GLA_HEADS):
                vs = slice(h * GLA_DV, (h + 1) * GLA_DV)
                q_t, k_d, att, e_tot = pre[step, d, h]
                v = v_ref[rows, vs]
                sT = s_ref[d, h]
                o = (jnp.dot(att, v, preferred_element_type=F32)
                     + lax.dot_general(q_t, sT.astype(BF16), (((1,), (1,)), ((), ())),
                                       preferred_element_type=F32))
                o_ref[rows, vs] = o.astype(o_ref.dtype)
                dsT = lax.dot_general(v, k_d, (((0,), (0,)), ((), ())), preferred_element_type=F32)
                s_ref[d, h] = sT * e_tot + dsT


def _gla(u3, w2p, gb):
    B, T, _ = u3.shape
    tb = _tile(T, 256, GLA_CHUNK)
    nb = T // tb
    qk_w = GLA_HEADS * GLA_DK
    v_w = GLA_HEADS * GLA_DV

    def specs(idx):
        return [pl.BlockSpec((None, tb, qk_w), lambda b, i: (b, idx(i), C_Q // qk_w)),
                pl.BlockSpec((None, tb, qk_w), lambda b, i: (b, idx(i), C_K // qk_w)),
                pl.BlockSpec((None, tb, v_w), lambda b, i: (b, idx(i), C_V // v_w)),
                pl.BlockSpec((None, tb, 128), lambda b, i: (b, idx(i), C_LR // 128))]

    fwd = lambda i: i
    bwd = lambda i: nb - 1 - i
    out_sd = jax.ShapeDtypeStruct((B, T, v_w), BF16)
    return pl.pallas_call(
        _gla_kernel,
        grid=(B, nb),
        in_specs=specs(fwd) + specs(bwd) + [
            pl.BlockSpec((2, 128, qk_w), lambda b, i: (0, 0, 0)),
            pl.BlockSpec((2, 1, qk_w), lambda b, i: (0, 0, 0))],
        out_specs=[pl.BlockSpec((None, tb, v_w), lambda b, i: (b, i, 0)),
                   pl.BlockSpec((None, tb, v_w), lambda b, i: (b, nb - 1 - i, 0))],
        out_shape=[out_sd, out_sd],
        scratch_shapes=[pltpu.VMEM((2, GLA_HEADS, GLA_DV, GLA_DK), F32)],
        compiler_params=_cparams(("parallel", "arbitrary")),
        name="gla",
    )(u3, u3, u3, u3, u3, u3, u3, u3, w2p, gb)


def _merge_kernel(tiles_a, oa_ref, ob_ref, bz_ref, cf_ref, cb_ref, cz_ref, m0_ref, m1_ref, m2_ref,
                  *rest):
    *h_refs, wb_ref, wo_ref, gn_ref, pg_ref, o_ref = rest
    o_a = oa_ref[...]
    o_b = (ob_ref[...].astype(F32) * _silu(bz_ref[...].astype(F32))).astype(BF16)
    parts = []
    for h in range(GLA_HEADS):
        vs = slice(h * GLA_DV, (h + 1) * GLA_DV)
        x = cf_ref[:, vs].astype(F32) + cb_ref[:, vs].astype(F32)
        ms = jnp.mean(x * x, axis=-1, keepdims=True)
        y = x * lax.rsqrt(ms + EPS) * gn_ref[...]
        parts.append((y * _silu(cz_ref[:, vs].astype(F32))).astype(BF16))
    o_c = jnp.concatenate(parts, axis=-1)

    merged = None
    for n, (o, m_ref) in enumerate(((o_a, m0_ref), (o_b, m1_ref), (o_c, m2_ref))):
        term = _sigmoid(m_ref[...].astype(F32)) * jnp.dot(o, wb_ref[n], preferred_element_type=F32)
        merged = term if merged is None else merged + term
    out = jnp.dot(merged.astype(BF16), wo_ref[...], preferred_element_type=F32)
    ms = jnp.mean(out * out, axis=-1, keepdims=True)
    h = _pick(tiles_a, *h_refs) if tiles_a else h_refs[0][...]
    o_ref[...] = h + out * lax.rsqrt(ms + EPS) * pg_ref[...]


def _merge(o_a, o_b, o_cf, o_cb, u, h_srcs, tm, wb, wo, gla_g, post_g):
    n = u.shape[0]
    tiles_a = h_srcs[0].shape[0] // tm if len(h_srcs) == 2 else None
    W = BRANCH_W

    def rows(width, col=0):
        return pl.BlockSpec((tm, width), lambda i: (i, col))

    def const(shape):
        return pl.BlockSpec(shape, lambda i: (0,) * len(shape), pipeline_mode=pl.Buffered(1))

    return pl.pallas_call(
        functools.partial(_merge_kernel, tiles_a),
        grid=(n // tm,),
        in_specs=[rows(W), rows(W), rows(W, B_Z // W), rows(W), rows(W), rows(W, C_Z // W),
                  rows(D_MODEL, M_LOG // D_MODEL), rows(D_MODEL, M_LOG // D_MODEL + 1),
                  rows(D_MODEL, M_LOG // D_MODEL + 2),
                  *(_split_specs(tm, D_MODEL, tiles_a) if tiles_a else (rows(D_MODEL),)),
                  const((3, W, D_MODEL)), const((D_MODEL, D_MODEL)),
                  const((1, GLA_DV)), const((1, D_MODEL))],
        out_specs=rows(D_MODEL),
        out_shape=jax.ShapeDtypeStruct((n, D_MODEL), F32),
        compiler_params=_cparams(("arbitrary",)),
        name="merge",
    )(o_a, o_b, u, o_cf, o_cb, u, u, u, u, *h_srcs, wb, wo, gla_g, post_g)


def _ple_kernel(tiles_a, last, h_ref, pa_ref, pb_ref, wp_ref, wg_ref, ng_ref, *out_refs):
    h = h_ref[...]
    p = _pick(tiles_a, pa_ref, pb_ref)
    e = jnp.dot(p.astype(BF16), wp_ref[...], preferred_element_type=F32)
    gate = _sigmoid(jnp.dot(h.astype(BF16), wg_ref[...], preferred_element_type=F32))
    h2 = h + gate * e
    if last:
        ya_ref, yb_ref = out_refs
        i = pl.program_id(0)

        @pl.when(i < tiles_a)
        def _():
            ya_ref[...] = h2

        @pl.when(i >= tiles_a)
        def _():
            yb_ref[...] = h2
    else:
        o_ref, xn_ref = out_refs
        o_ref[...] = h2
        ms = jnp.mean(h2 * h2, axis=-1, keepdims=True)
        xn_ref[...] = (h2 * lax.rsqrt(ms + EPS) * ng_ref[...]).astype(BF16)


def _ple(h, p_a, p_b, rows_a, layer, tm, wp, wg, next_g, last):
    n = h.shape[0]
    tiles_a = rows_a // tm
    tiles_b = (n - rows_a) // tm

    def rows(width):
        return pl.BlockSpec((tm, width), lambda i: (i, 0))

    def const(shape):
        return pl.BlockSpec(shape, lambda i: (0,) * len(shape), pipeline_mode=pl.Buffered(1))

    if last:
        out_specs = list(_split_specs(tm, D_MODEL, tiles_a))
        out_shape = [jax.ShapeDtypeStruct((rows_a, D_MODEL), F32),
                     jax.ShapeDtypeStruct((n - rows_a, D_MODEL), F32)]
    else:
        out_specs = [rows(D_MODEL), rows(D_MODEL)]
        out_shape = [jax.ShapeDtypeStruct((n, D_MODEL), F32), jax.ShapeDtypeStruct((n, D_MODEL), BF16)]
    return pl.pallas_call(
        functools.partial(_ple_kernel, tiles_a, last),
        grid=(n // tm,),
        in_specs=[rows(D_MODEL),
                  *_split_specs(tm, PLE_DIM, tiles_a, layer * tiles_a, layer * tiles_b),
                  const((PLE_DIM, D_MODEL)), const((D_MODEL, D_MODEL)), const((1, D_MODEL))],
        out_specs=out_specs,
        out_shape=out_shape,
        compiler_params=_cparams(("arbitrary",)),
        name="ple",
    )(h, p_a, p_b, wp, wg, next_g)


def _reorder_w_in(w_in):
    sizes = (1024, 1024, 1024, 1024, 256, 256, 1024, 512, 512, 1024, 2 * GLA_RANK, 1024, 3 * D_MODEL)
    names = ("a_val", "a_glu", "a_z", "b_q", "b_k", "b_v", "b_z", "c_q", "c_k", "c_v", "c_lr", "c_z", "m")
    parts, start = {}, 0
    for name, s in zip(names, sizes):
        parts[name] = w_in[..., start:start + s]
        start += s
    parts["b_q"] = _permute_heads(parts["b_q"], N_Q_HEADS)
    parts["b_k"] = _permute_heads(parts["b_k"], N_KV_HEADS)
    pad = jnp.zeros(w_in.shape[:-1] + (U_COLS - C_LR - 2 * GLA_RANK,), w_in.dtype)
    order = ("a_val", "a_glu", "a_z", "b_q", "b_z", "c_v", "c_z", "c_q", "c_k", "m", "b_k", "b_v", "c_lr")
    w = jnp.concatenate([parts[k] for k in order] + [pad], axis=-1)
    assert w.shape[-1] == U_COLS
    return w.astype(BF16)


def _rope_tables(T):
    n_rows = T // GRID_W
    row_idx = jnp.repeat(jnp.arange(n_rows), GRID_W).astype(F32)
    col_idx = jnp.tile(jnp.arange(GRID_W), n_rows).astype(F32)
    inv = ROPE_THETA ** (-jnp.arange(ROPE_PAIRS, dtype=F32) / ROPE_PAIRS)
    ang_r = row_idx[:, None] * inv
    ang_c = col_idx[:, None] * inv
    cos = jnp.concatenate([jnp.cos(ang_r), jnp.cos(ang_c), jnp.cos(ang_r), jnp.cos(ang_c)], axis=-1)
    sin = jnp.concatenate([-jnp.sin(ang_r), -jnp.sin(ang_c), jnp.sin(ang_r), jnp.sin(ang_c)], axis=-1)
    return cos, sin


_ROPE_PERM_BLOCKS = (0, 2, 1, 3)


def _permute_heads(w, n_heads):
    shaped = w.reshape(w.shape[:-1] + (n_heads, HEAD_DIM))
    blocks = [shaped[..., b * ROPE_PAIRS:(b + 1) * ROPE_PAIRS] for b in _ROPE_PERM_BLOCKS]
    return jnp.concatenate(blocks, axis=-1).reshape(w.shape)


def _gla_gate_weights(gate_w2):
    depth = gate_w2.shape[0]
    w = jnp.zeros((depth, 2, 128, GLA_HEADS * GLA_DK), F32)
    for d in range(2):
        w = w.at[:, d, d * GLA_RANK:(d + 1) * GLA_RANK, :].set(gate_w2[:, d])
    return w.astype(BF16)


def kernel(x_prompt, x_sample, p_prompt, p_sample, pre_norm_g, post_norm_g, w_in, conv_w, conv_b, conv_ln_g, conv_ln_b, q_norm_g, k_norm_g, gla_gate_w2, gla_gate_b, gla_norm_g, w_branch, w_out, w_ple, w_ple_gate):
    depth = w_in.shape[0]
    T = x_prompt.shape[1]
    assert x_sample.shape[1] == T
    B = x_prompt.shape[0] + x_sample.shape[0]
    n = B * T
    x_a = x_prompt.reshape(-1, D_MODEL)
    x_b = x_sample.reshape(-1, D_MODEL)
    rows_a = x_a.shape[0]
    p_a = p_prompt.reshape(-1, PLE_DIM)
    p_b = p_sample.reshape(-1, PLE_DIM)
    tm_merge = _tile(T, 256, 16)
    tm_ple = _tile(T, 512, 16)

    w_in_r = _reorder_w_in(w_in)
    w_branch_b = w_branch.astype(BF16)
    w_out_b = w_out.astype(BF16)
    w_ple_b = w_ple.astype(BF16)
    w_gate_b = w_ple_gate.astype(BF16)
    w2p = _gla_gate_weights(gla_gate_w2)
    conv_w_p = jnp.pad(conv_w, ((0, 0), (0, 1), (0, 0)))
    cos, sin = _rope_tables(T)
    gq_perm = _permute_heads(q_norm_g, 1)
    gk_perm = _permute_heads(k_norm_g, 1)
    tk = _tile(T // 2, 512, 128)

    xn = _prenorm(x_a, x_b, pre_norm_g[0][None], tm_merge)
    h_srcs = (x_a, x_b)
    for l in range(depth):
        u = _inproj(xn, w_in_r[l])
        u3 = u.reshape(B, T, U_COLS)
        o_a = _conv_branch(u3, conv_w_p[l], conv_b[l][None], conv_ln_g[l][None], conv_ln_b[l][None])
        q, k, vT = _qkprep(u3, cos, sin, gq_perm[l][None], gk_perm[l][None], tk)
        o_b = _attention(q, k, vT)
        o_cf, o_cb = _gla(u3, w2p[l], gla_gate_b[l][:, None, :])
        h = _merge(o_a.reshape(n, BRANCH_W), o_b.reshape(n, BRANCH_W), o_cf.reshape(n, BRANCH_W),
                   o_cb.reshape(n, BRANCH_W), u, h_srcs, tm_merge,
                   w_branch_b[l], w_out_b[l], gla_norm_g[l][None], post_norm_g[l][None])
        last = l == depth - 1
        next_g = pre_norm_g[(l + 1) % depth][None]
        outs = _ple(h, p_a, p_b, rows_a, l, tm_ple, w_ple_b[l], w_gate_b[l], next_g, last)
        if last:
            y_prompt, y_sample = outs
        else:
            h, xn = outs
            h_srcs = (h,)
    return (y_prompt.reshape(x_prompt.shape), y_sample.reshape(x_sample.shape))
```

```python
import functools
import math

import jax
import jax.numpy as jnp
from jax import lax
from jax.experimental import pallas as pl
from jax.experimental.pallas import tpu as pltpu

F32 = jnp.float32
BF16 = jnp.bfloat16

D_MODEL = 2048
GRID_W = 64
PLE_DIM = 256
BRANCH_W = 1024
EPS = 1e-6
CONV_K = 31
CONV_HALO = 16
N_Q_HEADS = 8
N_KV_HEADS = 2
Q_PER_KV = N_Q_HEADS // N_KV_HEADS
HEAD_DIM = 128
V_PAD_ROWS = 16
ROPE_PAIRS = HEAD_DIM // 4
ROPE_THETA = 10000.0
GLA_HEADS = 4
GLA_DK = 128
GLA_DV = 256
GLA_RANK = 16
GLA_GATE_NORM = 16.0
GLA_CHUNK = 64

A_VAL, A_GLU, A_Z = 0, 1024, 2048
B_Q, B_Z = 3072, 4096
C_V, C_Z = 5120, 6144
C_Q, C_K = 7168, 7680
M_LOG = 8192
B_KV = 14336
C_LR = 14848
MXU_COLS = 256
INPROJ_TN = 6 * MXU_COLS
U_COLS = 15360

VMEM_LIMIT = 56 * 1024 * 1024


def _cparams(sem, vmem=VMEM_LIMIT):
    return pltpu.CompilerParams(dimension_semantics=sem, vmem_limit_bytes=vmem)


def _tile(n, cap, mult=8):
    t = min(cap, n)
    while t > mult and (n % t or t % mult):
        t -= mult
    assert n % t == 0, (n, cap, mult)
    return t


def _sigmoid(x):
    return 1.0 / (1.0 + jnp.exp(-x))


def _silu(x):
    return x * _sigmoid(x)


def _split_specs(tm, width, tiles_a, a_first=0, b_first=0, col=0):
    a = pl.BlockSpec((tm, width), lambda i: (a_first + jnp.minimum(i, tiles_a - 1), col))
    b = pl.BlockSpec((tm, width), lambda i: (b_first + jnp.maximum(i - tiles_a, 0), col))
    return a, b


def _pick(tiles_a, a_ref, b_ref):
    return jnp.where(pl.program_id(0) < tiles_a, a_ref[...], b_ref[...])


def _prenorm_kernel(tiles_a, xa_ref, xb_ref, g_ref, o_ref):
    x = _pick(tiles_a, xa_ref, xb_ref)
    ms = jnp.mean(x * x, axis=-1, keepdims=True)
    o_ref[...] = (x * lax.rsqrt(ms + EPS) * g_ref[...]).astype(BF16)


def _prenorm(xa, xb, g, tm):
    n = xa.shape[0] + xb.shape[0]
    tiles_a = xa.shape[0] // tm
    return pl.pallas_call(
        functools.partial(_prenorm_kernel, tiles_a),
        grid=(n // tm,),
        in_specs=[*_split_specs(tm, D_MODEL, tiles_a),
                  pl.BlockSpec((1, D_MODEL), lambda i: (0, 0))],
        out_specs=pl.BlockSpec((tm, D_MODEL), lambda i: (i, 0)),
        out_shape=jax.ShapeDtypeStruct((n, D_MODEL), BF16),
        compiler_params=_cparams(("arbitrary",)),
        name="prenorm",
    )(xa, xb, g)


def _inproj_kernel(x_ref, w_ref, u_ref):
    u_ref[...] = jnp.dot(x_ref[...], w_ref[...], preferred_element_type=F32).astype(BF16)


def _inproj(xn, w):
    n = xn.shape[0]
    tm = _tile(n, 1024, 16)
    tn = INPROJ_TN
    return pl.pallas_call(
        _inproj_kernel,
        grid=(n // tm, U_COLS // tn),
        in_specs=[pl.BlockSpec((tm, D_MODEL), lambda i, j: (i, 0)),
                  pl.BlockSpec((D_MODEL, tn), lambda i, j: (0, j))],
        out_specs=pl.BlockSpec((tm, tn), lambda i, j: (i, j)),
        out_shape=jax.ShapeDtypeStruct((n, U_COLS), BF16),
        compiler_params=_cparams(("parallel", "arbitrary")),
        name="inproj",
    )(xn, w)


def _conv_kernel(val_ref, glu_ref, z_ref, pval_ref, pglu_ref, nval_ref, nglu_ref,
                 w_ref, cb_ref, lg_ref, lb_ref, o_ref, abuf, cbuf):
    i = pl.program_id(1)
    n = pl.num_programs(1)
    tt = val_ref.shape[0]
    H = CONV_HALO

    def glu(v_ref, g_ref):
        return v_ref[...].astype(F32) * _sigmoid(g_ref[...].astype(F32))

    abuf[0:H, :] = glu(pval_ref, pglu_ref) * (i > 0).astype(F32)
    abuf[H:H + tt, :] = glu(val_ref, glu_ref)
    abuf[H + tt:2 * H + tt, :] = glu(nval_ref, nglu_ref) * (i < n - 1).astype(F32)

    assert H - CONV_K // 2 == 1 and CONV_K < 32
    for c in range(BRANCH_W // 128):
        cs = slice(c * 128, (c + 1) * 128)
        acc = None
        for r in range(8):
            part = None
            for q in range(4):
                j = 8 * q + r
                if j == 0:
                    continue
                term = abuf[8 * q:8 * q + tt + 8, cs] * w_ref[j - 1:j, cs]
                part = term if part is None else part + term
            shifted = part[r:r + tt, :]
            acc = shifted if acc is None else acc + shifted
        cbuf[:, cs] = acc + cb_ref[:, cs]

    x = cbuf[...]
    mu = jnp.mean(x, axis=-1, keepdims=True)
    xc = x - mu
    var = jnp.mean(xc * xc, axis=-1, keepdims=True)
    y = xc * lax.rsqrt(var + EPS) * lg_ref[...] + lb_ref[...]
    o_ref[...] = (_silu(y) * _silu(z_ref[...].astype(F32))).astype(BF16)


def _conv_branch(u3, conv_w, conv_b, ln_g, ln_b):
    B, T, _ = u3.shape
    tt = _tile(T, 128, 16)
    H = CONV_HALO
    hb = tt // H
    nh = T // H

    def cur(col):
        return pl.BlockSpec((None, tt, BRANCH_W), lambda b, i: (b, i, col))

    def prev(col):
        return pl.BlockSpec((None, H, BRANCH_W), lambda b, i: (b, jnp.maximum(i * hb - 1, 0), col))

    def nxt(col):
        return pl.BlockSpec((None, H, BRANCH_W), lambda b, i: (b, jnp.minimum((i + 1) * hb, nh - 1), col))

    vec = pl.BlockSpec((1, BRANCH_W), lambda b, i: (0, 0))
    return pl.pallas_call(
        _conv_kernel,
        grid=(B, T // tt),
        in_specs=[cur(A_VAL // BRANCH_W), cur(A_GLU // BRANCH_W), cur(A_Z // BRANCH_W),
                  prev(A_VAL // BRANCH_W), prev(A_GLU // BRANCH_W),
                  nxt(A_VAL // BRANCH_W), nxt(A_GLU // BRANCH_W),
                  pl.BlockSpec((CONV_K + 1, BRANCH_W), lambda b, i: (0, 0)), vec, vec, vec],
        out_specs=pl.BlockSpec((None, tt, BRANCH_W), lambda b, i: (b, i, 0)),
        out_shape=jax.ShapeDtypeStruct((B, T, BRANCH_W), BF16),
        scratch_shapes=[pltpu.VMEM((tt + 2 * H, BRANCH_W), F32), pltpu.VMEM((tt, BRANCH_W), F32)],
        compiler_params=_cparams(("parallel", "parallel")),
        name="conv",
    )(u3, u3, u3, u3, u3, u3, u3, conv_w, conv_b, ln_g, ln_b)


def _norm_rope(x, g, cos, sin):
    ms = jnp.mean(x * x, axis=-1, keepdims=True)
    y = x * lax.rsqrt(ms + EPS) * g
    return y * cos + pltpu.roll(y, HEAD_DIM // 2, 1) * sin


def _qkprep_kernel(q_ref, kv_ref, cos_ref, sin_ref, gq_ref, gk_ref, qo_ref, k_ref, vT_ref):
    tt = q_ref.shape[0]
    cos = cos_ref[...]
    sin = sin_ref[...]
    qscale = HEAD_DIM ** -0.5 * math.log2(math.e)
    for h in range(N_Q_HEADS):
        x = q_ref[:, h * HEAD_DIM:(h + 1) * HEAD_DIM].astype(F32)
        y = _norm_rope(x, gq_ref[...], cos, sin) * qscale
        qo_ref[h] = y.astype(BF16)
    for g in range(N_KV_HEADS):
        x = kv_ref[:, g * HEAD_DIM:(g + 1) * HEAD_DIM].astype(F32)
        k_ref[g] = _norm_rope(x, gk_ref[...], cos, sin).astype(BF16)
        v = kv_ref[:, (N_KV_HEADS + g) * HEAD_DIM:(N_KV_HEADS + g + 1) * HEAD_DIM].astype(F32)
        vT_ref[g, 0, :HEAD_DIM, :] = v.T.astype(BF16)
        pad_row = lax.broadcasted_iota(jnp.int32, (V_PAD_ROWS, tt), 0)
        vT_ref[g, 0, HEAD_DIM:, :] = jnp.where(pad_row == 0, 1.0, 0.0).astype(BF16)


def _qkprep(u3, cos, sin, gq, gk, tk):
    B, T, _ = u3.shape
    qw = N_Q_HEADS * HEAD_DIM
    kvw = 2 * N_KV_HEADS * HEAD_DIM
    vec = pl.BlockSpec((1, HEAD_DIM), lambda b, i: (0, 0))
    tab = pl.BlockSpec((tk, HEAD_DIM), lambda b, i: (i, 0))
    return pl.pallas_call(
        _qkprep_kernel,
        grid=(B, T // tk),
        in_specs=[pl.BlockSpec((None, tk, qw), lambda b, i: (b, i, B_Q // qw)),
                  pl.BlockSpec((None, tk, kvw), lambda b, i: (b, i, B_KV // kvw)),
                  tab, tab, vec, vec],
        out_specs=[pl.BlockSpec((None, N_Q_HEADS, tk, HEAD_DIM), lambda b, i: (b, 0, i, 0)),
                   pl.BlockSpec((None, N_KV_HEADS, tk, HEAD_DIM), lambda b, i: (b, 0, i, 0)),
                   pl.BlockSpec((None, N_KV_HEADS, 1, HEAD_DIM + V_PAD_ROWS, tk), lambda b, i: (b, 0, i, 0, 0))],
        out_shape=[jax.ShapeDtypeStruct((B, N_Q_HEADS, T, HEAD_DIM), BF16),
                   jax.ShapeDtypeStruct((B, N_KV_HEADS, T, HEAD_DIM), BF16),
                   jax.ShapeDtypeStruct((B, N_KV_HEADS, T // tk, HEAD_DIM + V_PAD_ROWS, tk), BF16)],
        compiler_params=_cparams(("parallel", "parallel")),
        name="qkprep",
    )(u3, u3, cos, sin, gq, gk)


def _col_reduce(op, x):
    rows, cols = x.shape
    if rows % 64 == 0:
        x = op(x.reshape(8, rows // 8, cols), axis=0)
    return op(x, axis=0, keepdims=True)


def _attn_kernel(q_ref, k_ref, vT_ref, o_ref, acc_ref, s_ref):
    nk, _, tk = vT_ref.shape
    tq = q_ref.shape[1]
    unroll = next(u for u in (4, 2) if nk % u == 0)
    acc_ref[...] = jnp.zeros_like(acc_ref)

    def scores(j, slot):
        k = k_ref[pl.ds(pl.multiple_of(j * tk, tk), tk), :]
        tile_max = []
        for h in range(Q_PER_KV):
            sT = lax.dot_general(k, q_ref[h], (((1,), (1,)), ((), ())),
                                 preferred_element_type=F32)
            s_ref[slot, h] = sT
            tile_max.append(_col_reduce(jnp.max, sT))
        return tuple(tile_max)

    def consume(j, slot, tile_max, ms):
        vT = vT_ref[j]
        new_ms = []
        for h in range(Q_PER_KV):
            m_new = jnp.maximum(ms[h], tile_max[h])
            alpha = jnp.exp2(ms[h] - m_new)
            p = jnp.exp2(s_ref[slot, h] - m_new).astype(BF16)
            pv = jnp.dot(vT, p, preferred_element_type=F32)
            acc_ref[h] = alpha * acc_ref[h] + pv
            new_ms.append(m_new)
        return tuple(new_ms)

    def body(i, carry):
        tile_max, ms = carry
        j = unroll * i
        for u in range(unroll):
            nxt = j + u + 1
            if u == unroll - 1:
                nxt = jnp.minimum(nxt, nk - 1)
            next_max = scores(nxt, (u + 1) % 2)
            ms = consume(j + u, u % 2, tile_max, ms)
            tile_max = next_max
        return tile_max, ms

    m0 = tuple(jnp.full((1, tq), -jnp.inf, F32) for _ in range(Q_PER_KV))
    lax.fori_loop(0, nk // unroll, body, (scores(0, 0), m0))
    for h in range(Q_PER_KV):
        a = acc_ref[h]
        o = a[:HEAD_DIM] * (1.0 / a[HEAD_DIM:HEAD_DIM + 1])
        o_ref[:, h * HEAD_DIM:(h + 1) * HEAD_DIM] = o.T.astype(BF16)


def _attention(q, k, vT):
    B, _, T, _ = q.shape
    nk, hv, tk = vT.shape[2], vT.shape[3], vT.shape[4]
    tq = _tile(T, 256, 128)
    gw = Q_PER_KV * HEAD_DIM
    return pl.pallas_call(
        _attn_kernel,
        grid=(B, N_KV_HEADS, T // tq),
        in_specs=[pl.BlockSpec((None, Q_PER_KV, tq, HEAD_DIM), lambda b, g, i: (b, g, i, 0)),
                  pl.BlockSpec((None, None, T, HEAD_DIM), lambda b, g, i: (b, g, 0, 0)),
                  pl.BlockSpec((None, None, nk, hv, tk), lambda b, g, i: (b, g, 0, 0, 0))],
        out_specs=pl.BlockSpec((None, tq, gw), lambda b, g, i: (b, i, g)),
        out_shape=jax.ShapeDtypeStruct((B, T, N_Q_HEADS * HEAD_DIM), BF16),
        scratch_shapes=[pltpu.VMEM((Q_PER_KV, hv, tq), F32),
                        pltpu.VMEM((2, Q_PER_KV, tk, tq), F32)],
        compiler_params=_cparams(("parallel", "parallel", "arbitrary")),
        name="attention",
    )(q, k, vT)


def _log_sigmoid(z):
    return jnp.minimum(z, 0.0) - jnp.log(1.0 + jnp.exp(-jnp.abs(z)))


def _gla_kernel(qf_ref, kf_ref, vf_ref, lrf_ref, qb_ref, kb_ref, vb_ref, lrb_ref,
                w2_ref, gb_ref, of_ref, ob_ref, s_ref):
    i = pl.program_id(1)
    tb = qf_ref.shape[0]
    L = GLA_CHUNK
    nchunk = tb // L

    @pl.when(i == 0)
    def _():
        s_ref[...] = jnp.zeros_like(s_ref)

    row = lax.broadcasted_iota(jnp.int32, (L, L), 0)
    col = lax.broadcasted_iota(jnp.int32, (L, L), 1)
    masks = (row >= col, col >= row)
    ones = tuple(jnp.where(m, 1.0, 0.0).astype(BF16) for m in masks)
    qscale = GLA_DK ** -0.5
    dirs = ((qf_ref, kf_ref, vf_ref, lrf_ref, of_ref), (qb_ref, kb_ref, vb_ref, lrb_ref, ob_ref))

    log_gate = []
    for d in range(2):
        z = jnp.dot(dirs[d][3][...], w2_ref[d], preferred_element_type=F32) + gb_ref[d]
        log_gate.append(_log_sigmoid(z) * (1.0 / GLA_GATE_NORM))

    cum = {}
    for step in range(nchunk):
        for d in range(2):
            c = step if d == 0 else nchunk - 1 - step
            g = log_gate[d][c * L:(c + 1) * L, :]
            g_hi = g.astype(BF16)
            g_lo = (g - g_hi.astype(F32)).astype(BF16)
            cum[step, d] = (jnp.dot(ones[d], g_hi, preferred_element_type=F32)
                            + jnp.dot(ones[d], g_lo, preferred_element_type=F32))
    scaled = {}
    for step in range(nchunk):
        for d in range(2):
            q_ref, k_ref = dirs[d][0], dirs[d][1]
            c = step if d == 0 else nchunk - 1 - step
            rows = slice(c * L, (c + 1) * L)
            for h in range(GLA_HEADS):
                ks = slice(h * GLA_DK, (h + 1) * GLA_DK)
                b = cum[step, d][:, ks]
                b_tot = b[L - 1:L, :] if d == 0 else b[0:1, :]
                q = q_ref[rows, ks].astype(F32) * qscale
                k = k_ref[rows, ks].astype(F32)
                q_t = (q * jnp.exp(b)).astype(BF16)
                k_t = (k * jnp.exp(-b)).astype(BF16)
                k_d = (k * jnp.exp(b_tot - b)).astype(BF16)
                scaled[step, d, h] = (q_t, k_t, k_d, jnp.exp(b_tot))
    pre = {}
    for step in range(nchunk):
        for d in range(2):
            for h in range(GLA_HEADS):
                q_t, k_t, k_d, e_tot = scaled[step, d, h]
                att = lax.dot_general(q_t, k_t, (((1,), (1,)), ((), ())), preferred_element_type=F32)
                att = jnp.where(masks[d], att, 0.0).astype(BF16)
                pre[step, d, h] = (q_t, k_d, att, e_tot)

    for step in range(nchunk):
        for d in range(2):
            v_ref, o_ref = dirs[d][2], dirs[d][4]
            c = step if d == 0 else nchunk - 1 - step
            rows = slice(c * L, (c + 1) * L)
            for h in range(GLA_HEADS):
                vs = slice(h * GLA_DV, (h + 1) * GLA_DV)
                q_t, k_d, att, e_tot = pre[step, d, h]
                v = v_ref[rows, vs]
                sT = s_ref[d, h]
                o = (jnp.dot(att, v, preferred_element_type=F32)
                     + lax.dot_general(q_t, sT.astype(BF16), (((1,), (1,)), ((), ())),
                                       preferred_element_type=F32))
                o_ref[rows, vs] = o.astype(o_ref.dtype)
                dsT = lax.dot_general(v, k_d, (((0,), (0,)), ((), ())), preferred_element_type=F32)
                s_ref[d, h] = sT * e_tot + dsT


def _gla(u3, w2p, gb):
    B, T, _ = u3.shape
    tb = _tile(T, 256, GLA_CHUNK)
    nb = T // tb
    qk_w = GLA_HEADS * GLA_DK
    v_w = GLA_HEADS * GLA_DV

    def specs(idx):
        return [pl.BlockSpec((None, tb, qk_w), lambda b, i: (b, idx(i), C_Q // qk_w)),
                pl.BlockSpec((None, tb, qk_w), lambda b, i: (b, idx(i), C_K // qk_w)),
                pl.BlockSpec((None, tb, v_w), lambda b, i: (b, idx(i), C_V // v_w)),
                pl.BlockSpec((None, tb, 128), lambda b, i: (b, idx(i), C_LR // 128))]

    fwd = lambda i: i
    bwd = lambda i: nb - 1 - i
    out_sd = jax.ShapeDtypeStruct((B, T, v_w), BF16)
    return pl.pallas_call(
        _gla_kernel,
        grid=(B, nb),
        in_specs=specs(fwd) + specs(bwd) + [
            pl.BlockSpec((2, 128, qk_w), lambda b, i: (0, 0, 0)),
            pl.BlockSpec((2, 1, qk_w), lambda b, i: (0, 0, 0))],
        out_specs=[pl.BlockSpec((None, tb, v_w), lambda b, i: (b, i, 0)),
                   pl.BlockSpec((None, tb, v_w), lambda b, i: (b, nb - 1 - i, 0))],
        out_shape=[out_sd, out_sd],
        scratch_shapes=[pltpu.VMEM((2, GLA_HEADS, GLA_DV, GLA_DK), F32)],
        compiler_params=_cparams(("parallel", "arbitrary")),
        name="gla",
    )(u3, u3, u3, u3, u3, u3, u3, u3, w2p, gb)


def _merge_kernel(tiles_a, oa_ref, ob_ref, bz_ref, cf_ref, cb_ref, cz_ref, m0_ref, m1_ref, m2_ref,
                  *rest):
    *h_refs, wb_ref, wo_ref, gn_ref, pg_ref, o_ref = rest
    o_a = oa_ref[...]
    o_b = (ob_ref[...].astype(F32) * _silu(bz_ref[...].astype(F32))).astype(BF16)
    parts = []
    for h in range(GLA_HEADS):
        vs = slice(h * GLA_DV, (h + 1) * GLA_DV)
        x = cf_ref[:, vs].astype(F32) + cb_ref[:, vs].astype(F32)
        ms = jnp.mean(x * x, axis=-1, keepdims=True)
        y = x * lax.rsqrt(ms + EPS) * gn_ref[...]
        parts.append((y * _silu(cz_ref[:, vs].astype(F32))).astype(BF16))
    o_c = jnp.concatenate(parts, axis=-1)

    merged = None
    for n, (o, m_ref) in enumerate(((o_a, m0_ref), (o_b, m1_ref), (o_c, m2_ref))):
        term = _sigmoid(m_ref[...].astype(F32)) * jnp.dot(o, wb_ref[n], preferred_element_type=F32)
        merged = term if merged is None else merged + term
    out = jnp.dot(merged.astype(BF16), wo_ref[...], preferred_element_type=F32)
    ms = jnp.mean(out * out, axis=-1, keepdims=True)
    h = _pick(tiles_a, *h_refs) if tiles_a else h_refs[0][...]
    o_ref[...] = h + out * lax.rsqrt(ms + EPS) * pg_ref[...]


def _merge(o_a, o_b, o_cf, o_cb, u, h_srcs, tm, wb, wo, gla_g, post_g):
    n = u.shape[0]
    tiles_a = h_srcs[0].shape[0] // tm if len(h_srcs) == 2 else None
    W = BRANCH_W

    def rows(width, col=0):
        return pl.BlockSpec((tm, width), lambda i: (i, col))

    def const(shape):
        return pl.BlockSpec(shape, lambda i: (0,) * len(shape), pipeline_mode=pl.Buffered(1))

    return pl.pallas_call(
        functools.partial(_merge_kernel, tiles_a),
        grid=(n // tm,),
        in_specs=[rows(W), rows(W), rows(W, B_Z // W), rows(W), rows(W), rows(W, C_Z // W),
                  rows(D_MODEL, M_LOG // D_MODEL), rows(D_MODEL, M_LOG // D_MODEL + 1),
                  rows(D_MODEL, M_LOG // D_MODEL + 2),
                  *(_split_specs(tm, D_MODEL, tiles_a) if tiles_a else (rows(D_MODEL),)),
                  const((3, W, D_MODEL)), const((D_MODEL, D_MODEL)),
                  const((1, GLA_DV)), const((1, D_MODEL))],
        out_specs=rows(D_MODEL),
        out_shape=jax.ShapeDtypeStruct((n, D_MODEL), F32),
        compiler_params=_cparams(("arbitrary",)),
        name="merge",
    )(o_a, o_b, u, o_cf, o_cb, u, u, u, u, *h_srcs, wb, wo, gla_g, post_g)


def _ple_kernel(tiles_a, last, h_ref, pa_ref, pb_ref, wp_ref, wg_ref, ng_ref, *out_refs):
    h = h_ref[...]
    p = _pick(tiles_a, pa_ref, pb_ref)
    e = jnp.dot(p.astype(BF16), wp_ref[...], preferred_element_type=F32)
    gate = _sigmoid(jnp.dot(h.astype(BF16), wg_ref[...], preferred_element_type=F32))
    h2 = h + gate * e
    if last:
        ya_ref, yb_ref = out_refs
        i = pl.program_id(0)

        @pl.when(i < tiles_a)
        def _():
            ya_ref[...] = h2

        @pl.when(i >= tiles_a)
        def _():
            yb_ref[...] = h2
    else:
        o_ref, xn_ref = out_refs
        o_ref[...] = h2
        ms = jnp.mean(h2 * h2, axis=-1, keepdims=True)
        xn_ref[...] = (h2 * lax.rsqrt(ms + EPS) * ng_ref[...]).astype(BF16)


def _ple(h, p_a, p_b, rows_a, layer, tm, wp, wg, next_g, last):
    n = h.shape[0]
    tiles_a = rows_a // tm
    tiles_b = (n - rows_a) // tm

    def rows(width):
        return pl.BlockSpec((tm, width), lambda i: (i, 0))

    def const(shape):
        return pl.BlockSpec(shape, lambda i: (0,) * len(shape), pipeline_mode=pl.Buffered(1))

    if last:
        out_specs = list(_split_specs(tm, D_MODEL, tiles_a))
        out_shape = [jax.ShapeDtypeStruct((rows_a, D_MODEL), F32),
                     jax.ShapeDtypeStruct((n - rows_a, D_MODEL), F32)]
    else:
        out_specs = [rows(D_MODEL), rows(D_MODEL)]
        out_shape = [jax.ShapeDtypeStruct((n, D_MODEL), F32), jax.ShapeDtypeStruct((n, D_MODEL), BF16)]
    return pl.pallas_call(
        functools.partial(_ple_kernel, tiles_a, last),
        grid=(n // tm,),
        in_specs=[rows(D_MODEL),
                  *_split_specs(tm, PLE_DIM, tiles_a, layer * tiles_a, layer * tiles_b),
                  const((PLE_DIM, D_MODEL)), const((D_MODEL, D_MODEL)), const((1, D_MODEL))],
        out_specs=out_specs,
        out_shape=out_shape,
        compiler_params=_cparams(("arbitrary",)),
        name="ple",
    )(h, p_a, p_b, wp, wg, next_g)


def _reorder_w_in(w_in):
    sizes = (1024, 1024, 1024, 1024, 256, 256, 1024, 512, 512, 1024, 2 * GLA_RANK, 1024, 3 * D_MODEL)
    names = ("a_val", "a_glu", "a_z", "b_q", "b_k", "b_v", "b_z", "c_q", "c_k", "c_v", "c_lr", "c_z", "m")
    parts, start = {}, 0
    for name, s in zip(names, sizes):
        parts[name] = w_in[..., start:start + s].astype(BF16)
        start += s
    parts["b_q"] = _permute_heads(parts["b_q"], N_Q_HEADS)
    parts["b_k"] = _permute_heads(parts["b_k"], N_KV_HEADS)
    pad = jnp.zeros(w_in.shape[:-1] + (U_COLS - C_LR - 2 * GLA_RANK,), BF16)
    order = ("a_val", "a_glu", "a_z", "b_q", "b_z", "c_v", "c_z", "c_q", "c_k", "m", "b_k", "b_v", "c_lr")
    w = jnp.concatenate([parts[k] for k in order] + [pad], axis=-1)
    assert w.shape[-1] == U_COLS
    return w


def _rope_tables(T):
    n_rows = T // GRID_W
    row_idx = jnp.repeat(jnp.arange(n_rows), GRID_W).astype(F32)
    col_idx = jnp.tile(jnp.arange(GRID_W), n_rows).astype(F32)
    inv = ROPE_THETA ** (-jnp.arange(ROPE_PAIRS, dtype=F32) / ROPE_PAIRS)
    ang_r = row_idx[:, None] * inv
    ang_c = col_idx[:, None] * inv
    cos = jnp.concatenate([jnp.cos(ang_r), jnp.cos(ang_c), jnp.cos(ang_r), jnp.cos(ang_c)], axis=-1)
    sin = jnp.concatenate([-jnp.sin(ang_r), -jnp.sin(ang_c), jnp.sin(ang_r), jnp.sin(ang_c)], axis=-1)
    return cos, sin


_ROPE_PERM_BLOCKS = (0, 2, 1, 3)


def _permute_heads(w, n_heads):
    shaped = w.reshape(w.shape[:-1] + (n_heads, HEAD_DIM))
    blocks = [shaped[..., b * ROPE_PAIRS:(b + 1) * ROPE_PAIRS] for b in _ROPE_PERM_BLOCKS]
    return jnp.concatenate(blocks, axis=-1).reshape(w.shape)


def _gla_gate_weights(gate_w2):
    depth = gate_w2.shape[0]
    w = jnp.zeros((depth, 2, 128, GLA_HEADS * GLA_DK), F32)
    for d in range(2):
        w = w.at[:, d, d * GLA_RANK:(d + 1) * GLA_RANK, :].set(gate_w2[:, d])
    return w.astype(BF16)


def kernel(x_prompt, x_sample, p_prompt, p_sample, pre_norm_g, post_norm_g, w_in, conv_w, conv_b, conv_ln_g, conv_ln_b, q_norm_g, k_norm_g, gla_gate_w2, gla_gate_b, gla_norm_g, w_branch, w_out, w_ple, w_ple_gate):
    depth = w_in.shape[0]
    T = x_prompt.shape[1]
    assert x_sample.shape[1] == T
    B = x_prompt.shape[0] + x_sample.shape[0]
    n = B * T
    x_a = x_prompt.reshape(-1, D_MODEL)
    x_b = x_sample.reshape(-1, D_MODEL)
    rows_a = x_a.shape[0]
    p_a = p_prompt.reshape(-1, PLE_DIM)
    p_b = p_sample.reshape(-1, PLE_DIM)
    tm_merge = _tile(T, 256, 16)
    tm_ple = _tile(T, 512, 16)

    w_in_r = [_reorder_w_in(w_in[l]) for l in range(depth)]
    w_branch_b = [w_branch[l].astype(BF16) for l in range(depth)]
    w_out_b = [w_out[l].astype(BF16) for l in range(depth)]
    w_ple_b = [w_ple[l].astype(BF16) for l in range(depth)]
    w_gate_b = [w_ple_gate[l].astype(BF16) for l in range(depth)]
    w2p = _gla_gate_weights(gla_gate_w2)
    conv_w_p = jnp.pad(conv_w, ((0, 0), (0, 1), (0, 0)))
    cos, sin = _rope_tables(T)
    gq_perm = _permute_heads(q_norm_g, 1)
    gk_perm = _permute_heads(k_norm_g, 1)
    tk = _tile(T // 2, 512, 128)

    xn = _prenorm(x_a, x_b, pre_norm_g[0][None], tm_merge)
    h_srcs = (x_a, x_b)
    for l in range(depth):
        u = _inproj(xn, w_in_r[l])
        u3 = u.reshape(B, T, U_COLS)
        o_a = _conv_branch(u3, conv_w_p[l], conv_b[l][None], conv_ln_g[l][None], conv_ln_b[l][None])
        q, k, vT = _qkprep(u3, cos, sin, gq_perm[l][None], gk_perm[l][None], tk)
        o_b = _attention(q, k, vT)
        o_cf, o_cb = _gla(u3, w2p[l], gla_gate_b[l][:, None, :])
        h = _merge(o_a.reshape(n, BRANCH_W), o_b.reshape(n, BRANCH_W), o_cf.reshape(n, BRANCH_W),
                   o_cb.reshape(n, BRANCH_W), u, h_srcs, tm_merge,
                   w_branch_b[l], w_out_b[l], gla_norm_g[l][None], post_norm_g[l][None])
        last = l == depth - 1
        next_g = pre_norm_g[(l + 1) % depth][None]
        outs = _ple(h, p_a, p_b, rows_a, l, tm_ple, w_ple_b[l], w_gate_b[l], next_g, last)
        if last:
            y_prompt, y_sample = outs
        else:
            h, xn = outs
            h_srcs = (h,)
    return (y_prompt.reshape(x_prompt.shape), y_sample.reshape(x_sample.shape))
```

```python
import functools
import math

import jax
import jax.numpy as jnp
from jax import lax
from jax.experimental import pallas as pl
from jax.experimental.pallas import tpu as pltpu

F32 = jnp.float32
BF16 = jnp.bfloat16

D_MODEL = 2048
GRID_W = 64
PLE_DIM = 256
BRANCH_W = 1024
EPS = 1e-6
CONV_K = 31
CONV_HALO = 16
N_Q_HEADS = 8
N_KV_HEADS = 2
Q_PER_KV = N_Q_HEADS // N_KV_HEADS
HEAD_DIM = 128
V_PAD_ROWS = 16
ROPE_PAIRS = HEAD_DIM // 4
ROPE_THETA = 10000.0
GLA_HEADS = 4
GLA_DK = 128
GLA_DV = 256
GLA_RANK = 16
GLA_GATE_NORM = 16.0
GLA_CHUNK = 64

A_VAL, A_GLU, A_Z = 0, 1024, 2048
B_Q, B_Z = 3072, 4096
C_V, C_Z = 5120, 6144
C_Q, C_K = 7168, 7680
M_LOG = 8192
B_KV = 14336
C_LR = 14848
MXU_COLS = 256
INPROJ_TN = 6 * MXU_COLS
U_COLS = 15360

VMEM_LIMIT = 56 * 1024 * 1024


def _cparams(sem, vmem=VMEM_LIMIT):
    return pltpu.CompilerParams(dimension_semantics=sem, vmem_limit_bytes=vmem)


def _tile(n, cap, mult=8):
    t = min(cap, n)
    while t > mult and (n % t or t % mult):
        t -= mult
    assert n % t == 0, (n, cap, mult)
    return t


def _sigmoid(x):
    return 1.0 / (1.0 + jnp.exp(-x))


def _silu(x):
    return x * _sigmoid(x)


def _split_specs(tm, width, tiles_a, a_first=0, b_first=0, col=0):
    a = pl.BlockSpec((tm, width), lambda i: (a_first + jnp.minimum(i, tiles_a - 1), col))
    b = pl.BlockSpec((tm, width), lambda i: (b_first + jnp.maximum(i - tiles_a, 0), col))
    return a, b


def _pick(tiles_a, a_ref, b_ref):
    return jnp.where(pl.program_id(0) < tiles_a, a_ref[...], b_ref[...])


def _prenorm_kernel(tiles_a, xa_ref, xb_ref, g_ref, o_ref):
    x = _pick(tiles_a, xa_ref, xb_ref)
    ms = jnp.mean(x * x, axis=-1, keepdims=True)
    o_ref[...] = (x * lax.rsqrt(ms + EPS) * g_ref[...]).astype(BF16)


def _prenorm(xa, xb, g, tm):
    n = xa.shape[0] + xb.shape[0]
    tiles_a = xa.shape[0] // tm
    return pl.pallas_call(
        functools.partial(_prenorm_kernel, tiles_a),
        grid=(n // tm,),
        in_specs=[*_split_specs(tm, D_MODEL, tiles_a),
                  pl.BlockSpec((1, D_MODEL), lambda i: (0, 0))],
        out_specs=pl.BlockSpec((tm, D_MODEL), lambda i: (i, 0)),
        out_shape=jax.ShapeDtypeStruct((n, D_MODEL), BF16),
        compiler_params=_cparams(("arbitrary",)),
        name="prenorm",
    )(xa, xb, g)


def _inproj_kernel(x_ref, w_ref, u_ref):
    u_ref[...] = jnp.dot(x_ref[...], w_ref[...], preferred_element_type=F32).astype(BF16)


def _inproj(xn, w):
    n = xn.shape[0]
    tm = _tile(n, 1024, 16)
    tn = INPROJ_TN
    return pl.pallas_call(
        _inproj_kernel,
        grid=(n // tm, U_COLS // tn),
        in_specs=[pl.BlockSpec((tm, D_MODEL), lambda i, j: (i, 0)),
                  pl.BlockSpec((D_MODEL, tn), lambda i, j: (0, j))],
        out_specs=pl.BlockSpec((tm, tn), lambda i, j: (i, j)),
        out_shape=jax.ShapeDtypeStruct((n, U_COLS), BF16),
        compiler_params=_cparams(("parallel", "arbitrary")),
        name="inproj",
    )(xn, w)


def _conv_kernel(val_ref, glu_ref, z_ref, pval_ref, pglu_ref, nval_ref, nglu_ref,
                 w_ref, cb_ref, lg_ref, lb_ref, o_ref, abuf, cbuf):
    i = pl.program_id(1)
    n = pl.num_programs(1)
    tt = val_ref.shape[0]
    H = CONV_HALO

    def glu(v_ref, g_ref):
        return v_ref[...].astype(F32) * _sigmoid(g_ref[...].astype(F32))

    abuf[0:H, :] = glu(pval_ref, pglu_ref) * (i > 0).astype(F32)
    abuf[H:H + tt, :] = glu(val_ref, glu_ref)
    abuf[H + tt:2 * H + tt, :] = glu(nval_ref, nglu_ref) * (i < n - 1).astype(F32)

    assert H - CONV_K // 2 == 1 and CONV_K < 32
    for c in range(BRANCH_W // 128):
        cs = slice(c * 128, (c + 1) * 128)
        acc = None
        for r in range(8):
            part = None
            for q in range(4):
                j = 8 * q + r
                if j == 0:
                    continue
                term = abuf[8 * q:8 * q + tt + 8, cs] * w_ref[j - 1:j, cs]
                part = term if part is None else part + term
            shifted = part[r:r + tt, :]
            acc = shifted if acc is None else acc + shifted
        cbuf[:, cs] = acc + cb_ref[:, cs]

    x = cbuf[...]
    mu = jnp.mean(x, axis=-1, keepdims=True)
    xc = x - mu
    var = jnp.mean(xc * xc, axis=-1, keepdims=True)
    y = xc * lax.rsqrt(var + EPS) * lg_ref[...] + lb_ref[...]
    o_ref[...] = (_silu(y) * _silu(z_ref[...].astype(F32))).astype(BF16)


def _conv_branch(u3, conv_w, conv_b, ln_g, ln_b):
    B, T, _ = u3.shape
    tt = _tile(T, 128, 16)
    H = CONV_HALO
    hb = tt // H
    nh = T // H

    def cur(col):
        return pl.BlockSpec((None, tt, BRANCH_W), lambda b, i: (b, i, col))

    def prev(col):
        return pl.BlockSpec((None, H, BRANCH_W), lambda b, i: (b, jnp.maximum(i * hb - 1, 0), col))

    def nxt(col):
        return pl.BlockSpec((None, H, BRANCH_W), lambda b, i: (b, jnp.minimum((i + 1) * hb, nh - 1), col))

    vec = pl.BlockSpec((1, BRANCH_W), lambda b, i: (0, 0))
    return pl.pallas_call(
        _conv_kernel,
        grid=(B, T // tt),
        in_specs=[cur(A_VAL // BRANCH_W), cur(A_GLU // BRANCH_W), cur(A_Z // BRANCH_W),
                  prev(A_VAL // BRANCH_W), prev(A_GLU // BRANCH_W),
                  nxt(A_VAL // BRANCH_W), nxt(A_GLU // BRANCH_W),
                  pl.BlockSpec((CONV_K + 1, BRANCH_W), lambda b, i: (0, 0)), vec, vec, vec],
        out_specs=pl.BlockSpec((None, tt, BRANCH_W), lambda b, i: (b, i, 0)),
        out_shape=jax.ShapeDtypeStruct((B, T, BRANCH_W), BF16),
        scratch_shapes=[pltpu.VMEM((tt + 2 * H, BRANCH_W), F32), pltpu.VMEM((tt, BRANCH_W), F32)],
        compiler_params=_cparams(("parallel", "parallel")),
        name="conv",
    )(u3, u3, u3, u3, u3, u3, u3, conv_w, conv_b, ln_g, ln_b)


def _norm_rope(x, g, cos, sin):
    ms = jnp.mean(x * x, axis=-1, keepdims=True)
    y = x * lax.rsqrt(ms + EPS) * g
    return y * cos + pltpu.roll(y, HEAD_DIM // 2, 1) * sin


def _qkprep_kernel(q_ref, kv_ref, cos_ref, sin_ref, gq_ref, gk_ref, qo_ref, k_ref, vT_ref):
    tt = q_ref.shape[0]
    cos = cos_ref[...]
    sin = sin_ref[...]
    qscale = HEAD_DIM ** -0.5 * math.log2(math.e)
    for h in range(N_Q_HEADS):
        x = q_ref[:, h * HEAD_DIM:(h + 1) * HEAD_DIM].astype(F32)
        y = _norm_rope(x, gq_ref[...], cos, sin) * qscale
        qo_ref[h] = y.T.astype(BF16)
    for g in range(N_KV_HEADS):
        x = kv_ref[:, g * HEAD_DIM:(g + 1) * HEAD_DIM].astype(F32)
        k_ref[g] = _norm_rope(x, gk_ref[...], cos, sin).astype(BF16)
        v = kv_ref[:, (N_KV_HEADS + g) * HEAD_DIM:(N_KV_HEADS + g + 1) * HEAD_DIM].astype(F32)
        vT_ref[g, 0, :HEAD_DIM, :] = v.T.astype(BF16)
        pad_row = lax.broadcasted_iota(jnp.int32, (V_PAD_ROWS, tt), 0)
        vT_ref[g, 0, HEAD_DIM:, :] = jnp.where(pad_row == 0, 1.0, 0.0).astype(BF16)


def _qkprep(u3, cos, sin, gq, gk, tk):
    B, T, _ = u3.shape
    qw = N_Q_HEADS * HEAD_DIM
    kvw = 2 * N_KV_HEADS * HEAD_DIM
    vec = pl.BlockSpec((1, HEAD_DIM), lambda b, i: (0, 0))
    tab = pl.BlockSpec((tk, HEAD_DIM), lambda b, i: (i, 0))
    return pl.pallas_call(
        _qkprep_kernel,
        grid=(B, T // tk),
        in_specs=[pl.BlockSpec((None, tk, qw), lambda b, i: (b, i, B_Q // qw)),
                  pl.BlockSpec((None, tk, kvw), lambda b, i: (b, i, B_KV // kvw)),
                  tab, tab, vec, vec],
        out_specs=[pl.BlockSpec((None, N_Q_HEADS, HEAD_DIM, tk), lambda b, i: (b, 0, 0, i)),
                   pl.BlockSpec((None, N_KV_HEADS, tk, HEAD_DIM), lambda b, i: (b, 0, i, 0)),
                   pl.BlockSpec((None, N_KV_HEADS, 1, HEAD_DIM + V_PAD_ROWS, tk), lambda b, i: (b, 0, i, 0, 0))],
        out_shape=[jax.ShapeDtypeStruct((B, N_Q_HEADS, HEAD_DIM, T), BF16),
                   jax.ShapeDtypeStruct((B, N_KV_HEADS, T, HEAD_DIM), BF16),
                   jax.ShapeDtypeStruct((B, N_KV_HEADS, T // tk, HEAD_DIM + V_PAD_ROWS, tk), BF16)],
        compiler_params=_cparams(("parallel", "parallel")),
        name="qkprep",
    )(u3, u3, cos, sin, gq, gk)


def _col_reduce(op, x):
    rows, cols = x.shape
    if rows % 64 == 0:
        x = op(x.reshape(8, rows // 8, cols), axis=0)
    return op(x, axis=0, keepdims=True)


def _attn_kernel(q_ref, k_ref, vT_ref, o_ref, acc_ref, s_ref):
    nk, _, tk = vT_ref.shape
    tq = q_ref.shape[2]
    unroll = next(u for u in (4, 2) if nk % u == 0)
    acc_ref[...] = jnp.zeros_like(acc_ref)

    def scores(j, slot):
        k = k_ref[pl.ds(pl.multiple_of(j * tk, tk), tk), :]
        tile_max = []
        for h in range(Q_PER_KV):
            sT = jnp.dot(k, q_ref[h], preferred_element_type=F32)
            s_ref[slot, h] = sT
            tile_max.append(_col_reduce(jnp.max, sT))
        return tuple(tile_max)

    def consume(j, slot, tile_max, ms):
        vT = vT_ref[j]
        new_ms = []
        for h in range(Q_PER_KV):
            m_new = jnp.maximum(ms[h], tile_max[h])
            alpha = jnp.exp2(ms[h] - m_new)
            p = jnp.exp2(s_ref[slot, h] - m_new).astype(BF16)
            pv = jnp.dot(vT, p, preferred_element_type=F32)
            acc_ref[h] = alpha * acc_ref[h] + pv
            new_ms.append(m_new)
        return tuple(new_ms)

    def body(i, carry):
        tile_max, ms = carry
        j = unroll * i
        for u in range(unroll):
            nxt = j + u + 1
            if u == unroll - 1:
                nxt = jnp.minimum(nxt, nk - 1)
            next_max = scores(nxt, (u + 1) % 2)
            ms = consume(j + u, u % 2, tile_max, ms)
            tile_max = next_max
        return tile_max, ms

    m0 = tuple(jnp.full((1, tq), -jnp.inf, F32) for _ in range(Q_PER_KV))
    lax.fori_loop(0, nk // unroll, body, (scores(0, 0), m0))
    for h in range(Q_PER_KV):
        a = acc_ref[h]
        o = a[:HEAD_DIM] * (1.0 / a[HEAD_DIM:HEAD_DIM + 1])
        o_ref[:, h * HEAD_DIM:(h + 1) * HEAD_DIM] = o.T.astype(BF16)


def _attention(q, k, vT):
    B, _, _, T = q.shape
    nk, hv, tk = vT.shape[2], vT.shape[3], vT.shape[4]
    tq = _tile(T, 256, 128)
    gw = Q_PER_KV * HEAD_DIM
    return pl.pallas_call(
        _attn_kernel,
        grid=(B, N_KV_HEADS, T // tq),
        in_specs=[pl.BlockSpec((None, Q_PER_KV, HEAD_DIM, tq), lambda b, g, i: (b, g, 0, i)),
                  pl.BlockSpec((None, None, T, HEAD_DIM), lambda b, g, i: (b, g, 0, 0)),
                  pl.BlockSpec((None, None, nk, hv, tk), lambda b, g, i: (b, g, 0, 0, 0))],
        out_specs=pl.BlockSpec((None, tq, gw), lambda b, g, i: (b, i, g)),
        out_shape=jax.ShapeDtypeStruct((B, T, N_Q_HEADS * HEAD_DIM), BF16),
        scratch_shapes=[pltpu.VMEM((Q_PER_KV, hv, tq), F32),
                        pltpu.VMEM((2, Q_PER_KV, tk, tq), F32)],
        compiler_params=_cparams(("parallel", "parallel", "arbitrary")),
        name="attention",
    )(q, k, vT)


def _log_sigmoid(z):
    return jnp.minimum(z, 0.0) - jnp.log(1.0 + jnp.exp(-jnp.abs(z)))


def _gla_kernel(qf_ref, kf_ref, vf_ref, lrf_ref, qb_ref, kb_ref, vb_ref, lrb_ref,
                w2_ref, gb_ref, of_ref, ob_ref, s_ref):
    i = pl.program_id(1)
    tb = qf_ref.shape[0]
    L = GLA_CHUNK
    nchunk = tb // L

    @pl.when(i == 0)
    def _():
        s_ref[...] = jnp.zeros_like(s_ref)

    row = lax.broadcasted_iota(jnp.int32, (L, L), 0)
    col = lax.broadcasted_iota(jnp.int32, (L, L), 1)
    masks = (row >= col, col >= row)
    ones = tuple(jnp.where(m, 1.0, 0.0).astype(BF16) for m in masks)
    qscale = GLA_DK ** -0.5
    dirs = ((qf_ref, kf_ref, vf_ref, lrf_ref, of_ref), (qb_ref, kb_ref, vb_ref, lrb_ref, ob_ref))

    log_gate = []
    for d in range(2):
        z = jnp.dot(dirs[d][3][...], w2_ref[d], preferred_element_type=F32) + gb_ref[d]
        log_gate.append(_log_sigmoid(z) * (1.0 / GLA_GATE_NORM))

    cum = {}
    for step in range(nchunk):
        for d in range(2):
            c = step if d == 0 else nchunk - 1 - step
            g = log_gate[d][c * L:(c + 1) * L, :]
            g_hi = g.astype(BF16)
            g_lo = (g - g_hi.astype(F32)).astype(BF16)
            cum[step, d] = (jnp.dot(ones[d], g_hi, preferred_element_type=F32)
                            + jnp.dot(ones[d], g_lo, preferred_element_type=F32))
    scaled = {}
    for step in range(nchunk):
        for d in range(2):
            q_ref, k_ref = dirs[d][0], dirs[d][1]
            c = step if d == 0 else nchunk - 1 - step
            rows = slice(c * L, (c + 1) * L)
            for h in range(GLA_HEADS):
                ks = slice(h * GLA_DK, (h + 1) * GLA_DK)
                b = cum[step, d][:, ks]
                b_tot = b[L - 1:L, :] if d == 0 else b[0:1, :]
                q = q_ref[rows, ks].astype(F32) * qscale
                k = k_ref[rows, ks].astype(F32)
                q_t = (q * jnp.exp(b)).astype(BF16)
                k_t = (k * jnp.exp(-b)).astype(BF16)
                k_d = (k * jnp.exp(b_tot - b)).astype(BF16)
                scaled[step, d, h] = (q_t, k_t, k_d, jnp.exp(b_tot))
    pre = {}
    for step in range(nchunk):
        for d in range(2):
            for h in range(GLA_HEADS):
                q_t, k_t, k_d, e_tot = scaled[step, d, h]
                att = lax.dot_general(q_t, k_t, (((1,), (1,)), ((), ())), preferred_element_type=F32)
                att = jnp.where(masks[d], att, 0.0).astype(BF16)
                pre[step, d, h] = (q_t, k_d, att, e_tot)

    for step in range(nchunk):
        for d in range(2):
            v_ref, o_ref = dirs[d][2], dirs[d][4]
            c = step if d == 0 else nchunk - 1 - step
            rows = slice(c * L, (c + 1) * L)
            for h in range(GLA_HEADS):
                vs = slice(h * GLA_DV, (h + 1) * GLA_DV)
                q_t, k_d, att, e_tot = pre[step, d, h]
                v = v_ref[rows, vs]
                sT = s_ref[d, h]
                o = (jnp.dot(att, v, preferred_element_type=F32)
                     + lax.dot_general(q_t, sT.astype(BF16), (((1,), (1,)), ((), ())),
                                       preferred_element_type=F32))
                o_ref[rows, vs] = o.astype(o_ref.dtype)
                dsT = lax.dot_general(v, k_d, (((0,), (0,)), ((), ())), preferred_element_type=F32)
                s_ref[d, h] = sT * e_tot + dsT


def _gla(u3, w2p, gb):
    B, T, _ = u3.shape
    tb = _tile(T, 512, GLA_CHUNK)
    nb = T // tb
    qk_w = GLA_HEADS * GLA_DK
    v_w = GLA_HEADS * GLA_DV

    def specs(idx):
        return [pl.BlockSpec((None, tb, qk_w), lambda b, i: (b, idx(i), C_Q // qk_w)),
                pl.BlockSpec((None, tb, qk_w), lambda b, i: (b, idx(i), C_K // qk_w)),
                pl.BlockSpec((None, tb, v_w), lambda b, i: (b, idx(i), C_V // v_w)),
                pl.BlockSpec((None, tb, 128), lambda b, i: (b, idx(i), C_LR // 128))]

    fwd = lambda i: i
    bwd = lambda i: nb - 1 - i
    out_sd = jax.ShapeDtypeStruct((B, T, v_w), BF16)
    return pl.pallas_call(
        _gla_kernel,
        grid=(B, nb),
        in_specs=specs(fwd) + specs(bwd) + [
            pl.BlockSpec((2, 128, qk_w), lambda b, i: (0, 0, 0)),
            pl.BlockSpec((2, 1, qk_w), lambda b, i: (0, 0, 0))],
        out_specs=[pl.BlockSpec((None, tb, v_w), lambda b, i: (b, i, 0)),
                   pl.BlockSpec((None, tb, v_w), lambda b, i: (b, nb - 1 - i, 0))],
        out_shape=[out_sd, out_sd],
        scratch_shapes=[pltpu.VMEM((2, GLA_HEADS, GLA_DV, GLA_DK), F32)],
        compiler_params=_cparams(("parallel", "arbitrary")),
        name="gla",
    )(u3, u3, u3, u3, u3, u3, u3, u3, w2p, gb)


def _merge_kernel(tiles_a, oa_ref, ob_ref, bz_ref, cf_ref, cb_ref, cz_ref, m0_ref, m1_ref, m2_ref,
                  *rest):
    *h_refs, wb_ref, wo_ref, gn_ref, pg_ref, o_ref, mg_even, mg_odd = rest
    s = pl.program_id(0)

    @pl.when(s == 0)
    def _():
        mg_odd[...] = jnp.zeros_like(mg_odd)

    def step(mg_read, mg_write):
        out = jnp.dot(mg_read[...], wo_ref[...], preferred_element_type=F32)
        o_a = oa_ref[...]
        o_b = (ob_ref[...].astype(F32) * _silu(bz_ref[...].astype(F32))).astype(BF16)
        parts = []
        for h in range(GLA_HEADS):
            vs = slice(h * GLA_DV, (h + 1) * GLA_DV)
            x = cf_ref[:, vs].astype(F32) + cb_ref[:, vs].astype(F32)
            ms = jnp.mean(x * x, axis=-1, keepdims=True)
            y = x * lax.rsqrt(ms + EPS) * gn_ref[...]
            parts.append((y * _silu(cz_ref[:, vs].astype(F32))).astype(BF16))
        o_c = jnp.concatenate(parts, axis=-1)
        merged = None
        for n, (o, m_ref) in enumerate(((o_a, m0_ref), (o_b, m1_ref), (o_c, m2_ref))):
            term = _sigmoid(m_ref[...].astype(F32)) * jnp.dot(o, wb_ref[n], preferred_element_type=F32)
            merged = term if merged is None else merged + term
        ms = jnp.mean(out * out, axis=-1, keepdims=True)
        h = _pick(tiles_a + 1, *h_refs) if tiles_a else h_refs[0][...]
        o_ref[...] = h + out * lax.rsqrt(ms + EPS) * pg_ref[...]
        mg_write[...] = merged.astype(BF16)

    @pl.when(s % 2 == 0)
    def _():
        step(mg_odd, mg_even)

    @pl.when(s % 2 == 1)
    def _():
        step(mg_even, mg_odd)


def _merge(o_a, o_b, o_cf, o_cb, u, h_srcs, tm, wb, wo, gla_g, post_g):
    n = u.shape[0]
    n_tiles = n // tm
    tiles_a = h_srcs[0].shape[0] // tm if len(h_srcs) == 2 else None
    W = BRANCH_W

    def cur(width, col=0):
        return pl.BlockSpec((tm, width), lambda s: (jnp.minimum(s, n_tiles - 1), col))

    def prev(width):
        return pl.BlockSpec((tm, width), lambda s: (jnp.maximum(s - 1, 0), 0))

    def const(shape):
        return pl.BlockSpec(shape, lambda s: (0,) * len(shape), pipeline_mode=pl.Buffered(1))

    if tiles_a:
        h_specs = (pl.BlockSpec((tm, D_MODEL), lambda s: (jnp.clip(s - 1, 0, tiles_a - 1), 0)),
                   pl.BlockSpec((tm, D_MODEL), lambda s: (jnp.maximum(s - 1 - tiles_a, 0), 0)))
    else:
        h_specs = (prev(D_MODEL),)
    return pl.pallas_call(
        functools.partial(_merge_kernel, tiles_a),
        grid=(n_tiles + 1,),
        in_specs=[cur(W), cur(W), cur(W, B_Z // W), cur(W), cur(W), cur(W, C_Z // W),
                  cur(D_MODEL, M_LOG // D_MODEL), cur(D_MODEL, M_LOG // D_MODEL + 1),
                  cur(D_MODEL, M_LOG // D_MODEL + 2),
                  *h_specs,
                  const((3, W, D_MODEL)), const((D_MODEL, D_MODEL)),
                  const((1, GLA_DV)), const((1, D_MODEL))],
        out_specs=prev(D_MODEL),
        out_shape=jax.ShapeDtypeStruct((n, D_MODEL), F32),
        scratch_shapes=[pltpu.VMEM((tm, D_MODEL), BF16), pltpu.VMEM((tm, D_MODEL), BF16)],
        compiler_params=_cparams(("arbitrary",)),
        name="merge",
    )(o_a, o_b, u, o_cf, o_cb, u, u, u, u, *h_srcs, wb, wo, gla_g, post_g)


def _ple_kernel(tiles_a, last, h_ref, pa_ref, pb_ref, wp_ref, wg_ref, ng_ref, *out_refs):
    h = h_ref[...]
    p = _pick(tiles_a, pa_ref, pb_ref)
    e = jnp.dot(p.astype(BF16), wp_ref[...], preferred_element_type=F32)
    gate = _sigmoid(jnp.dot(h.astype(BF16), wg_ref[...], preferred_element_type=F32))
    h2 = h + gate * e
    if last:
        ya_ref, yb_ref = out_refs
        i = pl.program_id(0)

        @pl.when(i < tiles_a)
        def _():
            ya_ref[...] = h2

        @pl.when(i >= tiles_a)
        def _():
            yb_ref[...] = h2
    else:
        o_ref, xn_ref = out_refs
        o_ref[...] = h2
        ms = jnp.mean(h2 * h2, axis=-1, keepdims=True)
        xn_ref[...] = (h2 * lax.rsqrt(ms + EPS) * ng_ref[...]).astype(BF16)


def _ple(h, p_a, p_b, rows_a, layer, tm, wp, wg, next_g, last):
    n = h.shape[0]
    tiles_a = rows_a // tm
    tiles_b = (n - rows_a) // tm

    def rows(width):
        return pl.BlockSpec((tm, width), lambda i: (i, 0))

    def const(shape):
        return pl.BlockSpec(shape, lambda i: (0,) * len(shape), pipeline_mode=pl.Buffered(1))

    if last:
        out_specs = list(_split_specs(tm, D_MODEL, tiles_a))
        out_shape = [jax.ShapeDtypeStruct((rows_a, D_MODEL), F32),
                     jax.ShapeDtypeStruct((n - rows_a, D_MODEL), F32)]
    else:
        out_specs = [rows(D_MODEL), rows(D_MODEL)]
        out_shape = [jax.ShapeDtypeStruct((n, D_MODEL), F32), jax.ShapeDtypeStruct((n, D_MODEL), BF16)]
    return pl.pallas_call(
        functools.partial(_ple_kernel, tiles_a, last),
        grid=(n // tm,),
        in_specs=[rows(D_MODEL),
                  *_split_specs(tm, PLE_DIM, tiles_a, layer * tiles_a, layer * tiles_b),
                  const((PLE_DIM, D_MODEL)), const((D_MODEL, D_MODEL)), const((1, D_MODEL))],
        out_specs=out_specs,
        out_shape=out_shape,
        compiler_params=_cparams(("arbitrary",)),
        name="ple",
    )(h, p_a, p_b, wp, wg, next_g)


def _reorder_w_in(w_in):
    sizes = (1024, 1024, 1024, 1024, 256, 256, 1024, 512, 512, 1024, 2 * GLA_RANK, 1024, 3 * D_MODEL)
    names = ("a_val", "a_glu", "a_z", "b_q", "b_k", "b_v", "b_z", "c_q", "c_k", "c_v", "c_lr", "c_z", "m")
    parts, start = {}, 0
    for name, s in zip(names, sizes):
        parts[name] = w_in[..., start:start + s].astype(BF16)
        start += s
    parts["b_q"] = _permute_heads(parts["b_q"], N_Q_HEADS)
    parts["b_k"] = _permute_heads(parts["b_k"], N_KV_HEADS)
    pad = jnp.zeros(w_in.shape[:-1] + (U_COLS - C_LR - 2 * GLA_RANK,), BF16)
    order = ("a_val", "a_glu", "a_z", "b_q", "b_z", "c_v", "c_z", "c_q", "c_k", "m", "b_k", "b_v", "c_lr")
    w = jnp.concatenate([parts[k] for k in order] + [pad], axis=-1)
    assert w.shape[-1] == U_COLS
    return w


def _rope_tables(T):
    n_rows = T // GRID_W
    row_idx = jnp.repeat(jnp.arange(n_rows), GRID_W).astype(F32)
    col_idx = jnp.tile(jnp.arange(GRID_W), n_rows).astype(F32)
    inv = ROPE_THETA ** (-jnp.arange(ROPE_PAIRS, dtype=F32) / ROPE_PAIRS)
    ang_r = row_idx[:, None] * inv
    ang_c = col_idx[:, None] * inv
    cos = jnp.concatenate([jnp.cos(ang_r), jnp.cos(ang_c), jnp.cos(ang_r), jnp.cos(ang_c)], axis=-1)
    sin = jnp.concatenate([-jnp.sin(ang_r), -jnp.sin(ang_c), jnp.sin(ang_r), jnp.sin(ang_c)], axis=-1)
    return cos, sin


_ROPE_PERM_BLOCKS = (0, 2, 1, 3)


def _permute_heads(w, n_heads):
    shaped = w.reshape(w.shape[:-1] + (n_heads, HEAD_DIM))
    blocks = [shaped[..., b * ROPE_PAIRS:(b + 1) * ROPE_PAIRS] for b in _ROPE_PERM_BLOCKS]
    return jnp.concatenate(blocks, axis=-1).reshape(w.shape)


def _gla_gate_weights(gate_w2):
    depth = gate_w2.shape[0]
    w = jnp.zeros((depth, 2, 128, GLA_HEADS * GLA_DK), F32)
    for d in range(2):
        w = w.at[:, d, d * GLA_RANK:(d + 1) * GLA_RANK, :].set(gate_w2[:, d])
    return w.astype(BF16)


def kernel(x_prompt, x_sample, p_prompt, p_sample, pre_norm_g, post_norm_g, w_in, conv_w, conv_b, conv_ln_g, conv_ln_b, q_norm_g, k_norm_g, gla_gate_w2, gla_gate_b, gla_norm_g, w_branch, w_out, w_ple, w_ple_gate):
    depth = w_in.shape[0]
    T = x_prompt.shape[1]
    assert x_sample.shape[1] == T
    B = x_prompt.shape[0] + x_sample.shape[0]
    n = B * T
    x_a = x_prompt.reshape(-1, D_MODEL)
    x_b = x_sample.reshape(-1, D_MODEL)
    rows_a = x_a.shape[0]
    p_a = p_prompt.reshape(-1, PLE_DIM)
    p_b = p_sample.reshape(-1, PLE_DIM)
    tm_merge = _tile(T, 256, 16)
    tm_ple = _tile(T, 512, 16)

    w_in_r = [_reorder_w_in(w_in[l]) for l in range(depth)]
    w_branch_b = [w_branch[l].astype(BF16) for l in range(depth)]
    w_out_b = [w_out[l].astype(BF16) for l in range(depth)]
    w_ple_b = [w_ple[l].astype(BF16) for l in range(depth)]
    w_gate_b = [w_ple_gate[l].astype(BF16) for l in range(depth)]
    w2p = _gla_gate_weights(gla_gate_w2)
    conv_w_p = jnp.pad(conv_w, ((0, 0), (0, 1), (0, 0)))
    cos, sin = _rope_tables(T)
    gq_perm = _permute_heads(q_norm_g, 1)
    gk_perm = _permute_heads(k_norm_g, 1)
    tk = _tile(T // 2, 512, 128)

    xn = _prenorm(x_a, x_b, pre_norm_g[0][None], tm_merge)
    h_srcs = (x_a, x_b)
    for l in range(depth):
        u = _inproj(xn, w_in_r[l])
        u3 = u.reshape(B, T, U_COLS)
        o_a = _conv_branch(u3, conv_w_p[l], conv_b[l][None], conv_ln_g[l][None], conv_ln_b[l][None])
        q, k, vT = _qkprep(u3, cos, sin, gq_perm[l][None], gk_perm[l][None], tk)
        o_b = _attention(q, k, vT)
        o_cf, o_cb = _gla(u3, w2p[l], gla_gate_b[l][:, None, :])
        h = _merge(o_a.reshape(n, BRANCH_W), o_b.reshape(n, BRANCH_W), o_cf.reshape(n, BRANCH_W),
                   o_cb.reshape(n, BRANCH_W), u, h_srcs, tm_merge,
                   w_branch_b[l], w_out_b[l], gla_norm_g[l][None], post_norm_g[l][None])
        last = l == depth - 1
        next_g = pre_norm_g[(l + 1) % depth][None]
        outs = _ple(h, p_a, p_b, rows_a, l, tm_ple, w_ple_b[l], w_gate_b[l], next_g, last)
        if last:
            y_prompt, y_sample = outs
        else:
            h, xn = outs
            h_srcs = (h,)
    return (y_prompt.reshape(x_prompt.shape), y_sample.reshape(x_sample.shape))
```

```python
import functools
import math

import jax
import jax.numpy as jnp
from jax import lax
from jax.experimental import pallas as pl
from jax.experimental.pallas import tpu as pltpu

F32 = jnp.float32
BF16 = jnp.bfloat16

D_MODEL = 2048
GRID_W = 64
PLE_DIM = 256
BRANCH_W = 1024
EPS = 1e-6
CONV_K = 31
CONV_HALO = 16
N_Q_HEADS = 8
N_KV_HEADS = 2
Q_PER_KV = N_Q_HEADS // N_KV_HEADS
HEAD_DIM = 128
V_PAD_ROWS = 16
ROPE_PAIRS = HEAD_DIM // 4
ROPE_THETA = 10000.0
GLA_HEADS = 4
GLA_DK = 128
GLA_DV = 256
GLA_RANK = 16
GLA_GATE_NORM = 16.0
GLA_CHUNK = 64

A_VAL, A_GLU, A_Z = 0, 1024, 2048
B_Q, B_Z = 3072, 4096
C_V, C_Z = 5120, 6144
C_Q, C_K = 7168, 7680
M_LOG = 8192
B_KV = 14336
C_LR = 14848
MXU_COLS = 256
INPROJ_TN = 6 * MXU_COLS
U_COLS = 15360

VMEM_LIMIT = 56 * 1024 * 1024


def _cparams(sem, vmem=VMEM_LIMIT):
    return pltpu.CompilerParams(dimension_semantics=sem, vmem_limit_bytes=vmem)


def _tile(n, cap, mult=8):
    t = min(cap, n)
    while t > mult and (n % t or t % mult):
        t -= mult
    assert n % t == 0, (n, cap, mult)
    return t


def _sigmoid(x):
    return 1.0 / (1.0 + jnp.exp(-x))


def _silu(x):
    return x * _sigmoid(x)


def _split_specs(tm, width, tiles_a, a_first=0, b_first=0, col=0):
    a = pl.BlockSpec((tm, width), lambda i: (a_first + jnp.minimum(i, tiles_a - 1), col))
    b = pl.BlockSpec((tm, width), lambda i: (b_first + jnp.maximum(i - tiles_a, 0), col))
    return a, b


def _pick(tiles_a, a_ref, b_ref):
    return jnp.where(pl.program_id(0) < tiles_a, a_ref[...], b_ref[...])


def _prenorm_kernel(tiles_a, xa_ref, xb_ref, g_ref, o_ref):
    x = _pick(tiles_a, xa_ref, xb_ref)
    ms = jnp.mean(x * x, axis=-1, keepdims=True)
    o_ref[...] = (x * lax.rsqrt(ms + EPS) * g_ref[...]).astype(BF16)


def _prenorm(xa, xb, g, tm):
    n = xa.shape[0] + xb.shape[0]
    tiles_a = xa.shape[0] // tm
    return pl.pallas_call(
        functools.partial(_prenorm_kernel, tiles_a),
        grid=(n // tm,),
        in_specs=[*_split_specs(tm, D_MODEL, tiles_a),
                  pl.BlockSpec((1, D_MODEL), lambda i: (0, 0))],
        out_specs=pl.BlockSpec((tm, D_MODEL), lambda i: (i, 0)),
        out_shape=jax.ShapeDtypeStruct((n, D_MODEL), BF16),
        compiler_params=_cparams(("arbitrary",)),
        name="prenorm",
    )(xa, xb, g)


def _inproj_kernel(x_ref, w_ref, u_ref):
    u_ref[...] = jnp.dot(x_ref[...], w_ref[...], preferred_element_type=F32).astype(BF16)


def _inproj(xn, w):
    n = xn.shape[0]
    tm = _tile(n, 1024, 16)
    tn = INPROJ_TN
    return pl.pallas_call(
        _inproj_kernel,
        grid=(n // tm, U_COLS // tn),
        in_specs=[pl.BlockSpec((tm, D_MODEL), lambda i, j: (i, 0)),
                  pl.BlockSpec((D_MODEL, tn), lambda i, j: (0, j))],
        out_specs=pl.BlockSpec((tm, tn), lambda i, j: (i, j)),
        out_shape=jax.ShapeDtypeStruct((n, U_COLS), BF16),
        compiler_params=_cparams(("parallel", "arbitrary")),
        name="inproj",
    )(xn, w)


def _conv_kernel(val_ref, glu_ref, z_ref, pval_ref, pglu_ref, nval_ref, nglu_ref,
                 w_ref, cb_ref, lg_ref, lb_ref, o_ref, abuf, cbuf):
    i = pl.program_id(1)
    n = pl.num_programs(1)
    tt = val_ref.shape[0]
    H = CONV_HALO

    def glu(v_ref, g_ref):
        return v_ref[...].astype(F32) * _sigmoid(g_ref[...].astype(F32))

    abuf[0:H, :] = glu(pval_ref, pglu_ref) * (i > 0).astype(F32)
    abuf[H:H + tt, :] = glu(val_ref, glu_ref)
    abuf[H + tt:2 * H + tt, :] = glu(nval_ref, nglu_ref) * (i < n - 1).astype(F32)

    assert H - CONV_K // 2 == 1 and CONV_K < 32
    for c in range(BRANCH_W // 128):
        cs = slice(c * 128, (c + 1) * 128)
        acc = None
        for r in range(8):
            part = None
            for q in range(4):
                j = 8 * q + r
                if j == 0:
                    continue
                term = abuf[8 * q:8 * q + tt + 8, cs] * w_ref[j - 1:j, cs]
                part = term if part is None else part + term
            shifted = part[r:r + tt, :]
            acc = shifted if acc is None else acc + shifted
        cbuf[:, cs] = acc + cb_ref[:, cs]

    x = cbuf[...]
    mu = jnp.mean(x, axis=-1, keepdims=True)
    xc = x - mu
    var = jnp.mean(xc * xc, axis=-1, keepdims=True)
    y = xc * lax.rsqrt(var + EPS) * lg_ref[...] + lb_ref[...]
    o_ref[...] = (_silu(y) * _silu(z_ref[...].astype(F32))).astype(BF16)


def _conv_branch(u3, conv_w, conv_b, ln_g, ln_b):
    B, T, _ = u3.shape
    tt = _tile(T, 128, 16)
    H = CONV_HALO
    hb = tt // H
    nh = T // H

    def cur(col):
        return pl.BlockSpec((None, tt, BRANCH_W), lambda b, i: (b, i, col))

    def prev(col):
        return pl.BlockSpec((None, H, BRANCH_W), lambda b, i: (b, jnp.maximum(i * hb - 1, 0), col))

    def nxt(col):
        return pl.BlockSpec((None, H, BRANCH_W), lambda b, i: (b, jnp.minimum((i + 1) * hb, nh - 1), col))

    vec = pl.BlockSpec((1, BRANCH_W), lambda b, i: (0, 0))
    return pl.pallas_call(
        _conv_kernel,
        grid=(B, T // tt),
        in_specs=[cur(A_VAL // BRANCH_W), cur(A_GLU // BRANCH_W), cur(A_Z // BRANCH_W),
                  prev(A_VAL // BRANCH_W), prev(A_GLU // BRANCH_W),
                  nxt(A_VAL // BRANCH_W), nxt(A_GLU // BRANCH_W),
                  pl.BlockSpec((CONV_K + 1, BRANCH_W), lambda b, i: (0, 0)), vec, vec, vec],
        out_specs=pl.BlockSpec((None, tt, BRANCH_W), lambda b, i: (b, i, 0)),
        out_shape=jax.ShapeDtypeStruct((B, T, BRANCH_W), BF16),
        scratch_shapes=[pltpu.VMEM((tt + 2 * H, BRANCH_W), F32), pltpu.VMEM((tt, BRANCH_W), F32)],
        compiler_params=_cparams(("parallel", "parallel")),
        name="conv",
    )(u3, u3, u3, u3, u3, u3, u3, conv_w, conv_b, ln_g, ln_b)


def _norm_rope(x, g, cos, sin):
    ms = jnp.mean(x * x, axis=-1, keepdims=True)
    y = x * lax.rsqrt(ms + EPS) * g
    return y * cos + pltpu.roll(y, HEAD_DIM // 2, 1) * sin


def _qkprep_kernel(q_ref, kv_ref, cos_ref, sin_ref, gq_ref, gk_ref, qo_ref, k_ref, vT_ref):
    tt = q_ref.shape[0]
    cos = cos_ref[...]
    sin = sin_ref[...]
    qscale = HEAD_DIM ** -0.5 * math.log2(math.e)
    for h in range(N_Q_HEADS):
        x = q_ref[:, h * HEAD_DIM:(h + 1) * HEAD_DIM].astype(F32)
        y = _norm_rope(x, gq_ref[...], cos, sin) * qscale
        qo_ref[h] = y.T.astype(BF16)
    for g in range(N_KV_HEADS):
        x = kv_ref[:, g * HEAD_DIM:(g + 1) * HEAD_DIM].astype(F32)
        k_ref[g] = _norm_rope(x, gk_ref[...], cos, sin).astype(BF16)
        v = kv_ref[:, (N_KV_HEADS + g) * HEAD_DIM:(N_KV_HEADS + g + 1) * HEAD_DIM].astype(F32)
        vT_ref[g, 0, :HEAD_DIM, :] = v.T.astype(BF16)
        pad_row = lax.broadcasted_iota(jnp.int32, (V_PAD_ROWS, tt), 0)
        vT_ref[g, 0, HEAD_DIM:, :] = jnp.where(pad_row == 0, 1.0, 0.0).astype(BF16)


def _qkprep(u3, cos, sin, gq, gk, tk):
    B, T, _ = u3.shape
    qw = N_Q_HEADS * HEAD_DIM
    kvw = 2 * N_KV_HEADS * HEAD_DIM
    vec = pl.BlockSpec((1, HEAD_DIM), lambda b, i: (0, 0))
    tab = pl.BlockSpec((tk, HEAD_DIM), lambda b, i: (i, 0))
    return pl.pallas_call(
        _qkprep_kernel,
        grid=(B, T // tk),
        in_specs=[pl.BlockSpec((None, tk, qw), lambda b, i: (b, i, B_Q // qw)),
                  pl.BlockSpec((None, tk, kvw), lambda b, i: (b, i, B_KV // kvw)),
                  tab, tab, vec, vec],
        out_specs=[pl.BlockSpec((None, N_Q_HEADS, HEAD_DIM, tk), lambda b, i: (b, 0, 0, i)),
                   pl.BlockSpec((None, N_KV_HEADS, tk, HEAD_DIM), lambda b, i: (b, 0, i, 0)),
                   pl.BlockSpec((None, N_KV_HEADS, 1, HEAD_DIM + V_PAD_ROWS, tk), lambda b, i: (b, 0, i, 0, 0))],
        out_shape=[jax.ShapeDtypeStruct((B, N_Q_HEADS, HEAD_DIM, T), BF16),
                   jax.ShapeDtypeStruct((B, N_KV_HEADS, T, HEAD_DIM), BF16),
                   jax.ShapeDtypeStruct((B, N_KV_HEADS, T // tk, HEAD_DIM + V_PAD_ROWS, tk), BF16)],
        compiler_params=_cparams(("parallel", "parallel")),
        name="qkprep",
    )(u3, u3, cos, sin, gq, gk)


def _col_reduce(op, x):
    rows, cols = x.shape
    if rows % 64 == 0:
        x = op(x.reshape(8, rows // 8, cols), axis=0)
    return op(x, axis=0, keepdims=True)


def _attn_kernel(q_ref, k_ref, vT_ref, o_ref, acc_ref, s_ref):
    nk, _, tk = vT_ref.shape
    tq = q_ref.shape[2]
    unroll = next(u for u in (8, 4, 2) if nk % u == 0)
    heads = range(Q_PER_KV)
    acc_ref[...] = jnp.zeros_like(acc_ref)

    def scores(j, slot, h):
        k = k_ref[pl.ds(pl.multiple_of(j * tk, tk), tk), :]
        sT = jnp.dot(k, q_ref[h], preferred_element_type=F32)
        s_ref[slot, h] = sT
        return _col_reduce(jnp.max, sT)

    def consume(j, slot, h, tile_max, m):
        m_new = jnp.maximum(m, tile_max)
        alpha = jnp.exp2(m - m_new)
        p = jnp.exp2(s_ref[slot, h] - m_new).astype(BF16)
        pv = jnp.dot(vT_ref[j], p, preferred_element_type=F32)
        acc_ref[h] = alpha * acc_ref[h] + pv
        return m_new

    def body(i, carry):
        tile_max, ms = carry
        j = unroll * i
        for u in range(unroll):
            nxt = j + u + 1
            if u == unroll - 1:
                nxt = jnp.minimum(nxt, nk - 1)
            next_max = []
            new_ms = []
            for h in heads:
                next_max.append(scores(nxt, (u + 1) % 2, h))
                new_ms.append(consume(j + u, u % 2, h, tile_max[h], ms[h]))
            tile_max, ms = tuple(next_max), tuple(new_ms)
        return tile_max, ms

    m0 = tuple(jnp.full((1, tq), -jnp.inf, F32) for _ in heads)
    lax.fori_loop(0, nk // unroll, body, (tuple(scores(0, 0, h) for h in heads), m0))
    for h in range(Q_PER_KV):
        a = acc_ref[h]
        o = a[:HEAD_DIM] * (1.0 / a[HEAD_DIM:HEAD_DIM + 1])
        o_ref[:, h * HEAD_DIM:(h + 1) * HEAD_DIM] = o.T.astype(BF16)


def _attention(q, k, vT):
    B, _, _, T = q.shape
    nk, hv, tk = vT.shape[2], vT.shape[3], vT.shape[4]
    tq = _tile(T, 256, 128)
    gw = Q_PER_KV * HEAD_DIM
    return pl.pallas_call(
        _attn_kernel,
        grid=(B, N_KV_HEADS, T // tq),
        in_specs=[pl.BlockSpec((None, Q_PER_KV, HEAD_DIM, tq), lambda b, g, i: (b, g, 0, i)),
                  pl.BlockSpec((None, None, T, HEAD_DIM), lambda b, g, i: (b, g, 0, 0)),
                  pl.BlockSpec((None, None, nk, hv, tk), lambda b, g, i: (b, g, 0, 0, 0))],
        out_specs=pl.BlockSpec((None, tq, gw), lambda b, g, i: (b, i, g)),
        out_shape=jax.ShapeDtypeStruct((B, T, N_Q_HEADS * HEAD_DIM), BF16),
        scratch_shapes=[pltpu.VMEM((Q_PER_KV, hv, tq), F32),
                        pltpu.VMEM((2, Q_PER_KV, tk, tq), F32)],
        compiler_params=_cparams(("parallel", "parallel", "arbitrary")),
        name="attention",
    )(q, k, vT)


def _log_sigmoid(z):
    return jnp.minimum(z, 0.0) - jnp.log(1.0 + jnp.exp(-jnp.abs(z)))


def _gla_kernel(qf_ref, kf_ref, vf_ref, lrf_ref, qb_ref, kb_ref, vb_ref, lrb_ref,
                w2_ref, gb_ref, of_ref, ob_ref, s_ref):
    i = pl.program_id(1)
    tb = qf_ref.shape[0]
    L = GLA_CHUNK
    nchunk = tb // L

    @pl.when(i == 0)
    def _():
        s_ref[...] = jnp.zeros_like(s_ref)

    row = lax.broadcasted_iota(jnp.int32, (L, L), 0)
    col = lax.broadcasted_iota(jnp.int32, (L, L), 1)
    masks = (row >= col, col >= row)
    ones = tuple(jnp.where(m, 1.0, 0.0).astype(BF16) for m in masks)
    qscale = GLA_DK ** -0.5
    dirs = ((qf_ref, kf_ref, vf_ref, lrf_ref, of_ref), (qb_ref, kb_ref, vb_ref, lrb_ref, ob_ref))

    log_gate = []
    for d in range(2):
        z = jnp.dot(dirs[d][3][...], w2_ref[d], preferred_element_type=F32) + gb_ref[d]
        log_gate.append(_log_sigmoid(z) * (1.0 / GLA_GATE_NORM))

    cum = {}
    for step in range(nchunk):
        for d in range(2):
            c = step if d == 0 else nchunk - 1 - step
            g = log_gate[d][c * L:(c + 1) * L, :]
            g_hi = g.astype(BF16)
            g_lo = (g - g_hi.astype(F32)).astype(BF16)
            cum[step, d] = (jnp.dot(ones[d], g_hi, preferred_element_type=F32)
                            + jnp.dot(ones[d], g_lo, preferred_element_type=F32))
    scaled = {}
    for step in range(nchunk):
        for d in range(2):
            q_ref, k_ref = dirs[d][0], dirs[d][1]
            c = step if d == 0 else nchunk - 1 - step
            rows = slice(c * L, (c + 1) * L)
            for h in range(GLA_HEADS):
                ks = slice(h * GLA_DK, (h + 1) * GLA_DK)
                b = cum[step, d][:, ks]
                b_tot = b[L - 1:L, :] if d == 0 else b[0:1, :]
                q = q_ref[rows, ks].astype(F32) * qscale
                k = k_ref[rows, ks].astype(F32)
                q_t = (q * jnp.exp(b)).astype(BF16)
                k_t = (k * jnp.exp(-b)).astype(BF16)
                k_d = (k * jnp.exp(b_tot - b)).astype(BF16)
                scaled[step, d, h] = (q_t, k_t, k_d, jnp.exp(b_tot))
    pre = {}
    for step in range(nchunk):
        for d in range(2):
            for h in range(GLA_HEADS):
                q_t, k_t, k_d, e_tot = scaled[step, d, h]
                att = lax.dot_general(q_t, k_t, (((1,), (1,)), ((), ())), preferred_element_type=F32)
                att = jnp.where(masks[d], att, 0.0).astype(BF16)
                pre[step, d, h] = (q_t, k_d, att, e_tot)

    for step in range(nchunk):
        for d in range(2):
            v_ref, o_ref = dirs[d][2], dirs[d][4]
            c = step if d == 0 else nchunk - 1 - step
            rows = slice(c * L, (c + 1) * L)
            for h in range(GLA_HEADS):
                vs = slice(h * GLA_DV, (h + 1) * GLA_DV)
                q_t, k_d, att, e_tot = pre[step, d, h]
                v = v_ref[rows, vs]
                sT = s_ref[d, h]
                o = (jnp.dot(att, v, preferred_element_type=F32)
                     + lax.dot_general(q_t, sT.astype(BF16), (((1,), (1,)), ((), ())),
                                       preferred_element_type=F32))
                o_ref[rows, vs] = o.astype(o_ref.dtype)
                dsT = lax.dot_general(v, k_d, (((0,), (0,)), ((), ())), preferred_element_type=F32)
                s_ref[d, h] = sT * e_tot + dsT


def _gla(u3, w2p, gb):
    B, T, _ = u3.shape
    tb = _tile(T, 512, GLA_CHUNK)
    nb = T // tb
    qk_w = GLA_HEADS * GLA_DK
    v_w = GLA_HEADS * GLA_DV

    def specs(idx):
        return [pl.BlockSpec((None, tb, qk_w), lambda b, i: (b, idx(i), C_Q // qk_w)),
                pl.BlockSpec((None, tb, qk_w), lambda b, i: (b, idx(i), C_K // qk_w)),
                pl.BlockSpec((None, tb, v_w), lambda b, i: (b, idx(i), C_V // v_w)),
                pl.BlockSpec((None, tb, 128), lambda b, i: (b, idx(i), C_LR // 128))]

    fwd = lambda i: i
    bwd = lambda i: nb - 1 - i
    out_sd = jax.ShapeDtypeStruct((B, T, v_w), BF16)
    return pl.pallas_call(
        _gla_kernel,
        grid=(B, nb),
        in_specs=specs(fwd) + specs(bwd) + [
            pl.BlockSpec((2, 128, qk_w), lambda b, i: (0, 0, 0)),
            pl.BlockSpec((2, 1, qk_w), lambda b, i: (0, 0, 0))],
        out_specs=[pl.BlockSpec((None, tb, v_w), lambda b, i: (b, i, 0)),
                   pl.BlockSpec((None, tb, v_w), lambda b, i: (b, nb - 1 - i, 0))],
        out_shape=[out_sd, out_sd],
        scratch_shapes=[pltpu.VMEM((2, GLA_HEADS, GLA_DV, GLA_DK), F32)],
        compiler_params=_cparams(("parallel", "arbitrary")),
        name="gla",
    )(u3, u3, u3, u3, u3, u3, u3, u3, w2p, gb)


def _merge_kernel(tiles_a, oa_ref, ob_ref, bz_ref, cf_ref, cb_ref, cz_ref, m0_ref, m1_ref, m2_ref,
                  *rest):
    *h_refs, wb_ref, wo_ref, gn_ref, pg_ref, o_ref, mg_even, mg_odd = rest
    s = pl.program_id(0)

    @pl.when(s == 0)
    def _():
        mg_odd[...] = jnp.zeros_like(mg_odd)

    def step(mg_read, mg_write):
        out = jnp.dot(mg_read[...], wo_ref[...], preferred_element_type=F32)
        o_a = oa_ref[...]
        o_b = (ob_ref[...].astype(F32) * _silu(bz_ref[...].astype(F32))).astype(BF16)
        parts = []
        for h in range(GLA_HEADS):
            vs = slice(h * GLA_DV, (h + 1) * GLA_DV)
            x = cf_ref[:, vs].astype(F32) + cb_ref[:, vs].astype(F32)
            ms = jnp.mean(x * x, axis=-1, keepdims=True)
            y = x * lax.rsqrt(ms + EPS) * gn_ref[...]
            parts.append((y * _silu(cz_ref[:, vs].astype(F32))).astype(BF16))
        o_c = jnp.concatenate(parts, axis=-1)
        merged = None
        for n, (o, m_ref) in enumerate(((o_a, m0_ref), (o_b, m1_ref), (o_c, m2_ref))):
            term = _sigmoid(m_ref[...].astype(F32)) * jnp.dot(o, wb_ref[n], preferred_element_type=F32)
            merged = term if merged is None else merged + term
        ms = jnp.mean(out * out, axis=-1, keepdims=True)
        h = _pick(tiles_a + 1, *h_refs) if tiles_a else h_refs[0][...]
        o_ref[...] = h + out * lax.rsqrt(ms + EPS) * pg_ref[...]
        mg_write[...] = merged.astype(BF16)

    @pl.when(s % 2 == 0)
    def _():
        step(mg_odd, mg_even)

    @pl.when(s % 2 == 1)
    def _():
        step(mg_even, mg_odd)


def _merge(o_a, o_b, o_cf, o_cb, u, h_srcs, tm, wb, wo, gla_g, post_g):
    n = u.shape[0]
    n_tiles = n // tm
    tiles_a = h_srcs[0].shape[0] // tm if len(h_srcs) == 2 else None
    W = BRANCH_W

    def cur(width, col=0):
        return pl.BlockSpec((tm, width), lambda s: (jnp.minimum(s, n_tiles - 1), col))

    def prev(width):
        return pl.BlockSpec((tm, width), lambda s: (jnp.maximum(s - 1, 0), 0))

    def const(shape):
        return pl.BlockSpec(shape, lambda s: (0,) * len(shape), pipeline_mode=pl.Buffered(1))

    if tiles_a:
        h_specs = (pl.BlockSpec((tm, D_MODEL), lambda s: (jnp.clip(s - 1, 0, tiles_a - 1), 0)),
                   pl.BlockSpec((tm, D_MODEL), lambda s: (jnp.maximum(s - 1 - tiles_a, 0), 0)))
    else:
        h_specs = (prev(D_MODEL),)
    return pl.pallas_call(
        functools.partial(_merge_kernel, tiles_a),
        grid=(n_tiles + 1,),
        in_specs=[cur(W), cur(W), cur(W, B_Z // W), cur(W), cur(W), cur(W, C_Z // W),
                  cur(D_MODEL, M_LOG // D_MODEL), cur(D_MODEL, M_LOG // D_MODEL + 1),
                  cur(D_MODEL, M_LOG // D_MODEL + 2),
                  *h_specs,
                  const((3, W, D_MODEL)), const((D_MODEL, D_MODEL)),
                  const((1, GLA_DV)), const((1, D_MODEL))],
        out_specs=prev(D_MODEL),
        out_shape=jax.ShapeDtypeStruct((n, D_MODEL), F32),
        scratch_shapes=[pltpu.VMEM((tm, D_MODEL), BF16), pltpu.VMEM((tm, D_MODEL), BF16)],
        compiler_params=_cparams(("arbitrary",)),
        name="merge",
    )(o_a, o_b, u, o_cf, o_cb, u, u, u, u, *h_srcs, wb, wo, gla_g, post_g)


def _ple_kernel(tiles_a, last, h_ref, pa_ref, pb_ref, wp_ref, wg_ref, ng_ref, *out_refs):
    h = h_ref[...]
    p = _pick(tiles_a, pa_ref, pb_ref)
    e = jnp.dot(p.astype(BF16), wp_ref[...], preferred_element_type=F32)
    gate = _sigmoid(jnp.dot(h.astype(BF16), wg_ref[...], preferred_element_type=F32))
    h2 = h + gate * e
    if last:
        ya_ref, yb_ref = out_refs
        i = pl.program_id(0)

        @pl.when(i < tiles_a)
        def _():
            ya_ref[...] = h2

        @pl.when(i >= tiles_a)
        def _():
            yb_ref[...] = h2
    else:
        o_ref, xn_ref = out_refs
        o_ref[...] = h2
        ms = jnp.mean(h2 * h2, axis=-1, keepdims=True)
        xn_ref[...] = (h2 * lax.rsqrt(ms + EPS) * ng_ref[...]).astype(BF16)


def _ple(h, p_a, p_b, rows_a, layer, tm, wp, wg, next_g, last):
    n = h.shape[0]
    tiles_a = rows_a // tm
    tiles_b = (n - rows_a) // tm

    def rows(width):
        return pl.BlockSpec((tm, width), lambda i: (i, 0))

    def const(shape):
        return pl.BlockSpec(shape, lambda i: (0,) * len(shape), pipeline_mode=pl.Buffered(1))

    if last:
        out_specs = list(_split_specs(tm, D_MODEL, tiles_a))
        out_shape = [jax.ShapeDtypeStruct((rows_a, D_MODEL), F32),
                     jax.ShapeDtypeStruct((n - rows_a, D_MODEL), F32)]
    else:
        out_specs = [rows(D_MODEL), rows(D_MODEL)]
        out_shape = [jax.ShapeDtypeStruct((n, D_MODEL), F32), jax.ShapeDtypeStruct((n, D_MODEL), BF16)]
    return pl.pallas_call(
        functools.partial(_ple_kernel, tiles_a, last),
        grid=(n // tm,),
        in_specs=[rows(D_MODEL),
                  *_split_specs(tm, PLE_DIM, tiles_a, layer * tiles_a, layer * tiles_b),
                  const((PLE_DIM, D_MODEL)), const((D_MODEL, D_MODEL)), const((1, D_MODEL))],
        out_specs=out_specs,
        out_shape=out_shape,
        compiler_params=_cparams(("arbitrary",)),
        name="ple",
    )(h, p_a, p_b, wp, wg, next_g)


def _reorder_w_in(w_in):
    sizes = (1024, 1024, 1024, 1024, 256, 256, 1024, 512, 512, 1024, 2 * GLA_RANK, 1024, 3 * D_MODEL)
    names = ("a_val", "a_glu", "a_z", "b_q", "b_k", "b_v", "b_z", "c_q", "c_k", "c_v", "c_lr", "c_z", "m")
    parts, start = {}, 0
    for name, s in zip(names, sizes):
        parts[name] = w_in[..., start:start + s].astype(BF16)
        start += s
    parts["b_q"] = _permute_heads(parts["b_q"], N_Q_HEADS)
    parts["b_k"] = _permute_heads(parts["b_k"], N_KV_HEADS)
    pad = jnp.zeros(w_in.shape[:-1] + (U_COLS - C_LR - 2 * GLA_RANK,), BF16)
    order = ("a_val", "a_glu", "a_z", "b_q", "b_z", "c_v", "c_z", "c_q", "c_k", "m", "b_k", "b_v", "c_lr")
    w = jnp.concatenate([parts[k] for k in order] + [pad], axis=-1)
    assert w.shape[-1] == U_COLS
    return w


def _rope_tables(T):
    n_rows = T // GRID_W
    row_idx = jnp.repeat(jnp.arange(n_rows), GRID_W).astype(F32)
    col_idx = jnp.tile(jnp.arange(GRID_W), n_rows).astype(F32)
    inv = ROPE_THETA ** (-jnp.arange(ROPE_PAIRS, dtype=F32) / ROPE_PAIRS)
    ang_r = row_idx[:, None] * inv
    ang_c = col_idx[:, None] * inv
    cos = jnp.concatenate([jnp.cos(ang_r), jnp.cos(ang_c), jnp.cos(ang_r), jnp.cos(ang_c)], axis=-1)
    sin = jnp.concatenate([-jnp.sin(ang_r), -jnp.sin(ang_c), jnp.sin(ang_r), jnp.sin(ang_c)], axis=-1)
    return cos, sin


_ROPE_PERM_BLOCKS = (0, 2, 1, 3)


def _permute_heads(w, n_heads):
    shaped = w.reshape(w.shape[:-1] + (n_heads, HEAD_DIM))
    blocks = [shaped[..., b * ROPE_PAIRS:(b + 1) * ROPE_PAIRS] for b in _ROPE_PERM_BLOCKS]
    return jnp.concatenate(blocks, axis=-1).reshape(w.shape)


def _gla_gate_weights(gate_w2):
    depth = gate_w2.shape[0]
    w = jnp.zeros((depth, 2, 128, GLA_HEADS * GLA_DK), F32)
    for d in range(2):
        w = w.at[:, d, d * GLA_RANK:(d + 1) * GLA_RANK, :].set(gate_w2[:, d])
    return w.astype(BF16)


def kernel(x_prompt, x_sample, p_prompt, p_sample, pre_norm_g, post_norm_g, w_in, conv_w, conv_b, conv_ln_g, conv_ln_b, q_norm_g, k_norm_g, gla_gate_w2, gla_gate_b, gla_norm_g, w_branch, w_out, w_ple, w_ple_gate):
    depth = w_in.shape[0]
    T = x_prompt.shape[1]
    assert x_sample.shape[1] == T
    B = x_prompt.shape[0] + x_sample.shape[0]
    n = B * T
    x_a = x_prompt.reshape(-1, D_MODEL)
    x_b = x_sample.reshape(-1, D_MODEL)
    rows_a = x_a.shape[0]
    p_a = p_prompt.reshape(-1, PLE_DIM)
    p_b = p_sample.reshape(-1, PLE_DIM)
    tm_merge = _tile(T, 256, 16)
    tm_ple = _tile(T, 512, 16)

    w_in_r = [_reorder_w_in(w_in[l]) for l in range(depth)]
    w_branch_b = [w_branch[l].astype(BF16) for l in range(depth)]
    w_out_b = [w_out[l].astype(BF16) for l in range(depth)]
    w_ple_b = [w_ple[l].astype(BF16) for l in range(depth)]
    w_gate_b = [w_ple_gate[l].astype(BF16) for l in range(depth)]
    w2p = _gla_gate_weights(gla_gate_w2)
    conv_w_p = jnp.pad(conv_w, ((0, 0), (0, 1), (0, 0)))
    cos, sin = _rope_tables(T)
    gq_perm = _permute_heads(q_norm_g, 1)
    gk_perm = _permute_heads(k_norm_g, 1)
    tk = _tile(T // 2, 512, 128)

    xn = _prenorm(x_a, x_b, pre_norm_g[0][None], tm_merge)
    h_srcs = (x_a, x_b)
    for l in range(depth):
        u = _inproj(xn, w_in_r[l])
        u3 = u.reshape(B, T, U_COLS)
        o_a = _conv_branch(u3, conv_w_p[l], conv_b[l][None], conv_ln_g[l][None], conv_ln_b[l][None])
        q, k, vT = _qkprep(u3, cos, sin, gq_perm[l][None], gk_perm[l][None], tk)
        o_b = _attention(q, k, vT)
        o_cf, o_cb = _gla(u3, w2p[l], gla_gate_b[l][:, None, :])
        h = _merge(o_a.reshape(n, BRANCH_W), o_b.reshape(n, BRANCH_W), o_cf.reshape(n, BRANCH_W),
                   o_cb.reshape(n, BRANCH_W), u, h_srcs, tm_merge,
                   w_branch_b[l], w_out_b[l], gla_norm_g[l][None], post_norm_g[l][None])
        last = l == depth - 1
        next_g = pre_norm_g[(l + 1) % depth][None]
        outs = _ple(h, p_a, p_b, rows_a, l, tm_ple, w_ple_b[l], w_gate_b[l], next_g, last)
        if last:
            y_prompt, y_sample = outs
        else:
            h, xn = outs
            h_srcs = (h,)
    return (y_prompt.reshape(x_prompt.shape), y_sample.reshape(x_sample.shape))
```

```python
import functools
import math

import jax
import jax.numpy as jnp
from jax import lax
from jax.experimental import pallas as pl
from jax.experimental.pallas import tpu as pltpu

F32 = jnp.float32
BF16 = jnp.bfloat16

D_MODEL = 2048
GRID_W = 64
PLE_DIM = 256
BRANCH_W = 1024
EPS = 1e-6
CONV_K = 31
CONV_HALO = 16
N_Q_HEADS = 8
N_KV_HEADS = 2
Q_PER_KV = N_Q_HEADS // N_KV_HEADS
HEAD_DIM = 128
V_PAD_ROWS = 16
ROPE_PAIRS = HEAD_DIM // 4
ROPE_THETA = 10000.0
GLA_HEADS = 4
GLA_DK = 128
GLA_DV = 256
GLA_RANK = 16
GLA_GATE_NORM = 16.0
GLA_CHUNK = 64

A_VAL, A_GLU, A_Z = 0, 1024, 2048
B_Q, B_Z = 3072, 4096
C_V, C_Z = 5120, 6144
C_Q, C_K = 7168, 7680
M_LOG = 8192
B_KV = 14336
C_LR = 14848
MXU_COLS = 256
INPROJ_TN = 6 * MXU_COLS
U_COLS = 15360

VMEM_LIMIT = 56 * 1024 * 1024


def _cparams(sem, vmem=VMEM_LIMIT):
    return pltpu.CompilerParams(dimension_semantics=sem, vmem_limit_bytes=vmem)


def _tile(n, cap, mult=8):
    t = min(cap, n)
    while t > mult and (n % t or t % mult):
        t -= mult
    assert n % t == 0, (n, cap, mult)
    return t


def _sigmoid(x):
    return 1.0 / (1.0 + jnp.exp(-x))


def _silu(x):
    return x * _sigmoid(x)


def _split_specs(tm, width, tiles_a, a_first=0, b_first=0, col=0):
    a = pl.BlockSpec((tm, width), lambda i: (a_first + jnp.minimum(i, tiles_a - 1), col))
    b = pl.BlockSpec((tm, width), lambda i: (b_first + jnp.maximum(i - tiles_a, 0), col))
    return a, b


def _pick(tiles_a, a_ref, b_ref):
    return jnp.where(pl.program_id(0) < tiles_a, a_ref[...], b_ref[...])


def _prenorm_kernel(tiles_a, xa_ref, xb_ref, g_ref, o_ref):
    x = _pick(tiles_a, xa_ref, xb_ref)
    ms = jnp.mean(x * x, axis=-1, keepdims=True)
    o_ref[...] = (x * lax.rsqrt(ms + EPS) * g_ref[...]).astype(BF16)


def _prenorm(xa, xb, g, tm):
    n = xa.shape[0] + xb.shape[0]
    tiles_a = xa.shape[0] // tm
    return pl.pallas_call(
        functools.partial(_prenorm_kernel, tiles_a),
        grid=(n // tm,),
        in_specs=[*_split_specs(tm, D_MODEL, tiles_a),
                  pl.BlockSpec((1, D_MODEL), lambda i: (0, 0))],
        out_specs=pl.BlockSpec((tm, D_MODEL), lambda i: (i, 0)),
        out_shape=jax.ShapeDtypeStruct((n, D_MODEL), BF16),
        compiler_params=_cparams(("arbitrary",)),
        name="prenorm",
    )(xa, xb, g)


def _inproj_kernel(x_ref, w_ref, u_ref):
    u_ref[...] = jnp.dot(x_ref[...], w_ref[...], preferred_element_type=F32).astype(BF16)


def _inproj(xn, w):
    n = xn.shape[0]
    tm = _tile(n, 1024, 16)
    tn = INPROJ_TN
    return pl.pallas_call(
        _inproj_kernel,
        grid=(n // tm, U_COLS // tn),
        in_specs=[pl.BlockSpec((tm, D_MODEL), lambda i, j: (i, 0)),
                  pl.BlockSpec((D_MODEL, tn), lambda i, j: (0, j))],
        out_specs=pl.BlockSpec((tm, tn), lambda i, j: (i, j)),
        out_shape=jax.ShapeDtypeStruct((n, U_COLS), BF16),
        compiler_params=_cparams(("parallel", "arbitrary")),
        name="inproj",
    )(xn, w)


def _conv_kernel(val_ref, glu_ref, z_ref, pval_ref, pglu_ref, nval_ref, nglu_ref,
                 w_ref, cb_ref, lg_ref, lb_ref, o_ref, abuf, cbuf):
    i = pl.program_id(1)
    n = pl.num_programs(1)
    tt = val_ref.shape[0]
    H = CONV_HALO

    def glu(v_ref, g_ref):
        return v_ref[...].astype(F32) * _sigmoid(g_ref[...].astype(F32))

    abuf[0:H, :] = glu(pval_ref, pglu_ref) * (i > 0).astype(F32)
    abuf[H:H + tt, :] = glu(val_ref, glu_ref)
    abuf[H + tt:2 * H + tt, :] = glu(nval_ref, nglu_ref) * (i < n - 1).astype(F32)

    assert H - CONV_K // 2 == 1 and CONV_K < 32
    for c in range(BRANCH_W // 128):
        cs = slice(c * 128, (c + 1) * 128)
        acc = None
        for r in range(8):
            part = None
            for q in range(4):
                j = 8 * q + r
                if j == 0:
                    continue
                term = abuf[8 * q:8 * q + tt + 8, cs] * w_ref[j - 1:j, cs]
                part = term if part is None else part + term
            shifted = part[r:r + tt, :]
            acc = shifted if acc is None else acc + shifted
        cbuf[:, cs] = acc + cb_ref[:, cs]

    x = cbuf[...]
    mu = jnp.mean(x, axis=-1, keepdims=True)
    xc = x - mu
    var = jnp.mean(xc * xc, axis=-1, keepdims=True)
    y = xc * lax.rsqrt(var + EPS) * lg_ref[...] + lb_ref[...]
    o_ref[...] = (_silu(y) * _silu(z_ref[...].astype(F32))).astype(BF16)


def _conv_branch(u3, conv_w, conv_b, ln_g, ln_b):
    B, T, _ = u3.shape
    tt = _tile(T, 128, 16)
    H = CONV_HALO
    hb = tt // H
    nh = T // H

    def cur(col):
        return pl.BlockSpec((None, tt, BRANCH_W), lambda b, i: (b, i, col))

    def prev(col):
        return pl.BlockSpec((None, H, BRANCH_W), lambda b, i: (b, jnp.maximum(i * hb - 1, 0), col))

    def nxt(col):
        return pl.BlockSpec((None, H, BRANCH_W), lambda b, i: (b, jnp.minimum((i + 1) * hb, nh - 1), col))

    vec = pl.BlockSpec((1, BRANCH_W), lambda b, i: (0, 0))
    return pl.pallas_call(
        _conv_kernel,
        grid=(B, T // tt),
        in_specs=[cur(A_VAL // BRANCH_W), cur(A_GLU // BRANCH_W), cur(A_Z // BRANCH_W),
                  prev(A_VAL // BRANCH_W), prev(A_GLU // BRANCH_W),
                  nxt(A_VAL // BRANCH_W), nxt(A_GLU // BRANCH_W),
                  pl.BlockSpec((CONV_K + 1, BRANCH_W), lambda b, i: (0, 0)), vec, vec, vec],
        out_specs=pl.BlockSpec((None, tt, BRANCH_W), lambda b, i: (b, i, 0)),
        out_shape=jax.ShapeDtypeStruct((B, T, BRANCH_W), BF16),
        scratch_shapes=[pltpu.VMEM((tt + 2 * H, BRANCH_W), F32), pltpu.VMEM((tt, BRANCH_W), F32)],
        compiler_params=_cparams(("parallel", "parallel")),
        name="conv",
    )(u3, u3, u3, u3, u3, u3, u3, conv_w, conv_b, ln_g, ln_b)


def _norm_rope(x, g, cos, sin):
    ms = jnp.mean(x * x, axis=-1, keepdims=True)
    y = x * lax.rsqrt(ms + EPS) * g
    return y * cos + pltpu.roll(y, HEAD_DIM // 2, 1) * sin


def _qkprep_kernel(q_ref, kv_ref, cos_ref, sin_ref, gq_ref, gk_ref, qo_ref, k_ref, vT_ref):
    tt = q_ref.shape[0]
    cos = cos_ref[...]
    sin = sin_ref[...]
    qscale = HEAD_DIM ** -0.5 * math.log2(math.e)
    for h in range(N_Q_HEADS):
        x = q_ref[:, h * HEAD_DIM:(h + 1) * HEAD_DIM].astype(F32)
        y = _norm_rope(x, gq_ref[...], cos, sin) * qscale
        qo_ref[h] = y.T.astype(BF16)
    for g in range(N_KV_HEADS):
        x = kv_ref[:, g * HEAD_DIM:(g + 1) * HEAD_DIM].astype(F32)
        k_ref[g] = _norm_rope(x, gk_ref[...], cos, sin).astype(BF16)
        v = kv_ref[:, (N_KV_HEADS + g) * HEAD_DIM:(N_KV_HEADS + g + 1) * HEAD_DIM].astype(F32)
        vT_ref[g, 0, :HEAD_DIM, :] = v.T.astype(BF16)
        pad_row = lax.broadcasted_iota(jnp.int32, (V_PAD_ROWS, tt), 0)
        vT_ref[g, 0, HEAD_DIM:, :] = jnp.where(pad_row == 0, 1.0, 0.0).astype(BF16)


def _qkprep(u3, cos, sin, gq, gk, tk):
    B, T, _ = u3.shape
    qw = N_Q_HEADS * HEAD_DIM
    kvw = 2 * N_KV_HEADS * HEAD_DIM
    vec = pl.BlockSpec((1, HEAD_DIM), lambda b, i: (0, 0))
    tab = pl.BlockSpec((tk, HEAD_DIM), lambda b, i: (i, 0))
    return pl.pallas_call(
        _qkprep_kernel,
        grid=(B, T // tk),
        in_specs=[pl.BlockSpec((None, tk, qw), lambda b, i: (b, i, B_Q // qw)),
                  pl.BlockSpec((None, tk, kvw), lambda b, i: (b, i, B_KV // kvw)),
                  tab, tab, vec, vec],
        out_specs=[pl.BlockSpec((None, N_Q_HEADS, HEAD_DIM, tk), lambda b, i: (b, 0, 0, i)),
                   pl.BlockSpec((None, N_KV_HEADS, tk, HEAD_DIM), lambda b, i: (b, 0, i, 0)),
                   pl.BlockSpec((None, N_KV_HEADS, 1, HEAD_DIM + V_PAD_ROWS, tk), lambda b, i: (b, 0, i, 0, 0))],
        out_shape=[jax.ShapeDtypeStruct((B, N_Q_HEADS, HEAD_DIM, T), BF16),
                   jax.ShapeDtypeStruct((B, N_KV_HEADS, T, HEAD_DIM), BF16),
                   jax.ShapeDtypeStruct((B, N_KV_HEADS, T // tk, HEAD_DIM + V_PAD_ROWS, tk), BF16)],
        compiler_params=_cparams(("parallel", "parallel")),
        name="qkprep",
    )(u3, u3, cos, sin, gq, gk)


def _col_reduce(op, x):
    rows, cols = x.shape
    if rows % 64 == 0:
        x = op(x.reshape(8, rows // 8, cols), axis=0)
    return op(x, axis=0, keepdims=True)


def _attn_kernel(q_ref, qn_ref, k_ref, vT_ref, o_ref, acc_ref, s_ref, tmax_ref):
    nk, _, tk = vT_ref.shape
    tq = q_ref.shape[2]
    unroll = next(u for u in (8, 4, 2) if nk % u == 0)
    n_iter = nk // unroll
    heads = range(Q_PER_KV)

    def scores(j, slot, h, q):
        k = k_ref[pl.ds(pl.multiple_of(j * tk, tk), tk), :]
        sT = jnp.dot(k, q, preferred_element_type=F32)
        s_ref[slot, h] = sT
        return _col_reduce(jnp.max, sT)

    @pl.when(pl.program_id(2) == 0)
    def _():
        for h in heads:
            tmax_ref[h] = scores(0, 0, h, q_ref[h])

    acc_ref[...] = jnp.zeros_like(acc_ref)

    def consume(j, slot, h, tile_max, m):
        m_new = jnp.maximum(m, tile_max)
        alpha = jnp.exp2(m - m_new)
        p = jnp.exp2(s_ref[slot, h] - m_new).astype(BF16)
        pv = jnp.dot(vT_ref[j], p, preferred_element_type=F32)
        acc_ref[h] = alpha * acc_ref[h] + pv
        return m_new

    def body(i, carry):
        tile_max, ms = carry
        j = unroll * i
        wrap = i == n_iter - 1
        for u in range(unroll):
            nxt = j + u + 1
            last_u = u == unroll - 1
            if last_u:
                nxt = jnp.where(wrap, 0, nxt)
            next_max = []
            new_ms = []
            for h in heads:
                q = jnp.where(wrap, qn_ref[h], q_ref[h]) if last_u else q_ref[h]
                next_max.append(scores(nxt, (u + 1) % 2, h, q))
                new_ms.append(consume(j + u, u % 2, h, tile_max[h], ms[h]))
            tile_max, ms = tuple(next_max), tuple(new_ms)
        return tile_max, ms

    m0 = tuple(jnp.full((1, tq), -jnp.inf, F32) for _ in heads)
    tile_max, _ = lax.fori_loop(0, n_iter, body, (tuple(tmax_ref[h] for h in heads), m0))
    for h in heads:
        tmax_ref[h] = tile_max[h]
        a = acc_ref[h]
        o = a[:HEAD_DIM] * (1.0 / a[HEAD_DIM:HEAD_DIM + 1])
        o_ref[:, h * HEAD_DIM:(h + 1) * HEAD_DIM] = o.T.astype(BF16)


def _attention(q, k, vT):
    B, _, _, T = q.shape
    nk, hv, tk = vT.shape[2], vT.shape[3], vT.shape[4]
    tq = _tile(T, 256, 128)
    gw = Q_PER_KV * HEAD_DIM
    return pl.pallas_call(
        _attn_kernel,
        grid=(B, N_KV_HEADS, T // tq),
        in_specs=[pl.BlockSpec((None, Q_PER_KV, HEAD_DIM, tq), lambda b, g, i: (b, g, 0, i)),
                  pl.BlockSpec((None, Q_PER_KV, HEAD_DIM, tq),
                               lambda b, g, i: (b, g, 0, jnp.minimum(i + 1, T // tq - 1))),
                  pl.BlockSpec((None, None, T, HEAD_DIM), lambda b, g, i: (b, g, 0, 0)),
                  pl.BlockSpec((None, None, nk, hv, tk), lambda b, g, i: (b, g, 0, 0, 0))],
        out_specs=pl.BlockSpec((None, tq, gw), lambda b, g, i: (b, i, g)),
        out_shape=jax.ShapeDtypeStruct((B, T, N_Q_HEADS * HEAD_DIM), BF16),
        scratch_shapes=[pltpu.VMEM((Q_PER_KV, hv, tq), F32),
                        pltpu.VMEM((2, Q_PER_KV, tk, tq), F32),
                        pltpu.VMEM((Q_PER_KV, 1, tq), F32)],
        compiler_params=_cparams(("arbitrary", "arbitrary", "arbitrary")),
        name="attention",
    )(q, q, k, vT)


def _log_sigmoid(z):
    return jnp.minimum(z, 0.0) - jnp.log(1.0 + jnp.exp(-jnp.abs(z)))


def _gla_kernel(qf_ref, kf_ref, vf_ref, lrf_ref, qb_ref, kb_ref, vb_ref, lrb_ref,
                w2_ref, gb_ref, of_ref, ob_ref, s_ref):
    i = pl.program_id(1)
    tb = qf_ref.shape[0]
    L = GLA_CHUNK
    nchunk = tb // L

    @pl.when(i == 0)
    def _():
        s_ref[...] = jnp.zeros_like(s_ref)

    row = lax.broadcasted_iota(jnp.int32, (L, L), 0)
    col = lax.broadcasted_iota(jnp.int32, (L, L), 1)
    masks = (row >= col, col >= row)
    ones = tuple(jnp.where(m, 1.0, 0.0).astype(BF16) for m in masks)
    qscale = GLA_DK ** -0.5
    dirs = ((qf_ref, kf_ref, vf_ref, lrf_ref, of_ref), (qb_ref, kb_ref, vb_ref, lrb_ref, ob_ref))

    log_gate = []
    for d in range(2):
        z = jnp.dot(dirs[d][3][...], w2_ref[d], preferred_element_type=F32) + gb_ref[d]
        log_gate.append(_log_sigmoid(z) * (1.0 / GLA_GATE_NORM))

    cum = {}
    for step in range(nchunk):
        for d in range(2):
            c = step if d == 0 else nchunk - 1 - step
            g = log_gate[d][c * L:(c + 1) * L, :]
            g_hi = g.astype(BF16)
            g_lo = (g - g_hi.astype(F32)).astype(BF16)
            cum[step, d] = (jnp.dot(ones[d], g_hi, preferred_element_type=F32)
                            + jnp.dot(ones[d], g_lo, preferred_element_type=F32))
    scaled = {}
    for step in range(nchunk):
        for d in range(2):
            q_ref, k_ref = dirs[d][0], dirs[d][1]
            c = step if d == 0 else nchunk - 1 - step
            rows = slice(c * L, (c + 1) * L)
            for h in range(GLA_HEADS):
                ks = slice(h * GLA_DK, (h + 1) * GLA_DK)
                b = cum[step, d][:, ks]
                b_tot = b[L - 1:L, :] if d == 0 else b[0:1, :]
                q = q_ref[rows, ks].astype(F32) * qscale
                k = k_ref[rows, ks].astype(F32)
                q_t = (q * jnp.exp(b)).astype(BF16)
                k_t = (k * jnp.exp(-b)).astype(BF16)
                k_d = (k * jnp.exp(b_tot - b)).astype(BF16)
                scaled[step, d, h] = (q_t, k_t, k_d, jnp.exp(b_tot))
    pre = {}
    for step in range(nchunk):
        for d in range(2):
            for h in range(GLA_HEADS):
                q_t, k_t, k_d, e_tot = scaled[step, d, h]
                att = lax.dot_general(q_t, k_t, (((1,), (1,)), ((), ())), preferred_element_type=F32)
                att = jnp.where(masks[d], att, 0.0).astype(BF16)
                pre[step, d, h] = (q_t, k_d, att, e_tot)

    for step in range(nchunk):
        for d in range(2):
            v_ref, o_ref = dirs[d][2], dirs[d][4]
            c = step if d == 0 else nchunk - 1 - step
            rows = slice(c * L, (c + 1) * L)
            for h in range(GLA_HEADS):
                vs = slice(h * GLA_DV, (h + 1) * GLA_DV)
                q_t, k_d, att, e_tot = pre[step, d, h]
                v = v_ref[rows, vs]
                sT = s_ref[d, h]
                o = (jnp.dot(att, v, preferred_element_type=F32)
                     + lax.dot_general(q_t, sT.astype(BF16), (((1,), (1,)), ((), ())),
                                       preferred_element_type=F32))
                o_ref[rows, vs] = o.astype(o_ref.dtype)
                dsT = lax.dot_general(v, k_d, (((0,), (0,)), ((), ())), preferred_element_type=F32)
                s_ref[d, h] = sT * e_tot + dsT


def _gla(u3, w2p, gb):
    B, T, _ = u3.shape
    tb = _tile(T, 512, GLA_CHUNK)
    nb = T // tb
    qk_w = GLA_HEADS * GLA_DK
    v_w = GLA_HEADS * GLA_DV

    def specs(idx):
        return [pl.BlockSpec((None, tb, qk_w), lambda b, i: (b, idx(i), C_Q // qk_w)),
                pl.BlockSpec((None, tb, qk_w), lambda b, i: (b, idx(i), C_K // qk_w)),
                pl.BlockSpec((None, tb, v_w), lambda b, i: (b, idx(i), C_V // v_w)),
                pl.BlockSpec((None, tb, 128), lambda b, i: (b, idx(i), C_LR // 128))]

    fwd = lambda i: i
    bwd = lambda i: nb - 1 - i
    out_sd = jax.ShapeDtypeStruct((B, T, v_w), BF16)
    return pl.pallas_call(
        _gla_kernel,
        grid=(B, nb),
        in_specs=specs(fwd) + specs(bwd) + [
            pl.BlockSpec((2, 128, qk_w), lambda b, i: (0, 0, 0)),
            pl.BlockSpec((2, 1, qk_w), lambda b, i: (0, 0, 0))],
        out_specs=[pl.BlockSpec((None, tb, v_w), lambda b, i: (b, i, 0)),
                   pl.BlockSpec((None, tb, v_w), lambda b, i: (b, nb - 1 - i, 0))],
        out_shape=[out_sd, out_sd],
        scratch_shapes=[pltpu.VMEM((2, GLA_HEADS, GLA_DV, GLA_DK), F32)],
        compiler_params=_cparams(("parallel", "arbitrary")),
        name="gla",
    )(u3, u3, u3, u3, u3, u3, u3, u3, w2p, gb)


def _merge_kernel(tiles_a, oa_ref, ob_ref, bz_ref, cf_ref, cb_ref, cz_ref, m0_ref, m1_ref, m2_ref,
                  *rest):
    *h_refs, wb_ref, wo_ref, gn_ref, pg_ref, o_ref, mg_even, mg_odd = rest
    s = pl.program_id(0)

    @pl.when(s == 0)
    def _():
        mg_odd[...] = jnp.zeros_like(mg_odd)

    def step(mg_read, mg_write):
        out = jnp.dot(mg_read[...], wo_ref[...], preferred_element_type=F32)
        o_a = oa_ref[...]
        o_b = (ob_ref[...].astype(F32) * _silu(bz_ref[...].astype(F32))).astype(BF16)
        parts = []
        for h in range(GLA_HEADS):
            vs = slice(h * GLA_DV, (h + 1) * GLA_DV)
            x = cf_ref[:, vs].astype(F32) + cb_ref[:, vs].astype(F32)
            ms = jnp.mean(x * x, axis=-1, keepdims=True)
            y = x * lax.rsqrt(ms + EPS) * gn_ref[...]
            parts.append((y * _silu(cz_ref[:, vs].astype(F32))).astype(BF16))
        o_c = jnp.concatenate(parts, axis=-1)
        merged = None
        for n, (o, m_ref) in enumerate(((o_a, m0_ref), (o_b, m1_ref), (o_c, m2_ref))):
            term = _sigmoid(m_ref[...].astype(F32)) * jnp.dot(o, wb_ref[n], preferred_element_type=F32)
            merged = term if merged is None else merged + term
        ms = jnp.mean(out * out, axis=-1, keepdims=True)
        h = _pick(tiles_a + 1, *h_refs) if tiles_a else h_refs[0][...]
        o_ref[...] = h + out * lax.rsqrt(ms + EPS) * pg_ref[...]
        mg_write[...] = merged.astype(BF16)

    @pl.when(s % 2 == 0)
    def _():
        step(mg_odd, mg_even)

    @pl.when(s % 2 == 1)
    def _():
        step(mg_even, mg_odd)


def _merge(o_a, o_b, o_cf, o_cb, u, h_srcs, tm, wb, wo, gla_g, post_g):
    n = u.shape[0]
    n_tiles = n // tm
    tiles_a = h_srcs[0].shape[0] // tm if len(h_srcs) == 2 else None
    W = BRANCH_W

    def cur(width, col=0):
        return pl.BlockSpec((tm, width), lambda s: (jnp.minimum(s, n_tiles - 1), col))

    def prev(width):
        return pl.BlockSpec((tm, width), lambda s: (jnp.maximum(s - 1, 0), 0))

    def const(shape):
        return pl.BlockSpec(shape, lambda s: (0,) * len(shape), pipeline_mode=pl.Buffered(1))

    if tiles_a:
        h_specs = (pl.BlockSpec((tm, D_MODEL), lambda s: (jnp.clip(s - 1, 0, tiles_a - 1), 0)),
                   pl.BlockSpec((tm, D_MODEL), lambda s: (jnp.maximum(s - 1 - tiles_a, 0), 0)))
    else:
        h_specs = (prev(D_MODEL),)
    return pl.pallas_call(
        functools.partial(_merge_kernel, tiles_a),
        grid=(n_tiles + 1,),
        in_specs=[cur(W), cur(W), cur(W, B_Z // W), cur(W), cur(W), cur(W, C_Z // W),
                  cur(D_MODEL, M_LOG // D_MODEL), cur(D_MODEL, M_LOG // D_MODEL + 1),
                  cur(D_MODEL, M_LOG // D_MODEL + 2),
                  *h_specs,
                  const((3, W, D_MODEL)), const((D_MODEL, D_MODEL)),
                  const((1, GLA_DV)), const((1, D_MODEL))],
        out_specs=prev(D_MODEL),
        out_shape=jax.ShapeDtypeStruct((n, D_MODEL), F32),
        scratch_shapes=[pltpu.VMEM((tm, D_MODEL), BF16), pltpu.VMEM((tm, D_MODEL), BF16)],
        compiler_params=_cparams(("arbitrary",)),
        name="merge",
    )(o_a, o_b, u, o_cf, o_cb, u, u, u, u, *h_srcs, wb, wo, gla_g, post_g)


def _ple_kernel(tiles_a, last, h_ref, pa_ref, pb_ref, wp_ref, wg_ref, ng_ref, *out_refs):
    h = h_ref[...]
    p = _pick(tiles_a, pa_ref, pb_ref)
    e = jnp.dot(p.astype(BF16), wp_ref[...], preferred_element_type=F32)
    gate = _sigmoid(jnp.dot(h.astype(BF16), wg_ref[...], preferred_element_type=F32))
    h2 = h + gate * e
    if last:
        ya_ref, yb_ref = out_refs
        i = pl.program_id(0)

        @pl.when(i < tiles_a)
        def _():
            ya_ref[...] = h2

        @pl.when(i >= tiles_a)
        def _():
            yb_ref[...] = h2
    else:
        o_ref, xn_ref = out_refs
        o_ref[...] = h2
        ms = jnp.mean(h2 * h2, axis=-1, keepdims=True)
        xn_ref[...] = (h2 * lax.rsqrt(ms + EPS) * ng_ref[...]).astype(BF16)


def _ple(h, p_a, p_b, rows_a, layer, tm, wp, wg, next_g, last):
    n = h.shape[0]
    tiles_a = rows_a // tm
    tiles_b = (n - rows_a) // tm

    def rows(width):
        return pl.BlockSpec((tm, width), lambda i: (i, 0))

    def const(shape):
        return pl.BlockSpec(shape, lambda i: (0,) * len(shape), pipeline_mode=pl.Buffered(1))

    if last:
        out_specs = list(_split_specs(tm, D_MODEL, tiles_a))
        out_shape = [jax.ShapeDtypeStruct((rows_a, D_MODEL), F32),
                     jax.ShapeDtypeStruct((n - rows_a, D_MODEL), F32)]
    else:
        out_specs = [rows(D_MODEL), rows(D_MODEL)]
        out_shape = [jax.ShapeDtypeStruct((n, D_MODEL), F32), jax.ShapeDtypeStruct((n, D_MODEL), BF16)]
    return pl.pallas_call(
        functools.partial(_ple_kernel, tiles_a, last),
        grid=(n // tm,),
        in_specs=[rows(D_MODEL),
                  *_split_specs(tm, PLE_DIM, tiles_a, layer * tiles_a, layer * tiles_b),
                  const((PLE_DIM, D_MODEL)), const((D_MODEL, D_MODEL)), const((1, D_MODEL))],
        out_specs=out_specs,
        out_shape=out_shape,
        compiler_params=_cparams(("arbitrary",)),
        name="ple",
    )(h, p_a, p_b, wp, wg, next_g)


def _reorder_w_in(w_in):
    sizes = (1024, 1024, 1024, 1024, 256, 256, 1024, 512, 512, 1024, 2 * GLA_RANK, 1024, 3 * D_MODEL)
    names = ("a_val", "a_glu", "a_z", "b_q", "b_k", "b_v", "b_z", "c_q", "c_k", "c_v", "c_lr", "c_z", "m")
    parts, start = {}, 0
    for name, s in zip(names, sizes):
        parts[name] = w_in[..., start:start + s].astype(BF16)
        start += s
    parts["b_q"] = _permute_heads(parts["b_q"], N_Q_HEADS)
    parts["b_k"] = _permute_heads(parts["b_k"], N_KV_HEADS)
    pad = jnp.zeros(w_in.shape[:-1] + (U_COLS - C_LR - 2 * GLA_RANK,), BF16)
    order = ("a_val", "a_glu", "a_z", "b_q", "b_z", "c_v", "c_z", "c_q", "c_k", "m", "b_k", "b_v", "c_lr")
    w = jnp.concatenate([parts[k] for k in order] + [pad], axis=-1)
    assert w.shape[-1] == U_COLS
    return w


def _rope_tables(T):
    n_rows = T // GRID_W
    row_idx = jnp.repeat(jnp.arange(n_rows), GRID_W).astype(F32)
    col_idx = jnp.tile(jnp.arange(GRID_W), n_rows).astype(F32)
    inv = ROPE_THETA ** (-jnp.arange(ROPE_PAIRS, dtype=F32) / ROPE_PAIRS)
    ang_r = row_idx[:, None] * inv
    ang_c = col_idx[:, None] * inv
    cos = jnp.concatenate([jnp.cos(ang_r), jnp.cos(ang_c), jnp.cos(ang_r), jnp.cos(ang_c)], axis=-1)
    sin = jnp.concatenate([-jnp.sin(ang_r), -jnp.sin(ang_c), jnp.sin(ang_r), jnp.sin(ang_c)], axis=-1)
    return cos, sin


_ROPE_PERM_BLOCKS = (0, 2, 1, 3)


def _permute_heads(w, n_heads):
    shaped = w.reshape(w.shape[:-1] + (n_heads, HEAD_DIM))
    blocks = [shaped[..., b * ROPE_PAIRS:(b + 1) * ROPE_PAIRS] for b in _ROPE_PERM_BLOCKS]
    return jnp.concatenate(blocks, axis=-1).reshape(w.shape)


def _gla_gate_weights(gate_w2):
    depth = gate_w2.shape[0]
    w = jnp.zeros((depth, 2, 128, GLA_HEADS * GLA_DK), F32)
    for d in range(2):
        w = w.at[:, d, d * GLA_RANK:(d + 1) * GLA_RANK, :].set(gate_w2[:, d])
    return w.astype(BF16)


def kernel(x_prompt, x_sample, p_prompt, p_sample, pre_norm_g, post_norm_g, w_in, conv_w, conv_b, conv_ln_g, conv_ln_b, q_norm_g, k_norm_g, gla_gate_w2, gla_gate_b, gla_norm_g, w_branch, w_out, w_ple, w_ple_gate):
    depth = w_in.shape[0]
    T = x_prompt.shape[1]
    assert x_sample.shape[1] == T
    B = x_prompt.shape[0] + x_sample.shape[0]
    n = B * T
    x_a = x_prompt.reshape(-1, D_MODEL)
    x_b = x_sample.reshape(-1, D_MODEL)
    rows_a = x_a.shape[0]
    p_a = p_prompt.reshape(-1, PLE_DIM)
    p_b = p_sample.reshape(-1, PLE_DIM)
    tm_merge = _tile(T, 256, 16)
    tm_ple = _tile(T, 512, 16)

    w_in_r = [_reorder_w_in(w_in[l]) for l in range(depth)]
    w_branch_b = [w_branch[l].astype(BF16) for l in range(depth)]
    w_out_b = [w_out[l].astype(BF16) for l in range(depth)]
    w_ple_b = [w_ple[l].astype(BF16) for l in range(depth)]
    w_gate_b = [w_ple_gate[l].astype(BF16) for l in range(depth)]
    w2p = _gla_gate_weights(gla_gate_w2)
    conv_w_p = jnp.pad(conv_w, ((0, 0), (0, 1), (0, 0)))
    cos, sin = _rope_tables(T)
    gq_perm = _permute_heads(q_norm_g, 1)
    gk_perm = _permute_heads(k_norm_g, 1)
    tk = _tile(T // 2, 512, 128)

    xn = _prenorm(x_a, x_b, pre_norm_g[0][None], tm_merge)
    h_srcs = (x_a, x_b)
    for l in range(depth):
        u = _inproj(xn, w_in_r[l])
        u3 = u.reshape(B, T, U_COLS)
        o_a = _conv_branch(u3, conv_w_p[l], conv_b[l][None], conv_ln_g[l][None], conv_ln_b[l][None])
        q, k, vT = _qkprep(u3, cos, sin, gq_perm[l][None], gk_perm[l][None], tk)
        o_b = _attention(q, k, vT)
        o_cf, o_cb = _gla(u3, w2p[l], gla_gate_b[l][:, None, :])
        h = _merge(o_a.reshape(n, BRANCH_W), o_b.reshape(n, BRANCH_W), o_cf.reshape(n, BRANCH_W),
                   o_cb.reshape(n, BRANCH_W), u, h_srcs, tm_merge,
                   w_branch_b[l], w_out_b[l], gla_norm_g[l][None], post_norm_g[l][None])
        last = l == depth - 1
        next_g = pre_norm_g[(l + 1) % depth][None]
        outs = _ple(h, p_a, p_b, rows_a, l, tm_ple, w_ple_b[l], w_gate_b[l], next_g, last)
        if last:
            y_prompt, y_sample = outs
        else:
            h, xn = outs
            h_srcs = (h,)
    return (y_prompt.reshape(x_prompt.shape), y_sample.reshape(x_sample.shape))
```

```python
import functools
import math

import jax
import jax.numpy as jnp
from jax import lax
from jax.experimental import pallas as pl
from jax.experimental.pallas import tpu as pltpu

F32 = jnp.float32
BF16 = jnp.bfloat16

D_MODEL = 2048
GRID_W = 64
PLE_DIM = 256
BRANCH_W = 1024
EPS = 1e-6
CONV_K = 31
CONV_HALO = 16
CONV_ROWS = 128
N_Q_HEADS = 8
N_KV_HEADS = 2
Q_PER_KV = N_Q_HEADS // N_KV_HEADS
HEAD_DIM = 128
V_PAD_ROWS = 16
ROPE_PAIRS = HEAD_DIM // 4
ROPE_THETA = 10000.0
GLA_HEADS = 4
GLA_DK = 128
GLA_DV = 256
GLA_RANK = 16
GLA_GATE_NORM = 16.0
GLA_CHUNK = 64

A_VAL, A_GLU, A_Z = 0, 1024, 2048
B_Q, B_Z = 3072, 4096
C_V, C_Z = 5120, 6144
C_Q, C_K = 7168, 7680
M_LOG = 8192
B_KV = 14336
C_LR = 14848
MXU_COLS = 256
INPROJ_TN = 6 * MXU_COLS
U_COLS = 15360

VMEM_LIMIT = 56 * 1024 * 1024


def _cparams(sem, vmem=VMEM_LIMIT):
    return pltpu.CompilerParams(dimension_semantics=sem, vmem_limit_bytes=vmem)


def _tile(n, cap, mult=8):
    t = min(cap, n)
    while t > mult and (n % t or t % mult):
        t -= mult
    assert n % t == 0, (n, cap, mult)
    return t


def _sigmoid(x):
    return 1.0 / (1.0 + jnp.exp(-x))


def _silu(x):
    return x * _sigmoid(x)


def _split_specs(tm, width, tiles_a, a_first=0, b_first=0, col=0):
    a = pl.BlockSpec((tm, width), lambda i: (a_first + jnp.minimum(i, tiles_a - 1), col))
    b = pl.BlockSpec((tm, width), lambda i: (b_first + jnp.maximum(i - tiles_a, 0), col))
    return a, b


def _pick(tiles_a, a_ref, b_ref):
    return jnp.where(pl.program_id(0) < tiles_a, a_ref[...], b_ref[...])


def _prenorm_kernel(tiles_a, xa_ref, xb_ref, g_ref, o_ref):
    x = _pick(tiles_a, xa_ref, xb_ref)
    ms = jnp.mean(x * x, axis=-1, keepdims=True)
    o_ref[...] = (x * lax.rsqrt(ms + EPS) * g_ref[...]).astype(BF16)


def _prenorm(xa, xb, g, tm):
    n = xa.shape[0] + xb.shape[0]
    tiles_a = xa.shape[0] // tm
    return pl.pallas_call(
        functools.partial(_prenorm_kernel, tiles_a),
        grid=(n // tm,),
        in_specs=[*_split_specs(tm, D_MODEL, tiles_a),
                  pl.BlockSpec((1, D_MODEL), lambda i: (0, 0))],
        out_specs=pl.BlockSpec((tm, D_MODEL), lambda i: (i, 0)),
        out_shape=jax.ShapeDtypeStruct((n, D_MODEL), BF16),
        compiler_params=_cparams(("arbitrary",)),
        name="prenorm",
    )(xa, xb, g)


def _inproj_kernel(x_ref, w_ref, u_ref):
    u_ref[...] = jnp.dot(x_ref[...], w_ref[...], preferred_element_type=F32).astype(BF16)


def _inproj(xn, w):
    n = xn.shape[0]
    tm = _tile(n, 1024, 16)
    tn = INPROJ_TN
    return pl.pallas_call(
        _inproj_kernel,
        grid=(n // tm, U_COLS // tn),
        in_specs=[pl.BlockSpec((tm, D_MODEL), lambda i, j: (i, 0)),
                  pl.BlockSpec((D_MODEL, tn), lambda i, j: (0, j))],
        out_specs=pl.BlockSpec((tm, tn), lambda i, j: (i, j)),
        out_shape=jax.ShapeDtypeStruct((n, U_COLS), BF16),
        compiler_params=_cparams(("parallel", "arbitrary")),
        name="inproj",
    )(xn, w)


def _conv_kernel(val_ref, glu_ref, z_ref, pval_ref, pglu_ref, nval_ref, nglu_ref,
                 w_ref, cb_ref, lg_ref, lb_ref, o_ref, abuf, cbuf):
    i = pl.program_id(1)
    n = pl.num_programs(1)
    tt = val_ref.shape[0]
    H = CONV_HALO

    def glu(v_ref, g_ref):
        return v_ref[...].astype(F32) * _sigmoid(g_ref[...].astype(F32))

    abuf[0:H, :] = glu(pval_ref, pglu_ref) * (i > 0).astype(F32)
    abuf[H:H + tt, :] = glu(val_ref, glu_ref)
    abuf[H + tt:2 * H + tt, :] = glu(nval_ref, nglu_ref) * (i < n - 1).astype(F32)

    assert H - CONV_K // 2 == 1 and CONV_K < 32
    rows = CONV_ROWS if tt % CONV_ROWS == 0 else tt
    for r0 in range(0, tt, rows):
        for c in range(BRANCH_W // 128):
            cs = slice(c * 128, (c + 1) * 128)
            acc = None
            for r in range(8):
                part = None
                for q in range(4):
                    j = 8 * q + r
                    if j == 0:
                        continue
                    term = abuf[r0 + 8 * q:r0 + 8 * q + rows + 8, cs] * w_ref[j - 1:j, cs]
                    part = term if part is None else part + term
                shifted = part[r:r + rows, :]
                acc = shifted if acc is None else acc + shifted
            cbuf[r0:r0 + rows, cs] = acc + cb_ref[:, cs]

    x = cbuf[...]
    mu = jnp.mean(x, axis=-1, keepdims=True)
    xc = x - mu
    var = jnp.mean(xc * xc, axis=-1, keepdims=True)
    y = xc * lax.rsqrt(var + EPS) * lg_ref[...] + lb_ref[...]
    o_ref[...] = (_silu(y) * _silu(z_ref[...].astype(F32))).astype(BF16)


def _conv_branch(u3, conv_w, conv_b, ln_g, ln_b):
    B, T, _ = u3.shape
    tt = _tile(T, 2 * CONV_ROWS, 16)
    H = CONV_HALO
    hb = tt // H
    nh = T // H

    def cur(col):
        return pl.BlockSpec((None, tt, BRANCH_W), lambda b, i: (b, i, col))

    def prev(col):
        return pl.BlockSpec((None, H, BRANCH_W), lambda b, i: (b, jnp.maximum(i * hb - 1, 0), col))

    def nxt(col):
        return pl.BlockSpec((None, H, BRANCH_W), lambda b, i: (b, jnp.minimum((i + 1) * hb, nh - 1), col))

    vec = pl.BlockSpec((1, BRANCH_W), lambda b, i: (0, 0))
    return pl.pallas_call(
        _conv_kernel,
        grid=(B, T // tt),
        in_specs=[cur(A_VAL // BRANCH_W), cur(A_GLU // BRANCH_W), cur(A_Z // BRANCH_W),
                  prev(A_VAL // BRANCH_W), prev(A_GLU // BRANCH_W),
                  nxt(A_VAL // BRANCH_W), nxt(A_GLU // BRANCH_W),
                  pl.BlockSpec((CONV_K + 1, BRANCH_W), lambda b, i: (0, 0)), vec, vec, vec],
        out_specs=pl.BlockSpec((None, tt, BRANCH_W), lambda b, i: (b, i, 0)),
        out_shape=jax.ShapeDtypeStruct((B, T, BRANCH_W), BF16),
        scratch_shapes=[pltpu.VMEM((tt + 2 * H, BRANCH_W), F32), pltpu.VMEM((tt, BRANCH_W), F32)],
        compiler_params=_cparams(("parallel", "parallel")),
        name="conv",
    )(u3, u3, u3, u3, u3, u3, u3, conv_w, conv_b, ln_g, ln_b)


def _norm_rope(x, g, cos, sin):
    ms = jnp.mean(x * x, axis=-1, keepdims=True)
    y = x * lax.rsqrt(ms + EPS) * g
    return y * cos + pltpu.roll(y, HEAD_DIM // 2, 1) * sin


def _qkprep_kernel(q_ref, kv_ref, cos_ref, sin_ref, gq_ref, gk_ref, qo_ref, k_ref, vT_ref):
    tt = q_ref.shape[0]
    cos = cos_ref[...]
    sin = sin_ref[...]
    qscale = HEAD_DIM ** -0.5 * math.log2(math.e)
    for h in range(N_Q_HEADS):
        x = q_ref[:, h * HEAD_DIM:(h + 1) * HEAD_DIM].astype(F32)
        y = _norm_rope(x, gq_ref[...], cos, sin) * qscale
        qo_ref[h] = y.T.astype(BF16)
    for g in range(N_KV_HEADS):
        x = kv_ref[:, g * HEAD_DIM:(g + 1) * HEAD_DIM].astype(F32)
        k_ref[g] = _norm_rope(x, gk_ref[...], cos, sin).astype(BF16)
        v = kv_ref[:, (N_KV_HEADS + g) * HEAD_DIM:(N_KV_HEADS + g + 1) * HEAD_DIM].astype(F32)
        vT_ref[g, 0, :HEAD_DIM, :] = v.T.astype(BF16)
        pad_row = lax.broadcasted_iota(jnp.int32, (V_PAD_ROWS, tt), 0)
        vT_ref[g, 0, HEAD_DIM:, :] = jnp.where(pad_row == 0, 1.0, 0.0).astype(BF16)


def _qkprep(u3, cos, sin, gq, gk, tk):
    B, T, _ = u3.shape
    qw = N_Q_HEADS * HEAD_DIM
    kvw = 2 * N_KV_HEADS * HEAD_DIM
    vec = pl.BlockSpec((1, HEAD_DIM), lambda b, i: (0, 0))
    tab = pl.BlockSpec((tk, HEAD_DIM), lambda b, i: (i, 0))
    return pl.pallas_call(
        _qkprep_kernel,
        grid=(B, T // tk),
        in_specs=[pl.BlockSpec((None, tk, qw), lambda b, i: (b, i, B_Q // qw)),
                  pl.BlockSpec((None, tk, kvw), lambda b, i: (b, i, B_KV // kvw)),
                  tab, tab, vec, vec],
        out_specs=[pl.BlockSpec((None, N_Q_HEADS, HEAD_DIM, tk), lambda b, i: (b, 0, 0, i)),
                   pl.BlockSpec((None, N_KV_HEADS, tk, HEAD_DIM), lambda b, i: (b, 0, i, 0)),
                   pl.BlockSpec((None, N_KV_HEADS, 1, HEAD_DIM + V_PAD_ROWS, tk), lambda b, i: (b, 0, i, 0, 0))],
        out_shape=[jax.ShapeDtypeStruct((B, N_Q_HEADS, HEAD_DIM, T), BF16),
                   jax.ShapeDtypeStruct((B, N_KV_HEADS, T, HEAD_DIM), BF16),
                   jax.ShapeDtypeStruct((B, N_KV_HEADS, T // tk, HEAD_DIM + V_PAD_ROWS, tk), BF16)],
        compiler_params=_cparams(("parallel", "parallel")),
        name="qkprep",
    )(u3, u3, cos, sin, gq, gk)


def _col_reduce(op, x):
    rows, cols = x.shape
    if rows % 64 == 0:
        x = op(x.reshape(8, rows // 8, cols), axis=0)
    return op(x, axis=0, keepdims=True)


def _attn_kernel(q_ref, k_ref, vT_ref, o_ref, acc_ref, s_ref):
    nk, _, tk = vT_ref.shape
    tq = q_ref.shape[2]
    unroll = next(u for u in (8, 4, 2) if nk % u == 0)
    heads = range(Q_PER_KV)
    acc_ref[...] = jnp.zeros_like(acc_ref)

    def scores(j, slot, h):
        k = k_ref[pl.ds(pl.multiple_of(j * tk, tk), tk), :]
        sT = jnp.dot(k, q_ref[h], preferred_element_type=F32)
        s_ref[slot, h] = sT
        return _col_reduce(jnp.max, sT)

    def consume(j, slot, h, tile_max, m):
        m_new = jnp.maximum(m, tile_max)
        alpha = jnp.exp2(m - m_new)
        p = jnp.exp2(s_ref[slot, h] - m_new).astype(BF16)
        pv = jnp.dot(vT_ref[j], p, preferred_element_type=F32)
        acc_ref[h] = alpha * acc_ref[h] + pv
        return m_new

    def body(i, carry):
        tile_max, ms = carry
        j = unroll * i
        for u in range(unroll):
            nxt = j + u + 1
            if u == unroll - 1:
                nxt = jnp.minimum(nxt, nk - 1)
            next_max = []
            new_ms = []
            for h in heads:
                next_max.append(scores(nxt, (u + 1) % 2, h))
                new_ms.append(consume(j + u, u % 2, h, tile_max[h], ms[h]))
            tile_max, ms = tuple(next_max), tuple(new_ms)
        return tile_max, ms

    m0 = tuple(jnp.full((1, tq), -jnp.inf, F32) for _ in heads)
    lax.fori_loop(0, nk // unroll, body, (tuple(scores(0, 0, h) for h in heads), m0))
    for h in range(Q_PER_KV):
        a = acc_ref[h]
        o = a[:HEAD_DIM] * (1.0 / a[HEAD_DIM:HEAD_DIM + 1])
        o_ref[:, h * HEAD_DIM:(h + 1) * HEAD_DIM] = o.T.astype(BF16)


def _attention(q, k, vT):
    B, _, _, T = q.shape
    nk, hv, tk = vT.shape[2], vT.shape[3], vT.shape[4]
    tq = _tile(T, 256, 128)
    gw = Q_PER_KV * HEAD_DIM
    return pl.pallas_call(
        _attn_kernel,
        grid=(B, N_KV_HEADS, T // tq),
        in_specs=[pl.BlockSpec((None, Q_PER_KV, HEAD_DIM, tq), lambda b, g, i: (b, g, 0, i)),
                  pl.BlockSpec((None, None, T, HEAD_DIM), lambda b, g, i: (b, g, 0, 0)),
                  pl.BlockSpec((None, None, nk, hv, tk), lambda b, g, i: (b, g, 0, 0, 0))],
        out_specs=pl.BlockSpec((None, tq, gw), lambda b, g, i: (b, i, g)),
        out_shape=jax.ShapeDtypeStruct((B, T, N_Q_HEADS * HEAD_DIM), BF16),
        scratch_shapes=[pltpu.VMEM((Q_PER_KV, hv, tq), F32),
                        pltpu.VMEM((2, Q_PER_KV, tk, tq), F32)],
        compiler_params=_cparams(("parallel", "parallel", "arbitrary")),
        name="attention",
    )(q, k, vT)


def _log_sigmoid(z):
    return jnp.minimum(z, 0.0) - jnp.log(1.0 + jnp.exp(-jnp.abs(z)))


def _gla_kernel(qf_ref, kf_ref, vf_ref, lrf_ref, qb_ref, kb_ref, vb_ref, lrb_ref,
                w2_ref, gb_ref, of_ref, ob_ref, s_ref):
    i = pl.program_id(1)
    tb = qf_ref.shape[0]
    L = GLA_CHUNK
    nchunk = tb // L

    @pl.when(i == 0)
    def _():
        s_ref[...] = jnp.zeros_like(s_ref)

    row = lax.broadcasted_iota(jnp.int32, (L, L), 0)
    col = lax.broadcasted_iota(jnp.int32, (L, L), 1)
    masks = (row >= col, col >= row)
    ones = tuple(jnp.where(m, 1.0, 0.0).astype(BF16) for m in masks)
    qscale = GLA_DK ** -0.5
    dirs = ((qf_ref, kf_ref, vf_ref, lrf_ref, of_ref), (qb_ref, kb_ref, vb_ref, lrb_ref, ob_ref))

    log_gate = []
    for d in range(2):
        z = jnp.dot(dirs[d][3][...], w2_ref[d], preferred_element_type=F32) + gb_ref[d]
        log_gate.append(_log_sigmoid(z) * (1.0 / GLA_GATE_NORM))

    cum = {}
    for step in range(nchunk):
        for d in range(2):
            c = step if d == 0 else nchunk - 1 - step
            g = log_gate[d][c * L:(c + 1) * L, :]
            g_hi = g.astype(BF16)
            g_lo = (g - g_hi.astype(F32)).astype(BF16)
            cum[step, d] = (jnp.dot(ones[d], g_hi, preferred_element_type=F32)
                            + jnp.dot(ones[d], g_lo, preferred_element_type=F32))
    scaled = {}
    for step in range(nchunk):
        for d in range(2):
            q_ref, k_ref = dirs[d][0], dirs[d][1]
            c = step if d == 0 else nchunk - 1 - step
            rows = slice(c * L, (c + 1) * L)
            for h in range(GLA_HEADS):
                ks = slice(h * GLA_DK, (h + 1) * GLA_DK)
                b = cum[step, d][:, ks]
                b_tot = b[L - 1:L, :] if d == 0 else b[0:1, :]
                q = q_ref[rows, ks].astype(F32) * qscale
                k = k_ref[rows, ks].astype(F32)
                q_t = (q * jnp.exp(b)).astype(BF16)
                k_t = (k * jnp.exp(-b)).astype(BF16)
                k_d = (k * jnp.exp(b_tot - b)).astype(BF16)
                scaled[step, d, h] = (q_t, k_t, k_d, jnp.exp(b_tot))
    pre = {}
    for step in range(nchunk):
        for d in range(2):
            for h in range(GLA_HEADS):
                q_t, k_t, k_d, e_tot = scaled[step, d, h]
                att = lax.dot_general(q_t, k_t, (((1,), (1,)), ((), ())), preferred_element_type=F32)
                att = jnp.where(masks[d], att, 0.0).astype(BF16)
                pre[step, d, h] = (q_t, k_d, att, e_tot)

    for step in range(nchunk):
        for d in range(2):
            v_ref, o_ref = dirs[d][2], dirs[d][4]
            c = step if d == 0 else nchunk - 1 - step
            rows = slice(c * L, (c + 1) * L)
            for h in range(GLA_HEADS):
                vs = slice(h * GLA_DV, (h + 1) * GLA_DV)
                q_t, k_d, att, e_tot = pre[step, d, h]
                v = v_ref[rows, vs]
                sT = s_ref[d, h]
                o = (jnp.dot(att, v, preferred_element_type=F32)
                     + lax.dot_general(q_t, sT.astype(BF16), (((1,), (1,)), ((), ())),
                                       preferred_element_type=F32))
                o_ref[rows, vs] = o.astype(o_ref.dtype)
                dsT = lax.dot_general(v, k_d, (((0,), (0,)), ((), ())), preferred_element_type=F32)
                s_ref[d, h] = sT * e_tot + dsT


def _gla(u3, w2p, gb):
    B, T, _ = u3.shape
    tb = _tile(T, 512, GLA_CHUNK)
    nb = T // tb
    qk_w = GLA_HEADS * GLA_DK
    v_w = GLA_HEADS * GLA_DV

    def specs(idx):
        return [pl.BlockSpec((None, tb, qk_w), lambda b, i: (b, idx(i), C_Q // qk_w)),
                pl.BlockSpec((None, tb, qk_w), lambda b, i: (b, idx(i), C_K // qk_w)),
                pl.BlockSpec((None, tb, v_w), lambda b, i: (b, idx(i), C_V // v_w)),
                pl.BlockSpec((None, tb, 128), lambda b, i: (b, idx(i), C_LR // 128))]

    fwd = lambda i: i
    bwd = lambda i: nb - 1 - i
    out_sd = jax.ShapeDtypeStruct((B, T, v_w), BF16)
    return pl.pallas_call(
        _gla_kernel,
        grid=(B, nb),
        in_specs=specs(fwd) + specs(bwd) + [
            pl.BlockSpec((2, 128, qk_w), lambda b, i: (0, 0, 0)),
            pl.BlockSpec((2, 1, qk_w), lambda b, i: (0, 0, 0))],
        out_specs=[pl.BlockSpec((None, tb, v_w), lambda b, i: (b, i, 0)),
                   pl.BlockSpec((None, tb, v_w), lambda b, i: (b, nb - 1 - i, 0))],
        out_shape=[out_sd, out_sd],
        scratch_shapes=[pltpu.VMEM((2, GLA_HEADS, GLA_DV, GLA_DK), F32)],
        compiler_params=_cparams(("parallel", "arbitrary")),
        name="gla",
    )(u3, u3, u3, u3, u3, u3, u3, u3, w2p, gb)


def _merge_kernel(tiles_a, oa_ref, ob_ref, bz_ref, cf_ref, cb_ref, cz_ref, m0_ref, m1_ref, m2_ref,
                  *rest):
    *h_refs, wb_ref, wo_ref, gn_ref, pg_ref, o_ref, mg_even, mg_odd = rest
    s = pl.program_id(0)

    @pl.when(s == 0)
    def _():
        mg_odd[...] = jnp.zeros_like(mg_odd)

    def step(mg_read, mg_write):
        out = jnp.dot(mg_read[...], wo_ref[...], preferred_element_type=F32)
        o_a = oa_ref[...]
        o_b = (ob_ref[...].astype(F32) * _silu(bz_ref[...].astype(F32))).astype(BF16)
        parts = []
        for h in range(GLA_HEADS):
            vs = slice(h * GLA_DV, (h + 1) * GLA_DV)
            x = cf_ref[:, vs].astype(F32) + cb_ref[:, vs].astype(F32)
            ms = jnp.mean(x * x, axis=-1, keepdims=True)
            y = x * lax.rsqrt(ms + EPS) * gn_ref[...]
            parts.append((y * _silu(cz_ref[:, vs].astype(F32))).astype(BF16))
        o_c = jnp.concatenate(parts, axis=-1)
        merged = None
        for n, (o, m_ref) in enumerate(((o_a, m0_ref), (o_b, m1_ref), (o_c, m2_ref))):
            term = _sigmoid(m_ref[...].astype(F32)) * jnp.dot(o, wb_ref[n], preferred_element_type=F32)
            merged = term if merged is None else merged + term
        ms = jnp.mean(out * out, axis=-1, keepdims=True)
        h = _pick(tiles_a + 1, *h_refs) if tiles_a else h_refs[0][...]
        o_ref[...] = h + out * lax.rsqrt(ms + EPS) * pg_ref[...]
        mg_write[...] = merged.astype(BF16)

    @pl.when(s % 2 == 0)
    def _():
        step(mg_odd, mg_even)

    @pl.when(s % 2 == 1)
    def _():
        step(mg_even, mg_odd)


def _merge(o_a, o_b, o_cf, o_cb, u, h_srcs, tm, wb, wo, gla_g, post_g):
    n = u.shape[0]
    n_tiles = n // tm
    tiles_a = h_srcs[0].shape[0] // tm if len(h_srcs) == 2 else None
    W = BRANCH_W

    def cur(width, col=0):
        return pl.BlockSpec((tm, width), lambda s: (jnp.minimum(s, n_tiles - 1), col))

    def prev(width):
        return pl.BlockSpec((tm, width), lambda s: (jnp.maximum(s - 1, 0), 0))

    def const(shape):
        return pl.BlockSpec(shape, lambda s: (0,) * len(shape), pipeline_mode=pl.Buffered(1))

    if tiles_a:
        h_specs = (pl.BlockSpec((tm, D_MODEL), lambda s: (jnp.clip(s - 1, 0, tiles_a - 1), 0)),
                   pl.BlockSpec((tm, D_MODEL), lambda s: (jnp.maximum(s - 1 - tiles_a, 0), 0)))
    else:
        h_specs = (prev(D_MODEL),)
    return pl.pallas_call(
        functools.partial(_merge_kernel, tiles_a),
        grid=(n_tiles + 1,),
        in_specs=[cur(W), cur(W), cur(W, B_Z // W), cur(W), cur(W), cur(W, C_Z // W),
                  cur(D_MODEL, M_LOG // D_MODEL), cur(D_MODEL, M_LOG // D_MODEL + 1),
                  cur(D_MODEL, M_LOG // D_MODEL + 2),
                  *h_specs,
                  const((3, W, D_MODEL)), const((D_MODEL, D_MODEL)),
                  const((1, GLA_DV)), const((1, D_MODEL))],
        out_specs=prev(D_MODEL),
        out_shape=jax.ShapeDtypeStruct((n, D_MODEL), F32),
        scratch_shapes=[pltpu.VMEM((tm, D_MODEL), BF16), pltpu.VMEM((tm, D_MODEL), BF16)],
        compiler_params=_cparams(("arbitrary",)),
        name="merge",
    )(o_a, o_b, u, o_cf, o_cb, u, u, u, u, *h_srcs, wb, wo, gla_g, post_g)


def _ple_kernel(tiles_a, last, h_ref, pa_ref, pb_ref, wp_ref, wg_ref, ng_ref, *out_refs):
    h = h_ref[...]
    p = _pick(tiles_a, pa_ref, pb_ref)
    e = jnp.dot(p.astype(BF16), wp_ref[...], preferred_element_type=F32)
    gate = _sigmoid(jnp.dot(h.astype(BF16), wg_ref[...], preferred_element_type=F32))
    h2 = h + gate * e
    if last:
        ya_ref, yb_ref = out_refs
        i = pl.program_id(0)

        @pl.when(i < tiles_a)
        def _():
            ya_ref[...] = h2

        @pl.when(i >= tiles_a)
        def _():
            yb_ref[...] = h2
    else:
        o_ref, xn_ref = out_refs
        o_ref[...] = h2
        ms = jnp.mean(h2 * h2, axis=-1, keepdims=True)
        xn_ref[...] = (h2 * lax.rsqrt(ms + EPS) * ng_ref[...]).astype(BF16)


def _ple(h, p_a, p_b, rows_a, layer, tm, wp, wg, next_g, last):
    n = h.shape[0]
    tiles_a = rows_a // tm
    tiles_b = (n - rows_a) // tm

    def rows(width):
        return pl.BlockSpec((tm, width), lambda i: (i, 0))

    def const(shape):
        return pl.BlockSpec(shape, lambda i: (0,) * len(shape), pipeline_mode=pl.Buffered(1))

    if last:
        out_specs = list(_split_specs(tm, D_MODEL, tiles_a))
        out_shape = [jax.ShapeDtypeStruct((rows_a, D_MODEL), F32),
                     jax.ShapeDtypeStruct((n - rows_a, D_MODEL), F32)]
    else:
        out_specs = [rows(D_MODEL), rows(D_MODEL)]
        out_shape = [jax.ShapeDtypeStruct((n, D_MODEL), F32), jax.ShapeDtypeStruct((n, D_MODEL), BF16)]
    return pl.pallas_call(
        functools.partial(_ple_kernel, tiles_a, last),
        grid=(n // tm,),
        in_specs=[rows(D_MODEL),
                  *_split_specs(tm, PLE_DIM, tiles_a, layer * tiles_a, layer * tiles_b),
                  const((PLE_DIM, D_MODEL)), const((D_MODEL, D_MODEL)), const((1, D_MODEL))],
        out_specs=out_specs,
        out_shape=out_shape,
        compiler_params=_cparams(("arbitrary",)),
        name="ple",
    )(h, p_a, p_b, wp, wg, next_g)


def _reorder_w_in(w_in):
    sizes = (1024, 1024, 1024, 1024, 256, 256, 1024, 512, 512, 1024, 2 * GLA_RANK, 1024, 3 * D_MODEL)
    names = ("a_val", "a_glu", "a_z", "b_q", "b_k", "b_v", "b_z", "c_q", "c_k", "c_v", "c_lr", "c_z", "m")
    parts, start = {}, 0
    for name, s in zip(names, sizes):
        parts[name] = w_in[..., start:start + s].astype(BF16)
        start += s
    parts["b_q"] = _permute_heads(parts["b_q"], N_Q_HEADS)
    parts["b_k"] = _permute_heads(parts["b_k"], N_KV_HEADS)
    pad = jnp.zeros(w_in.shape[:-1] + (U_COLS - C_LR - 2 * GLA_RANK,), BF16)
    order = ("a_val", "a_glu", "a_z", "b_q", "b_z", "c_v", "c_z", "c_q", "c_k", "m", "b_k", "b_v", "c_lr")
    w = jnp.concatenate([parts[k] for k in order] + [pad], axis=-1)
    assert w.shape[-1] == U_COLS
    return w


def _rope_tables(T):
    n_rows = T // GRID_W
    row_idx = jnp.repeat(jnp.arange(n_rows), GRID_W).astype(F32)
    col_idx = jnp.tile(jnp.arange(GRID_W), n_rows).astype(F32)
    inv = ROPE_THETA ** (-jnp.arange(ROPE_PAIRS, dtype=F32) / ROPE_PAIRS)
    ang_r = row_idx[:, None] * inv
    ang_c = col_idx[:, None] * inv
    cos = jnp.concatenate([jnp.cos(ang_r), jnp.cos(ang_c), jnp.cos(ang_r), jnp.cos(ang_c)], axis=-1)
    sin = jnp.concatenate([-jnp.sin(ang_r), -jnp.sin(ang_c), jnp.sin(ang_r), jnp.sin(ang_c)], axis=-1)
    return cos, sin


_ROPE_PERM_BLOCKS = (0, 2, 1, 3)


def _permute_heads(w, n_heads):
    shaped = w.reshape(w.shape[:-1] + (n_heads, HEAD_DIM))
    blocks = [shaped[..., b * ROPE_PAIRS:(b + 1) * ROPE_PAIRS] for b in _ROPE_PERM_BLOCKS]
    return jnp.concatenate(blocks, axis=-1).reshape(w.shape)


def _gla_gate_weights(gate_w2):
    depth = gate_w2.shape[0]
    w = jnp.zeros((depth, 2, 128, GLA_HEADS * GLA_DK), F32)
    for d in range(2):
        w = w.at[:, d, d * GLA_RANK:(d + 1) * GLA_RANK, :].set(gate_w2[:, d])
    return w.astype(BF16)


def kernel(x_prompt, x_sample, p_prompt, p_sample, pre_norm_g, post_norm_g, w_in, conv_w, conv_b, conv_ln_g, conv_ln_b, q_norm_g, k_norm_g, gla_gate_w2, gla_gate_b, gla_norm_g, w_branch, w_out, w_ple, w_ple_gate):
    depth = w_in.shape[0]
    T = x_prompt.shape[1]
    assert x_sample.shape[1] == T
    B = x_prompt.shape[0] + x_sample.shape[0]
    n = B * T
    x_a = x_prompt.reshape(-1, D_MODEL)
    x_b = x_sample.reshape(-1, D_MODEL)
    rows_a = x_a.shape[0]
    p_a = p_prompt.reshape(-1, PLE_DIM)
    p_b = p_sample.reshape(-1, PLE_DIM)
    tm_merge = _tile(T, 256, 16)
    tm_ple = _tile(T, 512, 16)

    w_in_r = [_reorder_w_in(w_in[l]) for l in range(depth)]
    w_branch_b = [w_branch[l].astype(BF16) for l in range(depth)]
    w_out_b = [w_out[l].astype(BF16) for l in range(depth)]
    w_ple_b = [w_ple[l].astype(BF16) for l in range(depth)]
    w_gate_b = [w_ple_gate[l].astype(BF16) for l in range(depth)]
    w2p = _gla_gate_weights(gla_gate_w2)
    conv_w_p = jnp.pad(conv_w, ((0, 0), (0, 1), (0, 0)))
    cos, sin = _rope_tables(T)
    gq_perm = _permute_heads(q_norm_g, 1)
    gk_perm = _permute_heads(k_norm_g, 1)
    tk = _tile(T // 2, 512, 128)

    xn = _prenorm(x_a, x_b, pre_norm_g[0][None], tm_ple)
    h_srcs = (x_a, x_b)
    for l in range(depth):
        u = _inproj(xn, w_in_r[l])
        u3 = u.reshape(B, T, U_COLS)
        o_a = _conv_branch(u3, conv_w_p[l], conv_b[l][None], conv_ln_g[l][None], conv_ln_b[l][None])
        q, k, vT = _qkprep(u3, cos, sin, gq_perm[l][None], gk_perm[l][None], tk)
        o_b = _attention(q, k, vT)
        o_cf, o_cb = _gla(u3, w2p[l], gla_gate_b[l][:, None, :])
        h = _merge(o_a.reshape(n, BRANCH_W), o_b.reshape(n, BRANCH_W), o_cf.reshape(n, BRANCH_W),
                   o_cb.reshape(n, BRANCH_W), u, h_srcs, tm_merge,
                   w_branch_b[l], w_out_b[l], gla_norm_g[l][None], post_norm_g[l][None])
        last = l == depth - 1
        next_g = pre_norm_g[(l + 1) % depth][None]
        outs = _ple(h, p_a, p_b, rows_a, l, tm_ple, w_ple_b[l], w_gate_b[l], next_g, last)
        if last:
            y_prompt, y_sample = outs
        else:
            h, xn = outs
            h_srcs = (h,)
    return (y_prompt.reshape(x_prompt.shape), y_sample.reshape(x_sample.shape))
```

```python
import functools
import math

import jax
import jax.numpy as jnp
from jax import lax
from jax.experimental import pallas as pl
from jax.experimental.pallas import tpu as pltpu

F32 = jnp.float32
BF16 = jnp.bfloat16

D_MODEL = 2048
GRID_W = 64
PLE_DIM = 256
BRANCH_W = 1024
EPS = 1e-6
CONV_K = 31
CONV_HALO = 16
CONV_ROWS = 128
N_Q_HEADS = 8
N_KV_HEADS = 2
Q_PER_KV = N_Q_HEADS // N_KV_HEADS
HEAD_DIM = 128
V_PAD_ROWS = 16
ROPE_PAIRS = HEAD_DIM // 4
ROPE_THETA = 10000.0
GLA_HEADS = 4
GLA_DK = 128
GLA_DV = 256
GLA_RANK = 16
GLA_GATE_NORM = 16.0
GLA_CHUNK = 64

A_VAL, A_GLU, A_Z = 0, 1024, 2048
B_Q, B_Z = 3072, 4096
C_V, C_Z = 5120, 6144
C_Q, C_K = 7168, 7680
M_LOG = 8192
B_KV = 14336
C_LR = 14848
MXU_COLS = 256
INPROJ_TN = 6 * MXU_COLS
U_COLS = 15360

VMEM_LIMIT = 56 * 1024 * 1024


def _cparams(sem, vmem=VMEM_LIMIT):
    return pltpu.CompilerParams(dimension_semantics=sem, vmem_limit_bytes=vmem)


def _tile(n, cap, mult=8):
    t = min(cap, n)
    while t > mult and (n % t or t % mult):
        t -= mult
    assert n % t == 0, (n, cap, mult)
    return t


def _sigmoid(x):
    return 1.0 / (1.0 + jnp.exp(-x))


def _silu(x):
    return x * _sigmoid(x)


def _split_specs(tm, width, tiles_a, a_first=0, b_first=0, col=0):
    a = pl.BlockSpec((tm, width), lambda i: (a_first + jnp.minimum(i, tiles_a - 1), col))
    b = pl.BlockSpec((tm, width), lambda i: (b_first + jnp.maximum(i - tiles_a, 0), col))
    return a, b


def _pick(tiles_a, a_ref, b_ref):
    return jnp.where(pl.program_id(0) < tiles_a, a_ref[...], b_ref[...])


def _prenorm_kernel(tiles_a, xa_ref, xb_ref, g_ref, o_ref):
    x = _pick(tiles_a, xa_ref, xb_ref)
    ms = jnp.mean(x * x, axis=-1, keepdims=True)
    o_ref[...] = (x * lax.rsqrt(ms + EPS) * g_ref[...]).astype(BF16)


def _prenorm(xa, xb, g, tm):
    n = xa.shape[0] + xb.shape[0]
    tiles_a = xa.shape[0] // tm
    return pl.pallas_call(
        functools.partial(_prenorm_kernel, tiles_a),
        grid=(n // tm,),
        in_specs=[*_split_specs(tm, D_MODEL, tiles_a),
                  pl.BlockSpec((1, D_MODEL), lambda i: (0, 0))],
        out_specs=pl.BlockSpec((tm, D_MODEL), lambda i: (i, 0)),
        out_shape=jax.ShapeDtypeStruct((n, D_MODEL), BF16),
        compiler_params=_cparams(("arbitrary",)),
        name="prenorm",
    )(xa, xb, g)


def _inproj_kernel(x_ref, w_ref, u_ref):
    u_ref[...] = jnp.dot(x_ref[...], w_ref[...], preferred_element_type=F32).astype(BF16)


def _inproj(xn, w):
    n = xn.shape[0]
    tm = _tile(n, 1024, 16)
    tn = INPROJ_TN
    return pl.pallas_call(
        _inproj_kernel,
        grid=(n // tm, U_COLS // tn),
        in_specs=[pl.BlockSpec((tm, D_MODEL), lambda i, j: (i, 0)),
                  pl.BlockSpec((D_MODEL, tn), lambda i, j: (0, j))],
        out_specs=pl.BlockSpec((tm, tn), lambda i, j: (i, j)),
        out_shape=jax.ShapeDtypeStruct((n, U_COLS), BF16),
        compiler_params=_cparams(("parallel", "arbitrary")),
        name="inproj",
    )(xn, w)


def _conv_kernel(val_ref, glu_ref, z_ref, pval_ref, pglu_ref, nval_ref, nglu_ref,
                 w_ref, cb_ref, lg_ref, lb_ref, o_ref, abuf, cbuf):
    i = pl.program_id(1)
    n = pl.num_programs(1)
    tt = val_ref.shape[0]
    H = CONV_HALO

    def glu(v_ref, g_ref):
        return v_ref[...].astype(F32) * _sigmoid(g_ref[...].astype(F32))

    abuf[0:H, :] = glu(pval_ref, pglu_ref) * (i > 0).astype(F32)
    abuf[H:H + tt, :] = glu(val_ref, glu_ref)
    abuf[H + tt:2 * H + tt, :] = glu(nval_ref, nglu_ref) * (i < n - 1).astype(F32)

    assert H - CONV_K // 2 == 1 and CONV_K < 32
    rows = CONV_ROWS if tt % CONV_ROWS == 0 else tt
    for r0 in range(0, tt, rows):
        for c in range(BRANCH_W // 128):
            cs = slice(c * 128, (c + 1) * 128)
            acc = None
            for r in range(8):
                part = None
                for q in range(4):
                    j = 8 * q + r
                    if j == 0:
                        continue
                    term = abuf[r0 + 8 * q:r0 + 8 * q + rows + 8, cs] * w_ref[j - 1:j, cs]
                    part = term if part is None else part + term
                shifted = part[r:r + rows, :]
                acc = shifted if acc is None else acc + shifted
            cbuf[r0:r0 + rows, cs] = acc + cb_ref[:, cs]

    x = cbuf[...]
    mu = jnp.mean(x, axis=-1, keepdims=True)
    xc = x - mu
    var = jnp.mean(xc * xc, axis=-1, keepdims=True)
    y = xc * lax.rsqrt(var + EPS) * lg_ref[...] + lb_ref[...]
    o_ref[...] = (_silu(y) * _silu(z_ref[...].astype(F32))).astype(BF16)


def _conv_branch(u3, conv_w, conv_b, ln_g, ln_b):
    B, T, _ = u3.shape
    tt = _tile(T, 2 * CONV_ROWS, 16)
    H = CONV_HALO
    hb = tt // H
    nh = T // H

    def cur(col):
        return pl.BlockSpec((None, tt, BRANCH_W), lambda b, i: (b, i, col))

    def prev(col):
        return pl.BlockSpec((None, H, BRANCH_W), lambda b, i: (b, jnp.maximum(i * hb - 1, 0), col))

    def nxt(col):
        return pl.BlockSpec((None, H, BRANCH_W), lambda b, i: (b, jnp.minimum((i + 1) * hb, nh - 1), col))

    vec = pl.BlockSpec((1, BRANCH_W), lambda b, i: (0, 0))
    return pl.pallas_call(
        _conv_kernel,
        grid=(B, T // tt),
        in_specs=[cur(A_VAL // BRANCH_W), cur(A_GLU // BRANCH_W), cur(A_Z // BRANCH_W),
                  prev(A_VAL // BRANCH_W), prev(A_GLU // BRANCH_W),
                  nxt(A_VAL // BRANCH_W), nxt(A_GLU // BRANCH_W),
                  pl.BlockSpec((CONV_K + 1, BRANCH_W), lambda b, i: (0, 0)), vec, vec, vec],
        out_specs=pl.BlockSpec((None, tt, BRANCH_W), lambda b, i: (b, i, 0)),
        out_shape=jax.ShapeDtypeStruct((B, T, BRANCH_W), BF16),
        scratch_shapes=[pltpu.VMEM((tt + 2 * H, BRANCH_W), F32), pltpu.VMEM((tt, BRANCH_W), F32)],
        compiler_params=_cparams(("parallel", "parallel")),
        name="conv",
    )(u3, u3, u3, u3, u3, u3, u3, conv_w, conv_b, ln_g, ln_b)


def _norm_rope(x, g, cos, sin):
    ms = jnp.mean(x * x, axis=-1, keepdims=True)
    y = x * lax.rsqrt(ms + EPS) * g
    return y * cos + pltpu.roll(y, HEAD_DIM // 2, 1) * sin


def _qkprep_kernel(q_ref, kv_ref, cos_ref, sin_ref, gq_ref, gk_ref, qo_ref, k_ref, vT_ref):
    tt = q_ref.shape[0]
    cos = cos_ref[...]
    sin = sin_ref[...]
    qscale = HEAD_DIM ** -0.5 * math.log2(math.e)
    for h in range(N_Q_HEADS):
        x = q_ref[:, h * HEAD_DIM:(h + 1) * HEAD_DIM].astype(F32)
        y = _norm_rope(x, gq_ref[...], cos, sin) * qscale
        qo_ref[h] = y.T.astype(BF16)
    for g in range(N_KV_HEADS):
        x = kv_ref[:, g * HEAD_DIM:(g + 1) * HEAD_DIM].astype(F32)
        k_ref[g] = _norm_rope(x, gk_ref[...], cos, sin).astype(BF16)
        v = kv_ref[:, (N_KV_HEADS + g) * HEAD_DIM:(N_KV_HEADS + g + 1) * HEAD_DIM].astype(F32)
        vT_ref[g, 0, :HEAD_DIM, :] = v.T.astype(BF16)
        pad_row = lax.broadcasted_iota(jnp.int32, (V_PAD_ROWS, tt), 0)
        vT_ref[g, 0, HEAD_DIM:, :] = jnp.where(pad_row == 0, 1.0, 0.0).astype(BF16)


def _qkprep(u3, cos, sin, gq, gk, tk):
    B, T, _ = u3.shape
    qw = N_Q_HEADS * HEAD_DIM
    kvw = 2 * N_KV_HEADS * HEAD_DIM
    vec = pl.BlockSpec((1, HEAD_DIM), lambda b, i: (0, 0))
    tab = pl.BlockSpec((tk, HEAD_DIM), lambda b, i: (i, 0))
    return pl.pallas_call(
        _qkprep_kernel,
        grid=(B, T // tk),
        in_specs=[pl.BlockSpec((None, tk, qw), lambda b, i: (b, i, B_Q // qw)),
                  pl.BlockSpec((None, tk, kvw), lambda b, i: (b, i, B_KV // kvw)),
                  tab, tab, vec, vec],
        out_specs=[pl.BlockSpec((None, N_Q_HEADS, HEAD_DIM, tk), lambda b, i: (b, 0, 0, i)),
                   pl.BlockSpec((None, N_KV_HEADS, tk, HEAD_DIM), lambda b, i: (b, 0, i, 0)),
                   pl.BlockSpec((None, N_KV_HEADS, 1, HEAD_DIM + V_PAD_ROWS, tk), lambda b, i: (b, 0, i, 0, 0))],
        out_shape=[jax.ShapeDtypeStruct((B, N_Q_HEADS, HEAD_DIM, T), BF16),
                   jax.ShapeDtypeStruct((B, N_KV_HEADS, T, HEAD_DIM), BF16),
                   jax.ShapeDtypeStruct((B, N_KV_HEADS, T // tk, HEAD_DIM + V_PAD_ROWS, tk), BF16)],
        compiler_params=_cparams(("parallel", "parallel")),
        name="qkprep",
    )(u3, u3, cos, sin, gq, gk)


def _col_reduce(op, x):
    rows, cols = x.shape
    if rows % 64 == 0:
        x = op(x.reshape(8, rows // 8, cols), axis=0)
    return op(x, axis=0, keepdims=True)


def _attn_kernel(q_ref, k_ref, vT_ref, o_ref, acc_ref, s_ref):
    nk, _, tk = vT_ref.shape
    tq = q_ref.shape[2]
    unroll = next(u for u in (4, 2) if nk % u == 0)
    heads = range(Q_PER_KV)
    acc_ref[...] = jnp.zeros_like(acc_ref)

    def scores(j, slot, h):
        k = k_ref[pl.ds(pl.multiple_of(j * tk, tk), tk), :]
        sT = jnp.dot(k, q_ref[h], preferred_element_type=F32)
        s_ref[slot, h] = sT
        return _col_reduce(jnp.max, sT)

    def consume(j, slot, h, tile_max, m):
        m_new = jnp.maximum(m, tile_max)
        alpha = jnp.exp2(m - m_new)
        p = jnp.exp2(s_ref[slot, h] - m_new).astype(BF16)
        pv = jnp.dot(vT_ref[j], p, preferred_element_type=F32)
        acc_ref[h] = alpha * acc_ref[h] + pv
        return m_new

    def body(i, carry):
        tile_max, ms = carry
        j = unroll * i
        for u in range(unroll):
            nxt = j + u + 1
            if u == unroll - 1:
                nxt = jnp.minimum(nxt, nk - 1)
            next_max = []
            new_ms = []
            for h in heads:
                next_max.append(scores(nxt, (u + 1) % 2, h))
                new_ms.append(consume(j + u, u % 2, h, tile_max[h], ms[h]))
            tile_max, ms = tuple(next_max), tuple(new_ms)
        return tile_max, ms

    m0 = tuple(jnp.full((1, tq), -jnp.inf, F32) for _ in heads)
    lax.fori_loop(0, nk // unroll, body, (tuple(scores(0, 0, h) for h in heads), m0))
    for h in range(Q_PER_KV):
        a = acc_ref[h]
        o = a[:HEAD_DIM] * (1.0 / a[HEAD_DIM:HEAD_DIM + 1])
        o_ref[:, h * HEAD_DIM:(h + 1) * HEAD_DIM] = o.T.astype(BF16)


def _attention(q, k, vT):
    B, _, _, T = q.shape
    nk, hv, tk = vT.shape[2], vT.shape[3], vT.shape[4]
    tq = _tile(T, 512, 128)
    gw = Q_PER_KV * HEAD_DIM
    return pl.pallas_call(
        _attn_kernel,
        grid=(B, N_KV_HEADS, T // tq),
        in_specs=[pl.BlockSpec((None, Q_PER_KV, HEAD_DIM, tq), lambda b, g, i: (b, g, 0, i)),
                  pl.BlockSpec((None, None, T, HEAD_DIM), lambda b, g, i: (b, g, 0, 0)),
                  pl.BlockSpec((None, None, nk, hv, tk), lambda b, g, i: (b, g, 0, 0, 0))],
        out_specs=pl.BlockSpec((None, tq, gw), lambda b, g, i: (b, i, g)),
        out_shape=jax.ShapeDtypeStruct((B, T, N_Q_HEADS * HEAD_DIM), BF16),
        scratch_shapes=[pltpu.VMEM((Q_PER_KV, hv, tq), F32),
                        pltpu.VMEM((2, Q_PER_KV, tk, tq), F32)],
        compiler_params=_cparams(("parallel", "parallel", "arbitrary")),
        name="attention",
    )(q, k, vT)


def _log_sigmoid(z):
    return jnp.minimum(z, 0.0) - jnp.log(1.0 + jnp.exp(-jnp.abs(z)))


def _gla_kernel(qf_ref, kf_ref, vf_ref, lrf_ref, qb_ref, kb_ref, vb_ref, lrb_ref,
                w2_ref, gb_ref, of_ref, ob_ref, s_ref):
    i = pl.program_id(1)
    tb = qf_ref.shape[0]
    L = GLA_CHUNK
    nchunk = tb // L

    @pl.when(i == 0)
    def _():
        s_ref[...] = jnp.zeros_like(s_ref)

    row = lax.broadcasted_iota(jnp.int32, (L, L), 0)
    col = lax.broadcasted_iota(jnp.int32, (L, L), 1)
    masks = (row >= col, col >= row)
    ones = tuple(jnp.where(m, 1.0, 0.0).astype(BF16) for m in masks)
    qscale = GLA_DK ** -0.5
    dirs = ((qf_ref, kf_ref, vf_ref, lrf_ref, of_ref), (qb_ref, kb_ref, vb_ref, lrb_ref, ob_ref))

    log_gate = []
    for d in range(2):
        z = jnp.dot(dirs[d][3][...], w2_ref[d], preferred_element_type=F32) + gb_ref[d]
        log_gate.append(_log_sigmoid(z) * (1.0 / GLA_GATE_NORM))

    cum = {}
    for step in range(nchunk):
        for d in range(2):
            c = step if d == 0 else nchunk - 1 - step
            g = log_gate[d][c * L:(c + 1) * L, :]
            g_hi = g.astype(BF16)
            g_lo = (g - g_hi.astype(F32)).astype(BF16)
            cum[step, d] = (jnp.dot(ones[d], g_hi, preferred_element_type=F32)
                            + jnp.dot(ones[d], g_lo, preferred_element_type=F32))
    scaled = {}
    for step in range(nchunk):
        for d in range(2):
            q_ref, k_ref = dirs[d][0], dirs[d][1]
            c = step if d == 0 else nchunk - 1 - step
            rows = slice(c * L, (c + 1) * L)
            for h in range(GLA_HEADS):
                ks = slice(h * GLA_DK, (h + 1) * GLA_DK)
                b = cum[step, d][:, ks]
                b_tot = b[L - 1:L, :] if d == 0 else b[0:1, :]
                q = q_ref[rows, ks].astype(F32) * qscale
                k = k_ref[rows, ks].astype(F32)
                q_t = (q * jnp.exp(b)).astype(BF16)
                k_t = (k * jnp.exp(-b)).astype(BF16)
                k_d = (k * jnp.exp(b_tot - b)).astype(BF16)
                scaled[step, d, h] = (q_t, k_t, k_d, jnp.exp(b_tot))
    pre = {}
    for step in range(nchunk):
        for d in range(2):
            for h in range(GLA_HEADS):
                q_t, k_t, k_d, e_tot = scaled[step, d, h]
                att = lax.dot_general(q_t, k_t, (((1,), (1,)), ((), ())), preferred_element_type=F32)
                att = jnp.where(masks[d], att, 0.0).astype(BF16)
                pre[step, d, h] = (q_t, k_d, att, e_tot)

    for step in range(nchunk):
        for d in range(2):
            v_ref, o_ref = dirs[d][2], dirs[d][4]
            c = step if d == 0 else nchunk - 1 - step
            rows = slice(c * L, (c + 1) * L)
            for h in range(GLA_HEADS):
                vs = slice(h * GLA_DV, (h + 1) * GLA_DV)
                q_t, k_d, att, e_tot = pre[step, d, h]
                v = v_ref[rows, vs]
                sT = s_ref[d, h]
                o = (jnp.dot(att, v, preferred_element_type=F32)
                     + lax.dot_general(q_t, sT.astype(BF16), (((1,), (1,)), ((), ())),
                                       preferred_element_type=F32))
                o_ref[rows, vs] = o.astype(o_ref.dtype)
                dsT = lax.dot_general(v, k_d, (((0,), (0,)), ((), ())), preferred_element_type=F32)
                s_ref[d, h] = sT * e_tot + dsT


def _gla(u3, w2p, gb):
    B, T, _ = u3.shape
    tb = _tile(T, 512, GLA_CHUNK)
    nb = T // tb
    qk_w = GLA_HEADS * GLA_DK
    v_w = GLA_HEADS * GLA_DV

    def specs(idx):
        return [pl.BlockSpec((None, tb, qk_w), lambda b, i: (b, idx(i), C_Q // qk_w)),
                pl.BlockSpec((None, tb, qk_w), lambda b, i: (b, idx(i), C_K // qk_w)),
                pl.BlockSpec((None, tb, v_w), lambda b, i: (b, idx(i), C_V // v_w)),
                pl.BlockSpec((None, tb, 128), lambda b, i: (b, idx(i), C_LR // 128))]

    fwd = lambda i: i
    bwd = lambda i: nb - 1 - i
    out_sd = jax.ShapeDtypeStruct((B, T, v_w), BF16)
    return pl.pallas_call(
        _gla_kernel,
        grid=(B, nb),
        in_specs=specs(fwd) + specs(bwd) + [
            pl.BlockSpec((2, 128, qk_w), lambda b, i: (0, 0, 0)),
            pl.BlockSpec((2, 1, qk_w), lambda b, i: (0, 0, 0))],
        out_specs=[pl.BlockSpec((None, tb, v_w), lambda b, i: (b, i, 0)),
                   pl.BlockSpec((None, tb, v_w), lambda b, i: (b, nb - 1 - i, 0))],
        out_shape=[out_sd, out_sd],
        scratch_shapes=[pltpu.VMEM((2, GLA_HEADS, GLA_DV, GLA_DK), F32)],
        compiler_params=_cparams(("parallel", "arbitrary")),
        name="gla",
    )(u3, u3, u3, u3, u3, u3, u3, u3, w2p, gb)


def _merge_kernel(tiles_a, oa_ref, ob_ref, bz_ref, cf_ref, cb_ref, cz_ref, m0_ref, m1_ref, m2_ref,
                  *rest):
    *h_refs, wb_ref, wo_ref, gn_ref, pg_ref, o_ref, mg_even, mg_odd = rest
    s = pl.program_id(0)

    @pl.when(s == 0)
    def _():
        mg_odd[...] = jnp.zeros_like(mg_odd)

    def step(mg_read, mg_write):
        out = jnp.dot(mg_read[...], wo_ref[...], preferred_element_type=F32)
        o_a = oa_ref[...]
        o_b = (ob_ref[...].astype(F32) * _silu(bz_ref[...].astype(F32))).astype(BF16)
        parts = []
        for h in range(GLA_HEADS):
            vs = slice(h * GLA_DV, (h + 1) * GLA_DV)
            x = cf_ref[:, vs].astype(F32) + cb_ref[:, vs].astype(F32)
            ms = jnp.mean(x * x, axis=-1, keepdims=True)
            y = x * lax.rsqrt(ms + EPS) * gn_ref[...]
            parts.append((y * _silu(cz_ref[:, vs].astype(F32))).astype(BF16))
        o_c = jnp.concatenate(parts, axis=-1)
        merged = None
        for n, (o, m_ref) in enumerate(((o_a, m0_ref), (o_b, m1_ref), (o_c, m2_ref))):
            term = _sigmoid(m_ref[...].astype(F32)) * jnp.dot(o, wb_ref[n], preferred_element_type=F32)
            merged = term if merged is None else merged + term
        ms = jnp.mean(out * out, axis=-1, keepdims=True)
        h = _pick(tiles_a + 1, *h_refs) if tiles_a else h_refs[0][...]
        o_ref[...] = h + out * lax.rsqrt(ms + EPS) * pg_ref[...]
        mg_write[...] = merged.astype(BF16)

    @pl.when(s % 2 == 0)
    def _():
        step(mg_odd, mg_even)

    @pl.when(s % 2 == 1)
    def _():
        step(mg_even, mg_odd)


def _merge(o_a, o_b, o_cf, o_cb, u, h_srcs, tm, wb, wo, gla_g, post_g):
    n = u.shape[0]
    n_tiles = n // tm
    tiles_a = h_srcs[0].shape[0] // tm if len(h_srcs) == 2 else None
    W = BRANCH_W

    def cur(width, col=0):
        return pl.BlockSpec((tm, width), lambda s: (jnp.minimum(s, n_tiles - 1), col))

    def prev(width):
        return pl.BlockSpec((tm, width), lambda s: (jnp.maximum(s - 1, 0), 0))

    def const(shape):
        return pl.BlockSpec(shape, lambda s: (0,) * len(shape), pipeline_mode=pl.Buffered(1))

    if tiles_a:
        h_specs = (pl.BlockSpec((tm, D_MODEL), lambda s: (jnp.clip(s - 1, 0, tiles_a - 1), 0)),
                   pl.BlockSpec((tm, D_MODEL), lambda s: (jnp.maximum(s - 1 - tiles_a, 0), 0)))
    else:
        h_specs = (prev(D_MODEL),)
    return pl.pallas_call(
        functools.partial(_merge_kernel, tiles_a),
        grid=(n_tiles + 1,),
        in_specs=[cur(W), cur(W), cur(W, B_Z // W), cur(W), cur(W), cur(W, C_Z // W),
                  cur(D_MODEL, M_LOG // D_MODEL), cur(D_MODEL, M_LOG // D_MODEL + 1),
                  cur(D_MODEL, M_LOG // D_MODEL + 2),
                  *h_specs,
                  const((3, W, D_MODEL)), const((D_MODEL, D_MODEL)),
                  const((1, GLA_DV)), const((1, D_MODEL))],
        out_specs=prev(D_MODEL),
        out_shape=jax.ShapeDtypeStruct((n, D_MODEL), F32),
        scratch_shapes=[pltpu.VMEM((tm, D_MODEL), BF16), pltpu.VMEM((tm, D_MODEL), BF16)],
        compiler_params=_cparams(("arbitrary",)),
        name="merge",
    )(o_a, o_b, u, o_cf, o_cb, u, u, u, u, *h_srcs, wb, wo, gla_g, post_g)


def _ple_kernel(tiles_a, last, h_ref, pa_ref, pb_ref, wp_ref, wg_ref, ng_ref, *out_refs):
    h = h_ref[...]
    p = _pick(tiles_a, pa_ref, pb_ref)
    e = jnp.dot(p.astype(BF16), wp_ref[...], preferred_element_type=F32)
    gate = _sigmoid(jnp.dot(h.astype(BF16), wg_ref[...], preferred_element_type=F32))
    h2 = h + gate * e
    if last:
        ya_ref, yb_ref = out_refs
        i = pl.program_id(0)

        @pl.when(i < tiles_a)
        def _():
            ya_ref[...] = h2

        @pl.when(i >= tiles_a)
        def _():
            yb_ref[...] = h2
    else:
        o_ref, xn_ref = out_refs
        o_ref[...] = h2
        ms = jnp.mean(h2 * h2, axis=-1, keepdims=True)
        xn_ref[...] = (h2 * lax.rsqrt(ms + EPS) * ng_ref[...]).astype(BF16)


def _ple(h, p_a, p_b, rows_a, layer, tm, wp, wg, next_g, last):
    n = h.shape[0]
    tiles_a = rows_a // tm
    tiles_b = (n - rows_a) // tm

    def rows(width):
        return pl.BlockSpec((tm, width), lambda i: (i, 0))

    def const(shape):
        return pl.BlockSpec(shape, lambda i: (0,) * len(shape), pipeline_mode=pl.Buffered(1))

    if last:
        out_specs = list(_split_specs(tm, D_MODEL, tiles_a))
        out_shape = [jax.ShapeDtypeStruct((rows_a, D_MODEL), F32),
                     jax.ShapeDtypeStruct((n - rows_a, D_MODEL), F32)]
    else:
        out_specs = [rows(D_MODEL), rows(D_MODEL)]
        out_shape = [jax.ShapeDtypeStruct((n, D_MODEL), F32), jax.ShapeDtypeStruct((n, D_MODEL), BF16)]
    return pl.pallas_call(
        functools.partial(_ple_kernel, tiles_a, last),
        grid=(n // tm,),
        in_specs=[rows(D_MODEL),
                  *_split_specs(tm, PLE_DIM, tiles_a, layer * tiles_a, layer * tiles_b),
                  const((PLE_DIM, D_MODEL)), const((D_MODEL, D_MODEL)), const((1, D_MODEL))],
        out_specs=out_specs,
        out_shape=out_shape,
        compiler_params=_cparams(("arbitrary",)),
        name="ple",
    )(h, p_a, p_b, wp, wg, next_g)


def _reorder_w_in(w_in):
    sizes = (1024, 1024, 1024, 1024, 256, 256, 1024, 512, 512, 1024, 2 * GLA_RANK, 1024, 3 * D_MODEL)
    names = ("a_val", "a_glu", "a_z", "b_q", "b_k", "b_v", "b_z", "c_q", "c_k", "c_v", "c_lr", "c_z", "m")
    parts, start = {}, 0
    for name, s in zip(names, sizes):
        parts[name] = w_in[..., start:start + s].astype(BF16)
        start += s
    parts["b_q"] = _permute_heads(parts["b_q"], N_Q_HEADS)
    parts["b_k"] = _permute_heads(parts["b_k"], N_KV_HEADS)
    pad = jnp.zeros(w_in.shape[:-1] + (U_COLS - C_LR - 2 * GLA_RANK,), BF16)
    order = ("a_val", "a_glu", "a_z", "b_q", "b_z", "c_v", "c_z", "c_q", "c_k", "m", "b_k", "b_v", "c_lr")
    w = jnp.concatenate([parts[k] for k in order] + [pad], axis=-1)
    assert w.shape[-1] == U_COLS
    return w


def _rope_tables(T):
    n_rows = T // GRID_W
    row_idx = jnp.repeat(jnp.arange(n_rows), GRID_W).astype(F32)
    col_idx = jnp.tile(jnp.arange(GRID_W), n_rows).astype(F32)
    inv = ROPE_THETA ** (-jnp.arange(ROPE_PAIRS, dtype=F32) / ROPE_PAIRS)
    ang_r = row_idx[:, None] * inv
    ang_c = col_idx[:, None] * inv
    cos = jnp.concatenate([jnp.cos(ang_r), jnp.cos(ang_c), jnp.cos(ang_r), jnp.cos(ang_c)], axis=-1)
    sin = jnp.concatenate([-jnp.sin(ang_r), -jnp.sin(ang_c), jnp.sin(ang_r), jnp.sin(ang_c)], axis=-1)
    return cos, sin


_ROPE_PERM_BLOCKS = (0, 2, 1, 3)


def _permute_heads(w, n_heads):
    shaped = w.reshape(w.shape[:-1] + (n_heads, HEAD_DIM))
    blocks = [shaped[..., b * ROPE_PAIRS:(b + 1) * ROPE_PAIRS] for b in _ROPE_PERM_BLOCKS]
    return jnp.concatenate(blocks, axis=-1).reshape(w.shape)


def _gla_gate_weights(gate_w2):
    depth = gate_w2.shape[0]
    w = jnp.zeros((depth, 2, 128, GLA_HEADS * GLA_DK), F32)
    for d in range(2):
        w = w.at[:, d, d * GLA_RANK:(d + 1) * GLA_RANK, :].set(gate_w2[:, d])
    return w.astype(BF16)


def kernel(x_prompt, x_sample, p_prompt, p_sample, pre_norm_g, post_norm_g, w_in, conv_w, conv_b, conv_ln_g, conv_ln_b, q_norm_g, k_norm_g, gla_gate_w2, gla_gate_b, gla_norm_g, w_branch, w_out, w_ple, w_ple_gate):
    depth = w_in.shape[0]
    T = x_prompt.shape[1]
    assert x_sample.shape[1] == T
    B = x_prompt.shape[0] + x_sample.shape[0]
    n = B * T
    x_a = x_prompt.reshape(-1, D_MODEL)
    x_b = x_sample.reshape(-1, D_MODEL)
    rows_a = x_a.shape[0]
    p_a = p_prompt.reshape(-1, PLE_DIM)
    p_b = p_sample.reshape(-1, PLE_DIM)
    tm_merge = _tile(T, 256, 16)
    tm_ple = _tile(T, 512, 16)

    w_in_r = [_reorder_w_in(w_in[l]) for l in range(depth)]
    w_branch_b = [w_branch[l].astype(BF16) for l in range(depth)]
    w_out_b = [w_out[l].astype(BF16) for l in range(depth)]
    w_ple_b = [w_ple[l].astype(BF16) for l in range(depth)]
    w_gate_b = [w_ple_gate[l].astype(BF16) for l in range(depth)]
    w2p = _gla_gate_weights(gla_gate_w2)
    conv_w_p = jnp.pad(conv_w, ((0, 0), (0, 1), (0, 0)))
    cos, sin = _rope_tables(T)
    gq_perm = _permute_heads(q_norm_g, 1)
    gk_perm = _permute_heads(k_norm_g, 1)
    tk = _tile(T // 2, 512, 128)

    xn = _prenorm(x_a, x_b, pre_norm_g[0][None], tm_ple)
    h_srcs = (x_a, x_b)
    for l in range(depth):
        u = _inproj(xn, w_in_r[l])
        u3 = u.reshape(B, T, U_COLS)
        o_a = _conv_branch(u3, conv_w_p[l], conv_b[l][None], conv_ln_g[l][None], conv_ln_b[l][None])
        q, k, vT = _qkprep(u3, cos, sin, gq_perm[l][None], gk_perm[l][None], tk)
        o_b = _attention(q, k, vT)
        o_cf, o_cb = _gla(u3, w2p[l], gla_gate_b[l][:, None, :])
        h = _merge(o_a.reshape(n, BRANCH_W), o_b.reshape(n, BRANCH_W), o_cf.reshape(n, BRANCH_W),
                   o_cb.reshape(n, BRANCH_W), u, h_srcs, tm_merge,
                   w_branch_b[l], w_out_b[l], gla_norm_g[l][None], post_norm_g[l][None])
        last = l == depth - 1
        next_g = pre_norm_g[(l + 1) % depth][None]
        outs = _ple(h, p_a, p_b, rows_a, l, tm_ple, w_ple_b[l], w_gate_b[l], next_g, last)
        if last:
            y_prompt, y_sample = outs
        else:
            h, xn = outs
            h_srcs = (h,)
    return (y_prompt.reshape(x_prompt.shape), y_sample.reshape(x_sample.shape))
```

```python
import functools
import math

import jax
import jax.numpy as jnp
from jax import lax
from jax.experimental import pallas as pl
from jax.experimental.pallas import tpu as pltpu

F32 = jnp.float32
BF16 = jnp.bfloat16

D_MODEL = 2048
GRID_W = 64
PLE_DIM = 256
BRANCH_W = 1024
EPS = 1e-6
CONV_K = 31
CONV_HALO = 16
CONV_ROWS = 128
N_Q_HEADS = 8
N_KV_HEADS = 2
Q_PER_KV = N_Q_HEADS // N_KV_HEADS
HEAD_DIM = 128
V_PAD_ROWS = 16
ROPE_PAIRS = HEAD_DIM // 4
ROPE_THETA = 10000.0
GLA_HEADS = 4
GLA_DK = 128
GLA_DV = 256
GLA_RANK = 16
GLA_GATE_NORM = 16.0
GLA_CHUNK = 64

A_VAL, A_GLU, A_Z = 0, 1024, 2048
B_Q, B_Z = 3072, 4096
C_V, C_Z = 5120, 6144
C_Q, C_K = 7168, 7680
M_LOG = 8192
B_KV = 14336
C_LR = 14848
MXU_COLS = 256
INPROJ_TN = 6 * MXU_COLS
U_COLS = 15360

VMEM_LIMIT = 56 * 1024 * 1024


def _cparams(sem, vmem=VMEM_LIMIT):
    return pltpu.CompilerParams(dimension_semantics=sem, vmem_limit_bytes=vmem)


def _tile(n, cap, mult=8):
    t = min(cap, n)
    while t > mult and (n % t or t % mult):
        t -= mult
    assert n % t == 0, (n, cap, mult)
    return t


def _sigmoid(x):
    return 1.0 / (1.0 + jnp.exp(-x))


def _silu(x):
    return x * _sigmoid(x)


def _split_specs(tm, width, tiles_a, a_first=0, b_first=0, col=0):
    a = pl.BlockSpec((tm, width), lambda i: (a_first + jnp.minimum(i, tiles_a - 1), col))
    b = pl.BlockSpec((tm, width), lambda i: (b_first + jnp.maximum(i - tiles_a, 0), col))
    return a, b


def _pick(tiles_a, a_ref, b_ref):
    return jnp.where(pl.program_id(0) < tiles_a, a_ref[...], b_ref[...])


def _prenorm_kernel(tiles_a, xa_ref, xb_ref, g_ref, o_ref):
    x = _pick(tiles_a, xa_ref, xb_ref)
    ms = jnp.mean(x * x, axis=-1, keepdims=True)
    o_ref[...] = (x * lax.rsqrt(ms + EPS) * g_ref[...]).astype(BF16)


def _prenorm(xa, xb, g, tm):
    n = xa.shape[0] + xb.shape[0]
    tiles_a = xa.shape[0] // tm
    return pl.pallas_call(
        functools.partial(_prenorm_kernel, tiles_a),
        grid=(n // tm,),
        in_specs=[*_split_specs(tm, D_MODEL, tiles_a),
                  pl.BlockSpec((1, D_MODEL), lambda i: (0, 0))],
        out_specs=pl.BlockSpec((tm, D_MODEL), lambda i: (i, 0)),
        out_shape=jax.ShapeDtypeStruct((n, D_MODEL), BF16),
        compiler_params=_cparams(("arbitrary",)),
        name="prenorm",
    )(xa, xb, g)


def _inproj_kernel(x_ref, w_ref, u_ref):
    u_ref[...] = jnp.dot(x_ref[...], w_ref[...], preferred_element_type=F32).astype(BF16)


def _inproj(xn, w):
    n = xn.shape[0]
    tm = _tile(n, 1024, 16)
    tn = INPROJ_TN
    return pl.pallas_call(
        _inproj_kernel,
        grid=(n // tm, U_COLS // tn),
        in_specs=[pl.BlockSpec((tm, D_MODEL), lambda i, j: (i, 0)),
                  pl.BlockSpec((D_MODEL, tn), lambda i, j: (0, j))],
        out_specs=pl.BlockSpec((tm, tn), lambda i, j: (i, j)),
        out_shape=jax.ShapeDtypeStruct((n, U_COLS), BF16),
        compiler_params=_cparams(("parallel", "arbitrary")),
        name="inproj",
    )(xn, w)


def _conv_kernel(val_ref, glu_ref, z_ref, pval_ref, pglu_ref, nval_ref, nglu_ref,
                 w_ref, cb_ref, lg_ref, lb_ref, o_ref, abuf, cbuf):
    i = pl.program_id(1)
    n = pl.num_programs(1)
    tt = val_ref.shape[0]
    H = CONV_HALO

    def glu(v_ref, g_ref):
        return v_ref[...].astype(F32) * _sigmoid(g_ref[...].astype(F32))

    abuf[0:H, :] = glu(pval_ref, pglu_ref) * (i > 0).astype(F32)
    abuf[H:H + tt, :] = glu(val_ref, glu_ref)
    abuf[H + tt:2 * H + tt, :] = glu(nval_ref, nglu_ref) * (i < n - 1).astype(F32)

    assert H - CONV_K // 2 == 1 and CONV_K < 32
    rows = CONV_ROWS if tt % CONV_ROWS == 0 else tt
    for r0 in range(0, tt, rows):
        for c in range(BRANCH_W // 128):
            cs = slice(c * 128, (c + 1) * 128)
            acc = None
            for r in range(8):
                part = None
                for q in range(4):
                    j = 8 * q + r
                    if j == 0:
                        continue
                    term = abuf[r0 + 8 * q:r0 + 8 * q + rows + 8, cs] * w_ref[j - 1:j, cs]
                    part = term if part is None else part + term
                shifted = part[r:r + rows, :]
                acc = shifted if acc is None else acc + shifted
            cbuf[r0:r0 + rows, cs] = acc + cb_ref[:, cs]

    x = cbuf[...]
    mu = jnp.mean(x, axis=-1, keepdims=True)
    xc = x - mu
    var = jnp.mean(xc * xc, axis=-1, keepdims=True)
    y = xc * lax.rsqrt(var + EPS) * lg_ref[...] + lb_ref[...]
    o_ref[...] = (_silu(y) * _silu(z_ref[...].astype(F32))).astype(BF16)


def _conv_branch(u3, conv_w, conv_b, ln_g, ln_b):
    B, T, _ = u3.shape
    tt = _tile(T, 2 * CONV_ROWS, 16)
    H = CONV_HALO
    hb = tt // H
    nh = T // H

    def cur(col):
        return pl.BlockSpec((None, tt, BRANCH_W), lambda b, i: (b, i, col))

    def prev(col):
        return pl.BlockSpec((None, H, BRANCH_W), lambda b, i: (b, jnp.maximum(i * hb - 1, 0), col))

    def nxt(col):
        return pl.BlockSpec((None, H, BRANCH_W), lambda b, i: (b, jnp.minimum((i + 1) * hb, nh - 1), col))

    vec = pl.BlockSpec((1, BRANCH_W), lambda b, i: (0, 0))
    return pl.pallas_call(
        _conv_kernel,
        grid=(B, T // tt),
        in_specs=[cur(A_VAL // BRANCH_W), cur(A_GLU // BRANCH_W), cur(A_Z // BRANCH_W),
                  prev(A_VAL // BRANCH_W), prev(A_GLU // BRANCH_W),
                  nxt(A_VAL // BRANCH_W), nxt(A_GLU // BRANCH_W),
                  pl.BlockSpec((CONV_K + 1, BRANCH_W), lambda b, i: (0, 0)), vec, vec, vec],
        out_specs=pl.BlockSpec((None, tt, BRANCH_W), lambda b, i: (b, i, 0)),
        out_shape=jax.ShapeDtypeStruct((B, T, BRANCH_W), BF16),
        scratch_shapes=[pltpu.VMEM((tt + 2 * H, BRANCH_W), F32), pltpu.VMEM((tt, BRANCH_W), F32)],
        compiler_params=_cparams(("parallel", "parallel")),
        name="conv",
    )(u3, u3, u3, u3, u3, u3, u3, conv_w, conv_b, ln_g, ln_b)


def _norm_rope(x, g, cos, sin):
    ms = jnp.mean(x * x, axis=-1, keepdims=True)
    y = x * lax.rsqrt(ms + EPS) * g
    return y * cos + pltpu.roll(y, HEAD_DIM // 2, 1) * sin


def _qkprep_kernel(q_ref, kv_ref, cos_ref, sin_ref, gq_ref, gk_ref, qo_ref, k_ref, vT_ref):
    tt = q_ref.shape[0]
    cos = cos_ref[...]
    sin = sin_ref[...]
    qscale = HEAD_DIM ** -0.5 * math.log2(math.e)
    for h in range(N_Q_HEADS):
        x = q_ref[:, h * HEAD_DIM:(h + 1) * HEAD_DIM].astype(F32)
        y = _norm_rope(x, gq_ref[...], cos, sin) * qscale
        qo_ref[h] = y.T.astype(BF16)
    for g in range(N_KV_HEADS):
        x = kv_ref[:, g * HEAD_DIM:(g + 1) * HEAD_DIM].astype(F32)
        k_ref[g] = _norm_rope(x, gk_ref[...], cos, sin).astype(BF16)
        v = kv_ref[:, (N_KV_HEADS + g) * HEAD_DIM:(N_KV_HEADS + g + 1) * HEAD_DIM].astype(F32)
        vT_ref[g, 0, :HEAD_DIM, :] = v.T.astype(BF16)
        pad_row = lax.broadcasted_iota(jnp.int32, (V_PAD_ROWS, tt), 0)
        vT_ref[g, 0, HEAD_DIM:, :] = jnp.where(pad_row == 0, 1.0, 0.0).astype(BF16)


def _qkprep(u3, cos, sin, gq, gk, tk):
    B, T, _ = u3.shape
    qw = N_Q_HEADS * HEAD_DIM
    kvw = 2 * N_KV_HEADS * HEAD_DIM
    vec = pl.BlockSpec((1, HEAD_DIM), lambda b, i: (0, 0))
    tab = pl.BlockSpec((tk, HEAD_DIM), lambda b, i: (i, 0))
    return pl.pallas_call(
        _qkprep_kernel,
        grid=(B, T // tk),
        in_specs=[pl.BlockSpec((None, tk, qw), lambda b, i: (b, i, B_Q // qw)),
                  pl.BlockSpec((None, tk, kvw), lambda b, i: (b, i, B_KV // kvw)),
                  tab, tab, vec, vec],
        out_specs=[pl.BlockSpec((None, N_Q_HEADS, HEAD_DIM, tk), lambda b, i: (b, 0, 0, i)),
                   pl.BlockSpec((None, N_KV_HEADS, tk, HEAD_DIM), lambda b, i: (b, 0, i, 0)),
                   pl.BlockSpec((None, N_KV_HEADS, 1, HEAD_DIM + V_PAD_ROWS, tk), lambda b, i: (b, 0, i, 0, 0))],
        out_shape=[jax.ShapeDtypeStruct((B, N_Q_HEADS, HEAD_DIM, T), BF16),
                   jax.ShapeDtypeStruct((B, N_KV_HEADS, T, HEAD_DIM), BF16),
                   jax.ShapeDtypeStruct((B, N_KV_HEADS, T // tk, HEAD_DIM + V_PAD_ROWS, tk), BF16)],
        compiler_params=_cparams(("parallel", "parallel")),
        name="qkprep",
    )(u3, u3, cos, sin, gq, gk)


def _col_reduce(op, x):
    rows, cols = x.shape
    if rows % 64 == 0:
        x = op(x.reshape(8, rows // 8, cols), axis=0)
    return op(x, axis=0, keepdims=True)


def _attn_kernel(q_ref, k_ref, vT_ref, o_ref, acc_ref, s_ref):
    nk, _, tk = vT_ref.shape
    tq = q_ref.shape[2]
    unroll = next(u for u in (8, 4, 2) if nk % u == 0)
    heads = range(Q_PER_KV)
    acc_ref[...] = jnp.zeros_like(acc_ref)

    def scores(j, slot, h):
        k = k_ref[pl.ds(pl.multiple_of(j * tk, tk), tk), :]
        sT = jnp.dot(k, q_ref[h], preferred_element_type=F32)
        s_ref[slot, h] = sT
        return _col_reduce(jnp.max, sT)

    def consume(j, slot, h, tile_max, m):
        m_new = jnp.maximum(m, tile_max)
        alpha = jnp.exp2(m - m_new)
        p = jnp.exp2(s_ref[slot, h] - m_new).astype(BF16)
        pv = jnp.dot(vT_ref[j], p, preferred_element_type=F32)
        acc_ref[h] = alpha * acc_ref[h] + pv
        return m_new

    def body(i, carry):
        tile_max, ms = carry
        j = unroll * i
        for u in range(unroll):
            nxt = j + u + 1
            if u == unroll - 1:
                nxt = jnp.minimum(nxt, nk - 1)
            next_max = []
            new_ms = []
            for h in heads:
                next_max.append(scores(nxt, (u + 1) % 2, h))
                new_ms.append(consume(j + u, u % 2, h, tile_max[h], ms[h]))
            tile_max, ms = tuple(next_max), tuple(new_ms)
        return tile_max, ms

    m0 = tuple(jnp.full((1, tq), -jnp.inf, F32) for _ in heads)
    lax.fori_loop(0, nk // unroll, body, (tuple(scores(0, 0, h) for h in heads), m0))
    for h in range(Q_PER_KV):
        a = acc_ref[h]
        o = a[:HEAD_DIM] * (1.0 / a[HEAD_DIM:HEAD_DIM + 1])
        o_ref[:, h * HEAD_DIM:(h + 1) * HEAD_DIM] = o.T.astype(BF16)


def _attention(q, k, vT):
    B, _, _, T = q.shape
    nk, hv, tk = vT.shape[2], vT.shape[3], vT.shape[4]
    tq = _tile(T, 512, 128)
    gw = Q_PER_KV * HEAD_DIM
    return pl.pallas_call(
        _attn_kernel,
        grid=(B, N_KV_HEADS, T // tq),
        in_specs=[pl.BlockSpec((None, Q_PER_KV, HEAD_DIM, tq), lambda b, g, i: (b, g, 0, i)),
                  pl.BlockSpec((None, None, T, HEAD_DIM), lambda b, g, i: (b, g, 0, 0)),
                  pl.BlockSpec((None, None, nk, hv, tk), lambda b, g, i: (b, g, 0, 0, 0))],
        out_specs=pl.BlockSpec((None, tq, gw), lambda b, g, i: (b, i, g)),
        out_shape=jax.ShapeDtypeStruct((B, T, N_Q_HEADS * HEAD_DIM), BF16),
        scratch_shapes=[pltpu.VMEM((Q_PER_KV, hv, tq), F32),
                        pltpu.VMEM((2, Q_PER_KV, tk, tq), F32)],
        compiler_params=_cparams(("parallel", "parallel", "arbitrary")),
        name="attention",
    )(q, k, vT)


def _log_sigmoid(z):
    return jnp.minimum(z, 0.0) - jnp.log(1.0 + jnp.exp(-jnp.abs(z)))


def _gla_kernel(qf_ref, kf_ref, vf_ref, lrf_ref, qb_ref, kb_ref, vb_ref, lrb_ref,
                w2_ref, gb_ref, of_ref, ob_ref, s_ref):
    i = pl.program_id(1)
    tb = qf_ref.shape[0]
    L = GLA_CHUNK
    nchunk = tb // L

    @pl.when(i == 0)
    def _():
        s_ref[...] = jnp.zeros_like(s_ref)

    row = lax.broadcasted_iota(jnp.int32, (L, L), 0)
    col = lax.broadcasted_iota(jnp.int32, (L, L), 1)
    masks = (row >= col, col >= row)
    ones = tuple(jnp.where(m, 1.0, 0.0).astype(BF16) for m in masks)
    qscale = GLA_DK ** -0.5
    dirs = ((qf_ref, kf_ref, vf_ref, lrf_ref, of_ref), (qb_ref, kb_ref, vb_ref, lrb_ref, ob_ref))

    log_gate = []
    for d in range(2):
        z = jnp.dot(dirs[d][3][...], w2_ref[d], preferred_element_type=F32) + gb_ref[d]
        log_gate.append(_log_sigmoid(z) * (1.0 / GLA_GATE_NORM))

    cum = {}
    for step in range(nchunk):
        for d in range(2):
            c = step if d == 0 else nchunk - 1 - step
            g = log_gate[d][c * L:(c + 1) * L, :]
            g_hi = g.astype(BF16)
            g_lo = (g - g_hi.astype(F32)).astype(BF16)
            cum[step, d] = (jnp.dot(ones[d], g_hi, preferred_element_type=F32)
                            + jnp.dot(ones[d], g_lo, preferred_element_type=F32))
    scaled = {}
    for step in range(nchunk):
        for d in range(2):
            q_ref, k_ref = dirs[d][0], dirs[d][1]
            c = step if d == 0 else nchunk - 1 - step
            rows = slice(c * L, (c + 1) * L)
            for h in range(GLA_HEADS):
                ks = slice(h * GLA_DK, (h + 1) * GLA_DK)
                b = cum[step, d][:, ks]
                b_tot = b[L - 1:L, :] if d == 0 else b[0:1, :]
                q = q_ref[rows, ks].astype(F32) * qscale
                k = k_ref[rows, ks].astype(F32)
                q_t = (q * jnp.exp(b)).astype(BF16)
                k_t = (k * jnp.exp(-b)).astype(BF16)
                k_d = (k * jnp.exp(b_tot - b)).astype(BF16)
                scaled[step, d, h] = (q_t, k_t, k_d, jnp.exp(b_tot))
    pre = {}
    for step in range(nchunk):
        for d in range(2):
            for h in range(GLA_HEADS):
                q_t, k_t, k_d, e_tot = scaled[step, d, h]
                att = lax.dot_general(q_t, k_t, (((1,), (1,)), ((), ())), preferred_element_type=F32)
                att = jnp.where(masks[d], att, 0.0).astype(BF16)
                pre[step, d, h] = (q_t, k_d, att, e_tot)

    for step in range(nchunk):
        for d in range(2):
            v_ref, o_ref = dirs[d][2], dirs[d][4]
            c = step if d == 0 else nchunk - 1 - step
            rows = slice(c * L, (c + 1) * L)
            for h in range(GLA_HEADS):
                vs = slice(h * GLA_DV, (h + 1) * GLA_DV)
                q_t, k_d, att, e_tot = pre[step, d, h]
                v = v_ref[rows, vs]
                sT = s_ref[d, h]
                o = (jnp.dot(att, v, preferred_element_type=F32)
                     + lax.dot_general(q_t, sT.astype(BF16), (((1,), (1,)), ((), ())),
                                       preferred_element_type=F32))
                o_ref[rows, vs] = o.astype(o_ref.dtype)
                dsT = lax.dot_general(v, k_d, (((0,), (0,)), ((), ())), preferred_element_type=F32)
                s_ref[d, h] = sT * e_tot + dsT


def _gla(u3, w2p, gb):
    B, T, _ = u3.shape
    tb = _tile(T, 512, GLA_CHUNK)
    nb = T // tb
    qk_w = GLA_HEADS * GLA_DK
    v_w = GLA_HEADS * GLA_DV

    def specs(idx):
        return [pl.BlockSpec((None, tb, qk_w), lambda b, i: (b, idx(i), C_Q // qk_w)),
                pl.BlockSpec((None, tb, qk_w), lambda b, i: (b, idx(i), C_K // qk_w)),
                pl.BlockSpec((None, tb, v_w), lambda b, i: (b, idx(i), C_V // v_w)),
                pl.BlockSpec((None, tb, 128), lambda b, i: (b, idx(i), C_LR // 128))]

    fwd = lambda i: i
    bwd = lambda i: nb - 1 - i
    out_sd = jax.ShapeDtypeStruct((B, T, v_w), BF16)
    return pl.pallas_call(
        _gla_kernel,
        grid=(B, nb),
        in_specs=specs(fwd) + specs(bwd) + [
            pl.BlockSpec((2, 128, qk_w), lambda b, i: (0, 0, 0)),
            pl.BlockSpec((2, 1, qk_w), lambda b, i: (0, 0, 0))],
        out_specs=[pl.BlockSpec((None, tb, v_w), lambda b, i: (b, i, 0)),
                   pl.BlockSpec((None, tb, v_w), lambda b, i: (b, nb - 1 - i, 0))],
        out_shape=[out_sd, out_sd],
        scratch_shapes=[pltpu.VMEM((2, GLA_HEADS, GLA_DV, GLA_DK), F32)],
        compiler_params=_cparams(("parallel", "arbitrary")),
        name="gla",
    )(u3, u3, u3, u3, u3, u3, u3, u3, w2p, gb)


def _merge_kernel(tiles_a, oa_ref, ob_ref, bz_ref, cf_ref, cb_ref, cz_ref, m0_ref, m1_ref, m2_ref,
                  *rest):
    *h_refs, wb_ref, wo_ref, gn_ref, pg_ref, o_ref, mg_even, mg_odd = rest
    s = pl.program_id(0)

    @pl.when(s == 0)
    def _():
        mg_odd[...] = jnp.zeros_like(mg_odd)

    def step(mg_read, mg_write):
        out = jnp.dot(mg_read[...], wo_ref[...], preferred_element_type=F32)
        o_a = oa_ref[...]
        o_b = (ob_ref[...].astype(F32) * _silu(bz_ref[...].astype(F32))).astype(BF16)
        parts = []
        for h in range(GLA_HEADS):
            vs = slice(h * GLA_DV, (h + 1) * GLA_DV)
            x = cf_ref[:, vs].astype(F32) + cb_ref[:, vs].astype(F32)
            ms = jnp.mean(x * x, axis=-1, keepdims=True)
            y = x * lax.rsqrt(ms + EPS) * gn_ref[...]
            parts.append((y * _silu(cz_ref[:, vs].astype(F32))).astype(BF16))
        o_c = jnp.concatenate(parts, axis=-1)
        merged = None
        for n, (o, m_ref) in enumerate(((o_a, m0_ref), (o_b, m1_ref), (o_c, m2_ref))):
            term = _sigmoid(m_ref[...].astype(F32)) * jnp.dot(o, wb_ref[n], preferred_element_type=F32)
            merged = term if merged is None else merged + term
        ms = jnp.mean(out * out, axis=-1, keepdims=True)
        h = _pick(tiles_a + 1, *h_refs) if tiles_a else h_refs[0][...]
        o_ref[...] = h + out * lax.rsqrt(ms + EPS) * pg_ref[...]
        mg_write[...] = merged.astype(BF16)

    @pl.when(s % 2 == 0)
    def _():
        step(mg_odd, mg_even)

    @pl.when(s % 2 == 1)
    def _():
        step(mg_even, mg_odd)


def _merge(o_a, o_b, o_cf, o_cb, u, h_srcs, tm, wb, wo, gla_g, post_g):
    n = u.shape[0]
    n_tiles = n // tm
    tiles_a = h_srcs[0].shape[0] // tm if len(h_srcs) == 2 else None
    W = BRANCH_W

    def cur(width, col=0):
        return pl.BlockSpec((tm, width), lambda s: (jnp.minimum(s, n_tiles - 1), col))

    def prev(width):
        return pl.BlockSpec((tm, width), lambda s: (jnp.maximum(s - 1, 0), 0))

    def const(shape):
        return pl.BlockSpec(shape, lambda s: (0,) * len(shape), pipeline_mode=pl.Buffered(1))

    if tiles_a:
        h_specs = (pl.BlockSpec((tm, D_MODEL), lambda s: (jnp.clip(s - 1, 0, tiles_a - 1), 0)),
                   pl.BlockSpec((tm, D_MODEL), lambda s: (jnp.maximum(s - 1 - tiles_a, 0), 0)))
    else:
        h_specs = (prev(D_MODEL),)
    return pl.pallas_call(
        functools.partial(_merge_kernel, tiles_a),
        grid=(n_tiles + 1,),
        in_specs=[cur(W), cur(W), cur(W, B_Z // W), cur(W), cur(W), cur(W, C_Z // W),
                  cur(D_MODEL, M_LOG // D_MODEL), cur(D_MODEL, M_LOG // D_MODEL + 1),
                  cur(D_MODEL, M_LOG // D_MODEL + 2),
                  *h_specs,
                  const((3, W, D_MODEL)), const((D_MODEL, D_MODEL)),
                  const((1, GLA_DV)), const((1, D_MODEL))],
        out_specs=prev(D_MODEL),
        out_shape=jax.ShapeDtypeStruct((n, D_MODEL), F32),
        scratch_shapes=[pltpu.VMEM((tm, D_MODEL), BF16), pltpu.VMEM((tm, D_MODEL), BF16)],
        compiler_params=_cparams(("arbitrary",)),
        name="merge",
    )(o_a, o_b, u, o_cf, o_cb, u, u, u, u, *h_srcs, wb, wo, gla_g, post_g)


def _ple_kernel(tiles_a, last, h_ref, pa_ref, pb_ref, wp_ref, wg_ref, ng_ref, *out_refs):
    h = h_ref[...]
    p = _pick(tiles_a, pa_ref, pb_ref)
    e = jnp.dot(p.astype(BF16), wp_ref[...], preferred_element_type=F32)
    gate = _sigmoid(jnp.dot(h.astype(BF16), wg_ref[...], preferred_element_type=F32))
    h2 = h + gate * e
    if last:
        ya_ref, yb_ref = out_refs
        i = pl.program_id(0)

        @pl.when(i < tiles_a)
        def _():
            ya_ref[...] = h2

        @pl.when(i >= tiles_a)
        def _():
            yb_ref[...] = h2
    else:
        o_ref, xn_ref = out_refs
        o_ref[...] = h2
        ms = jnp.mean(h2 * h2, axis=-1, keepdims=True)
        xn_ref[...] = (h2 * lax.rsqrt(ms + EPS) * ng_ref[...]).astype(BF16)


def _ple(h, p_a, p_b, rows_a, layer, tm, wp, wg, next_g, last):
    n = h.shape[0]
    tiles_a = rows_a // tm
    tiles_b = (n - rows_a) // tm

    def rows(width):
        return pl.BlockSpec((tm, width), lambda i: (i, 0))

    def const(shape):
        return pl.BlockSpec(shape, lambda i: (0,) * len(shape), pipeline_mode=pl.Buffered(1))

    if last:
        out_specs = list(_split_specs(tm, D_MODEL, tiles_a))
        out_shape = [jax.ShapeDtypeStruct((rows_a, D_MODEL), F32),
                     jax.ShapeDtypeStruct((n - rows_a, D_MODEL), F32)]
    else:
        out_specs = [rows(D_MODEL), rows(D_MODEL)]
        out_shape = [jax.ShapeDtypeStruct((n, D_MODEL), F32), jax.ShapeDtypeStruct((n, D_MODEL), BF16)]
    return pl.pallas_call(
        functools.partial(_ple_kernel, tiles_a, last),
        grid=(n // tm,),
        in_specs=[rows(D_MODEL),
                  *_split_specs(tm, PLE_DIM, tiles_a, layer * tiles_a, layer * tiles_b),
                  const((PLE_DIM, D_MODEL)), const((D_MODEL, D_MODEL)), const((1, D_MODEL))],
        out_specs=out_specs,
        out_shape=out_shape,
        compiler_params=_cparams(("arbitrary",)),
        name="ple",
    )(h, p_a, p_b, wp, wg, next_g)


def _reorder_w_in(w_in):
    sizes = (1024, 1024, 1024, 1024, 256, 256, 1024, 512, 512, 1024, 2 * GLA_RANK, 1024, 3 * D_MODEL)
    names = ("a_val", "a_glu", "a_z", "b_q", "b_k", "b_v", "b_z", "c_q", "c_k", "c_v", "c_lr", "c_z", "m")
    parts, start = {}, 0
    for name, s in zip(names, sizes):
        parts[name] = w_in[..., start:start + s].astype(BF16)
        start += s
    parts["b_q"] = _permute_heads(parts["b_q"], N_Q_HEADS)
    parts["b_k"] = _permute_heads(parts["b_k"], N_KV_HEADS)
    pad = jnp.zeros(w_in.shape[:-1] + (U_COLS - C_LR - 2 * GLA_RANK,), BF16)
    order = ("a_val", "a_glu", "a_z", "b_q", "b_z", "c_v", "c_z", "c_q", "c_k", "m", "b_k", "b_v", "c_lr")
    w = jnp.concatenate([parts[k] for k in order] + [pad], axis=-1)
    assert w.shape[-1] == U_COLS
    return w


def _rope_tables(T):
    n_rows = T // GRID_W
    row_idx = jnp.repeat(jnp.arange(n_rows), GRID_W).astype(F32)
    col_idx = jnp.tile(jnp.arange(GRID_W), n_rows).astype(F32)
    inv = ROPE_THETA ** (-jnp.arange(ROPE_PAIRS, dtype=F32) / ROPE_PAIRS)
    ang_r = row_idx[:, None] * inv
    ang_c = col_idx[:, None] * inv
    cos = jnp.concatenate([jnp.cos(ang_r), jnp.cos(ang_c), jnp.cos(ang_r), jnp.cos(ang_c)], axis=-1)
    sin = jnp.concatenate([-jnp.sin(ang_r), -jnp.sin(ang_c), jnp.sin(ang_r), jnp.sin(ang_c)], axis=-1)
    return cos, sin


_ROPE_PERM_BLOCKS = (0, 2, 1, 3)


def _permute_heads(w, n_heads):
    shaped = w.reshape(w.shape[:-1] + (n_heads, HEAD_DIM))
    blocks = [shaped[..., b * ROPE_PAIRS:(b + 1) * ROPE_PAIRS] for b in _ROPE_PERM_BLOCKS]
    return jnp.concatenate(blocks, axis=-1).reshape(w.shape)


def _gla_gate_weights(gate_w2):
    depth = gate_w2.shape[0]
    w = jnp.zeros((depth, 2, 128, GLA_HEADS * GLA_DK), F32)
    for d in range(2):
        w = w.at[:, d, d * GLA_RANK:(d + 1) * GLA_RANK, :].set(gate_w2[:, d])
    return w.astype(BF16)


def kernel(x_prompt, x_sample, p_prompt, p_sample, pre_norm_g, post_norm_g, w_in, conv_w, conv_b, conv_ln_g, conv_ln_b, q_norm_g, k_norm_g, gla_gate_w2, gla_gate_b, gla_norm_g, w_branch, w_out, w_ple, w_ple_gate):
    depth = w_in.shape[0]
    T = x_prompt.shape[1]
    assert x_sample.shape[1] == T
    B = x_prompt.shape[0] + x_sample.shape[0]
    n = B * T
    x_a = x_prompt.reshape(-1, D_MODEL)
    x_b = x_sample.reshape(-1, D_MODEL)
    rows_a = x_a.shape[0]
    p_a = p_prompt.reshape(-1, PLE_DIM)
    p_b = p_sample.reshape(-1, PLE_DIM)
    tm_merge = _tile(T, 256, 16)
    tm_ple = _tile(T, 512, 16)

    w_in_r = [_reorder_w_in(w_in[l]) for l in range(depth)]
    w_branch_b = [w_branch[l].astype(BF16) for l in range(depth)]
    w_out_b = [w_out[l].astype(BF16) for l in range(depth)]
    w_ple_b = [w_ple[l].astype(BF16) for l in range(depth)]
    w_gate_b = [w_ple_gate[l].astype(BF16) for l in range(depth)]
    w2p = _gla_gate_weights(gla_gate_w2)
    conv_w_p = jnp.pad(conv_w, ((0, 0), (0, 1), (0, 0)))
    cos, sin = _rope_tables(T)
    gq_perm = _permute_heads(q_norm_g, 1)
    gk_perm = _permute_heads(k_norm_g, 1)
    tk = _tile(T // 2, 512, 128)

    xn = _prenorm(x_a, x_b, pre_norm_g[0][None], tm_ple)
    h_srcs = (x_a, x_b)
    for l in range(depth):
        u = _inproj(xn, w_in_r[l])
        u3 = u.reshape(B, T, U_COLS)
        o_a = _conv_branch(u3, conv_w_p[l], conv_b[l][None], conv_ln_g[l][None], conv_ln_b[l][None])
        q, k, vT = _qkprep(u3, cos, sin, gq_perm[l][None], gk_perm[l][None], tk)
        o_b = _attention(q, k, vT)
        o_cf, o_cb = _gla(u3, w2p[l], gla_gate_b[l][:, None, :])
        h = _merge(o_a.reshape(n, BRANCH_W), o_b.reshape(n, BRANCH_W), o_cf.reshape(n, BRANCH_W),
                   o_cb.reshape(n, BRANCH_W), u, h_srcs, tm_merge,
                   w_branch_b[l], w_out_b[l], gla_norm_g[l][None], post_norm_g[l][None])
        last = l == depth - 1
        next_g = pre_norm_g[(l + 1) % depth][None]
        outs = _ple(h, p_a, p_b, rows_a, l, tm_ple, w_ple_b[l], w_gate_b[l], next_g, last)
        if last:
            y_prompt, y_sample = outs
        else:
            h, xn = outs
            h_srcs = (h,)
    return (y_prompt.reshape(x_prompt.shape), y_sample.reshape(x_sample.shape))
```

```python
import functools
import math

import jax
import jax.numpy as jnp
from jax import lax
from jax.experimental import pallas as pl
from jax.experimental.pallas import tpu as pltpu

F32 = jnp.float32
BF16 = jnp.bfloat16

D_MODEL = 2048
GRID_W = 64
PLE_DIM = 256
BRANCH_W = 1024
EPS = 1e-6
CONV_K = 31
CONV_HALO = 16
CONV_ROWS = 128
N_Q_HEADS = 8
N_KV_HEADS = 2
Q_PER_KV = N_Q_HEADS // N_KV_HEADS
HEAD_DIM = 128
V_PAD_ROWS = 16
ROPE_PAIRS = HEAD_DIM // 4
ROPE_THETA = 10000.0
GLA_HEADS = 4
GLA_DK = 128
GLA_DV = 256
GLA_RANK = 16
GLA_GATE_NORM = 16.0
GLA_CHUNK = 64

A_VAL, A_GLU, A_Z = 0, 1024, 2048
B_Q, B_Z = 3072, 4096
C_V, C_Z = 5120, 6144
C_Q, C_K = 7168, 7680
M_LOG = 8192
B_KV = 14336
C_LR = 14848
MXU_COLS = 256
INPROJ_TN = 6 * MXU_COLS
U_COLS = 15360

VMEM_LIMIT = 56 * 1024 * 1024


def _cparams(sem, vmem=VMEM_LIMIT):
    return pltpu.CompilerParams(dimension_semantics=sem, vmem_limit_bytes=vmem)


def _tile(n, cap, mult=8):
    t = min(cap, n)
    while t > mult and (n % t or t % mult):
        t -= mult
    assert n % t == 0, (n, cap, mult)
    return t


def _sigmoid(x):
    return 1.0 / (1.0 + jnp.exp(-x))


def _silu(x):
    return x * _sigmoid(x)


def _split_specs(tm, width, tiles_a, a_first=0, b_first=0, col=0):
    a = pl.BlockSpec((tm, width), lambda i: (a_first + jnp.minimum(i, tiles_a - 1), col))
    b = pl.BlockSpec((tm, width), lambda i: (b_first + jnp.maximum(i - tiles_a, 0), col))
    return a, b


def _pick(tiles_a, a_ref, b_ref):
    return jnp.where(pl.program_id(0) < tiles_a, a_ref[...], b_ref[...])


def _prenorm_kernel(tiles_a, xa_ref, xb_ref, g_ref, o_ref):
    x = _pick(tiles_a, xa_ref, xb_ref)
    ms = jnp.mean(x * x, axis=-1, keepdims=True)
    o_ref[...] = (x * lax.rsqrt(ms + EPS) * g_ref[...]).astype(BF16)


def _prenorm(xa, xb, g, tm):
    n = xa.shape[0] + xb.shape[0]
    tiles_a = xa.shape[0] // tm
    return pl.pallas_call(
        functools.partial(_prenorm_kernel, tiles_a),
        grid=(n // tm,),
        in_specs=[*_split_specs(tm, D_MODEL, tiles_a),
                  pl.BlockSpec((1, D_MODEL), lambda i: (0, 0))],
        out_specs=pl.BlockSpec((tm, D_MODEL), lambda i: (i, 0)),
        out_shape=jax.ShapeDtypeStruct((n, D_MODEL), BF16),
        compiler_params=_cparams(("arbitrary",)),
        name="prenorm",
    )(xa, xb, g)


def _inproj_kernel(x_ref, w_ref, u_ref):
    u_ref[...] = jnp.dot(x_ref[...], w_ref[...], preferred_element_type=F32).astype(BF16)


def _inproj(xn, w):
    n = xn.shape[0]
    tm = _tile(n, 1024, 16)
    tn = INPROJ_TN
    return pl.pallas_call(
        _inproj_kernel,
        grid=(n // tm, U_COLS // tn),
        in_specs=[pl.BlockSpec((tm, D_MODEL), lambda i, j: (i, 0)),
                  pl.BlockSpec((D_MODEL, tn), lambda i, j: (0, j))],
        out_specs=pl.BlockSpec((tm, tn), lambda i, j: (i, j)),
        out_shape=jax.ShapeDtypeStruct((n, U_COLS), BF16),
        compiler_params=_cparams(("parallel", "arbitrary")),
        name="inproj",
    )(xn, w)


def _conv_kernel(val_ref, glu_ref, z_ref, pval_ref, pglu_ref, nval_ref, nglu_ref,
                 w_ref, cb_ref, lg_ref, lb_ref, o_ref, abuf, cbuf):
    i = pl.program_id(1)
    n = pl.num_programs(1)
    tt = val_ref.shape[0]
    H = CONV_HALO

    def glu(v_ref, g_ref):
        return v_ref[...].astype(F32) * _sigmoid(g_ref[...].astype(F32))

    abuf[0:H, :] = glu(pval_ref, pglu_ref) * (i > 0).astype(F32)
    abuf[H:H + tt, :] = glu(val_ref, glu_ref)
    abuf[H + tt:2 * H + tt, :] = glu(nval_ref, nglu_ref) * (i < n - 1).astype(F32)

    assert H - CONV_K // 2 == 1 and CONV_K < 32
    rows = CONV_ROWS if tt % CONV_ROWS == 0 else tt
    for r0 in range(0, tt, rows):
        for c in range(BRANCH_W // 128):
            cs = slice(c * 128, (c + 1) * 128)
            acc = None
            for r in range(8):
                part = None
                for q in range(4):
                    j = 8 * q + r
                    if j == 0:
                        continue
                    term = abuf[r0 + 8 * q:r0 + 8 * q + rows + 8, cs] * w_ref[j - 1:j, cs]
                    part = term if part is None else part + term
                shifted = part[r:r + rows, :]
                acc = shifted if acc is None else acc + shifted
            cbuf[r0:r0 + rows, cs] = acc + cb_ref[:, cs]

    x = cbuf[...]
    mu = jnp.mean(x, axis=-1, keepdims=True)
    xc = x - mu
    var = jnp.mean(xc * xc, axis=-1, keepdims=True)
    y = xc * lax.rsqrt(var + EPS) * lg_ref[...] + lb_ref[...]
    o_ref[...] = (_silu(y) * _silu(z_ref[...].astype(F32))).astype(BF16)


def _conv_branch(u3, conv_w, conv_b, ln_g, ln_b):
    B, T, _ = u3.shape
    tt = _tile(T, 2 * CONV_ROWS, 16)
    H = CONV_HALO
    hb = tt // H
    nh = T // H

    def cur(col):
        return pl.BlockSpec((None, tt, BRANCH_W), lambda b, i: (b, i, col))

    def prev(col):
        return pl.BlockSpec((None, H, BRANCH_W), lambda b, i: (b, jnp.maximum(i * hb - 1, 0), col))

    def nxt(col):
        return pl.BlockSpec((None, H, BRANCH_W), lambda b, i: (b, jnp.minimum((i + 1) * hb, nh - 1), col))

    vec = pl.BlockSpec((1, BRANCH_W), lambda b, i: (0, 0))
    return pl.pallas_call(
        _conv_kernel,
        grid=(B, T // tt),
        in_specs=[cur(A_VAL // BRANCH_W), cur(A_GLU // BRANCH_W), cur(A_Z // BRANCH_W),
                  prev(A_VAL // BRANCH_W), prev(A_GLU // BRANCH_W),
                  nxt(A_VAL // BRANCH_W), nxt(A_GLU // BRANCH_W),
                  pl.BlockSpec((CONV_K + 1, BRANCH_W), lambda b, i: (0, 0)), vec, vec, vec],
        out_specs=pl.BlockSpec((None, tt, BRANCH_W), lambda b, i: (b, i, 0)),
        out_shape=jax.ShapeDtypeStruct((B, T, BRANCH_W), BF16),
        scratch_shapes=[pltpu.VMEM((tt + 2 * H, BRANCH_W), F32), pltpu.VMEM((tt, BRANCH_W), F32)],
        compiler_params=_cparams(("parallel", "parallel")),
        name="conv",
    )(u3, u3, u3, u3, u3, u3, u3, conv_w, conv_b, ln_g, ln_b)


def _norm_rope(x, g, cos, sin):
    ms = jnp.mean(x * x, axis=-1, keepdims=True)
    y = x * lax.rsqrt(ms + EPS) * g
    return y * cos + pltpu.roll(y, HEAD_DIM // 2, 1) * sin


def _qkprep_kernel(q_ref, kv_ref, cos_ref, sin_ref, gq_ref, gk_ref, qo_ref, k_ref, vT_ref):
    tt = q_ref.shape[0]
    cos = cos_ref[...]
    sin = sin_ref[...]
    qscale = HEAD_DIM ** -0.5 * math.log2(math.e)
    for h in range(N_Q_HEADS):
        x = q_ref[:, h * HEAD_DIM:(h + 1) * HEAD_DIM].astype(F32)
        y = _norm_rope(x, gq_ref[...], cos, sin) * qscale
        qo_ref[h] = y.T.astype(BF16)
    for g in range(N_KV_HEADS):
        x = kv_ref[:, g * HEAD_DIM:(g + 1) * HEAD_DIM].astype(F32)
        k_ref[g] = _norm_rope(x, gk_ref[...], cos, sin).astype(BF16)
        v = kv_ref[:, (N_KV_HEADS + g) * HEAD_DIM:(N_KV_HEADS + g + 1) * HEAD_DIM].astype(F32)
        vT_ref[g, 0, :HEAD_DIM, :] = v.T.astype(BF16)
        pad_row = lax.broadcasted_iota(jnp.int32, (V_PAD_ROWS, tt), 0)
        vT_ref[g, 0, HEAD_DIM:, :] = jnp.where(pad_row == 0, 1.0, 0.0).astype(BF16)


def _qkprep(u3, cos, sin, gq, gk, tk):
    B, T, _ = u3.shape
    qw = N_Q_HEADS * HEAD_DIM
    kvw = 2 * N_KV_HEADS * HEAD_DIM
    vec = pl.BlockSpec((1, HEAD_DIM), lambda b, i: (0, 0))
    tab = pl.BlockSpec((tk, HEAD_DIM), lambda b, i: (i, 0))
    return pl.pallas_call(
        _qkprep_kernel,
        grid=(B, T // tk),
        in_specs=[pl.BlockSpec((None, tk, qw), lambda b, i: (b, i, B_Q // qw)),
                  pl.BlockSpec((None, tk, kvw), lambda b, i: (b, i, B_KV // kvw)),
                  tab, tab, vec, vec],
        out_specs=[pl.BlockSpec((None, N_Q_HEADS, HEAD_DIM, tk), lambda b, i: (b, 0, 0, i)),
                   pl.BlockSpec((None, N_KV_HEADS, tk, HEAD_DIM), lambda b, i: (b, 0, i, 0)),
                   pl.BlockSpec((None, N_KV_HEADS, 1, HEAD_DIM + V_PAD_ROWS, tk), lambda b, i: (b, 0, i, 0, 0))],
        out_shape=[jax.ShapeDtypeStruct((B, N_Q_HEADS, HEAD_DIM, T), BF16),
                   jax.ShapeDtypeStruct((B, N_KV_HEADS, T, HEAD_DIM), BF16),
                   jax.ShapeDtypeStruct((B, N_KV_HEADS, T // tk, HEAD_DIM + V_PAD_ROWS, tk), BF16)],
        compiler_params=_cparams(("parallel", "parallel")),
        name="qkprep",
    )(u3, u3, cos, sin, gq, gk)


def _col_reduce(op, x):
    rows, cols = x.shape
    if rows % 64 == 0:
        x = op(x.reshape(8, rows // 8, cols), axis=0)
    return op(x, axis=0, keepdims=True)


def _attn_kernel(q_ref, k_ref, vT_ref, o_ref, acc_ref, s_ref):
    nk, _, tk = vT_ref.shape
    tq = q_ref.shape[2]
    unroll = next(u for u in (16, 8, 4, 2) if nk % u == 0)
    heads = range(Q_PER_KV)
    acc_ref[...] = jnp.zeros_like(acc_ref)

    def scores(j, slot, h):
        k = k_ref[pl.ds(pl.multiple_of(j * tk, tk), tk), :]
        sT = jnp.dot(k, q_ref[h], preferred_element_type=F32)
        s_ref[slot, h] = sT
        return _col_reduce(jnp.max, sT)

    def consume(j, slot, h, tile_max, m):
        m_new = jnp.maximum(m, tile_max)
        alpha = jnp.exp2(m - m_new)
        p = jnp.exp2(s_ref[slot, h] - m_new).astype(BF16)
        pv = jnp.dot(vT_ref[j], p, preferred_element_type=F32)
        acc_ref[h] = alpha * acc_ref[h] + pv
        return m_new

    def body(i, carry):
        tile_max, ms = carry
        j = unroll * i
        for u in range(unroll):
            nxt = j + u + 1
            if u == unroll - 1:
                nxt = jnp.minimum(nxt, nk - 1)
            next_max = []
            new_ms = []
            for h in heads:
                next_max.append(scores(nxt, (u + 1) % 2, h))
                new_ms.append(consume(j + u, u % 2, h, tile_max[h], ms[h]))
            tile_max, ms = tuple(next_max), tuple(new_ms)
        return tile_max, ms

    m0 = tuple(jnp.full((1, tq), -jnp.inf, F32) for _ in heads)
    lax.fori_loop(0, nk // unroll, body, (tuple(scores(0, 0, h) for h in heads), m0))
    for h in range(Q_PER_KV):
        a = acc_ref[h]
        o = a[:HEAD_DIM] * (1.0 / a[HEAD_DIM:HEAD_DIM + 1])
        o_ref[:, h * HEAD_DIM:(h + 1) * HEAD_DIM] = o.T.astype(BF16)


def _attention(q, k, vT):
    B, _, _, T = q.shape
    nk, hv, tk = vT.shape[2], vT.shape[3], vT.shape[4]
    tq = _tile(T, 512, 128)
    gw = Q_PER_KV * HEAD_DIM
    return pl.pallas_call(
        _attn_kernel,
        grid=(B, N_KV_HEADS, T // tq),
        in_specs=[pl.BlockSpec((None, Q_PER_KV, HEAD_DIM, tq), lambda b, g, i: (b, g, 0, i)),
                  pl.BlockSpec((None, None, T, HEAD_DIM), lambda b, g, i: (b, g, 0, 0)),
                  pl.BlockSpec((None, None, nk, hv, tk), lambda b, g, i: (b, g, 0, 0, 0))],
        out_specs=pl.BlockSpec((None, tq, gw), lambda b, g, i: (b, i, g)),
        out_shape=jax.ShapeDtypeStruct((B, T, N_Q_HEADS * HEAD_DIM), BF16),
        scratch_shapes=[pltpu.VMEM((Q_PER_KV, hv, tq), F32),
                        pltpu.VMEM((2, Q_PER_KV, tk, tq), F32)],
        compiler_params=_cparams(("parallel", "parallel", "arbitrary")),
        name="attention",
    )(q, k, vT)


def _log_sigmoid(z):
    return jnp.minimum(z, 0.0) - jnp.log(1.0 + jnp.exp(-jnp.abs(z)))


def _gla_kernel(qf_ref, kf_ref, vf_ref, lrf_ref, qb_ref, kb_ref, vb_ref, lrb_ref,
                w2_ref, gb_ref, of_ref, ob_ref, s_ref):
    i = pl.program_id(1)
    tb = qf_ref.shape[0]
    L = GLA_CHUNK
    nchunk = tb // L

    @pl.when(i == 0)
    def _():
        s_ref[...] = jnp.zeros_like(s_ref)

    row = lax.broadcasted_iota(jnp.int32, (L, L), 0)
    col = lax.broadcasted_iota(jnp.int32, (L, L), 1)
    masks = (row >= col, col >= row)
    ones = tuple(jnp.where(m, 1.0, 0.0).astype(BF16) for m in masks)
    qscale = GLA_DK ** -0.5
    dirs = ((qf_ref, kf_ref, vf_ref, lrf_ref, of_ref), (qb_ref, kb_ref, vb_ref, lrb_ref, ob_ref))

    log_gate = []
    for d in range(2):
        z = jnp.dot(dirs[d][3][...], w2_ref[d], preferred_element_type=F32) + gb_ref[d]
        log_gate.append(_log_sigmoid(z) * (1.0 / GLA_GATE_NORM))

    cum = {}
    for step in range(nchunk):
        for d in range(2):
            c = step if d == 0 else nchunk - 1 - step
            g = log_gate[d][c * L:(c + 1) * L, :]
            g_hi = g.astype(BF16)
            g_lo = (g - g_hi.astype(F32)).astype(BF16)
            cum[step, d] = (jnp.dot(ones[d], g_hi, preferred_element_type=F32)
                            + jnp.dot(ones[d], g_lo, preferred_element_type=F32))
    scaled = {}
    for step in range(nchunk):
        for d in range(2):
            q_ref, k_ref = dirs[d][0], dirs[d][1]
            c = step if d == 0 else nchunk - 1 - step
            rows = slice(c * L, (c + 1) * L)
            for h in range(GLA_HEADS):
                ks = slice(h * GLA_DK, (h + 1) * GLA_DK)
                b = cum[step, d][:, ks]
                b_tot = b[L - 1:L, :] if d == 0 else b[0:1, :]
                q = q_ref[rows, ks].astype(F32) * qscale
                k = k_ref[rows, ks].astype(F32)
                q_t = (q * jnp.exp(b)).astype(BF16)
                k_t = (k * jnp.exp(-b)).astype(BF16)
                k_d = (k * jnp.exp(b_tot - b)).astype(BF16)
                scaled[step, d, h] = (q_t, k_t, k_d, jnp.exp(b_tot))
    pre = {}
    for step in range(nchunk):
        for d in range(2):
            for h in range(GLA_HEADS):
                q_t, k_t, k_d, e_tot = scaled[step, d, h]
                att = lax.dot_general(q_t, k_t, (((1,), (1,)), ((), ())), preferred_element_type=F32)
                att = jnp.where(masks[d], att, 0.0).astype(BF16)
                pre[step, d, h] = (q_t, k_d, att, e_tot)

    for step in range(nchunk):
        for d in range(2):
            v_ref, o_ref = dirs[d][2], dirs[d][4]
            c = step if d == 0 else nchunk - 1 - step
            rows = slice(c * L, (c + 1) * L)
            for h in range(GLA_HEADS):
                vs = slice(h * GLA_DV, (h + 1) * GLA_DV)
                q_t, k_d, att, e_tot = pre[step, d, h]
                v = v_ref[rows, vs]
                sT = s_ref[d, h]
                o = (jnp.dot(att, v, preferred_element_type=F32)
                     + lax.dot_general(q_t, sT.astype(BF16), (((1,), (1,)), ((), ())),
                                       preferred_element_type=F32))
                o_ref[rows, vs] = o.astype(o_ref.dtype)
                dsT = lax.dot_general(v, k_d, (((0,), (0,)), ((), ())), preferred_element_type=F32)
                s_ref[d, h] = sT * e_tot + dsT


def _gla(u3, w2p, gb):
    B, T, _ = u3.shape
    tb = _tile(T, 512, GLA_CHUNK)
    nb = T // tb
    qk_w = GLA_HEADS * GLA_DK
    v_w = GLA_HEADS * GLA_DV

    def specs(idx):
        return [pl.BlockSpec((None, tb, qk_w), lambda b, i: (b, idx(i), C_Q // qk_w)),
                pl.BlockSpec((None, tb, qk_w), lambda b, i: (b, idx(i), C_K // qk_w)),
                pl.BlockSpec((None, tb, v_w), lambda b, i: (b, idx(i), C_V // v_w)),
                pl.BlockSpec((None, tb, 128), lambda b, i: (b, idx(i), C_LR // 128))]

    fwd = lambda i: i
    bwd = lambda i: nb - 1 - i
    out_sd = jax.ShapeDtypeStruct((B, T, v_w), BF16)
    return pl.pallas_call(
        _gla_kernel,
        grid=(B, nb),
        in_specs=specs(fwd) + specs(bwd) + [
            pl.BlockSpec((2, 128, qk_w), lambda b, i: (0, 0, 0)),
            pl.BlockSpec((2, 1, qk_w), lambda b, i: (0, 0, 0))],
        out_specs=[pl.BlockSpec((None, tb, v_w), lambda b, i: (b, i, 0)),
                   pl.BlockSpec((None, tb, v_w), lambda b, i: (b, nb - 1 - i, 0))],
        out_shape=[out_sd, out_sd],
        scratch_shapes=[pltpu.VMEM((2, GLA_HEADS, GLA_DV, GLA_DK), F32)],
        compiler_params=_cparams(("parallel", "arbitrary")),
        name="gla",
    )(u3, u3, u3, u3, u3, u3, u3, u3, w2p, gb)


def _merge_kernel(tiles_a, oa_ref, ob_ref, bz_ref, cf_ref, cb_ref, cz_ref, m0_ref, m1_ref, m2_ref,
                  *rest):
    *h_refs, wb_ref, wo_ref, gn_ref, pg_ref, o_ref, mg_even, mg_odd = rest
    s = pl.program_id(0)

    @pl.when(s == 0)
    def _():
        mg_odd[...] = jnp.zeros_like(mg_odd)

    def step(mg_read, mg_write):
        out = jnp.dot(mg_read[...], wo_ref[...], preferred_element_type=F32)
        o_a = oa_ref[...]
        o_b = (ob_ref[...].astype(F32) * _silu(bz_ref[...].astype(F32))).astype(BF16)
        parts = []
        for h in range(GLA_HEADS):
            vs = slice(h * GLA_DV, (h + 1) * GLA_DV)
            x = cf_ref[:, vs].astype(F32) + cb_ref[:, vs].astype(F32)
            ms = jnp.mean(x * x, axis=-1, keepdims=True)
            y = x * lax.rsqrt(ms + EPS) * gn_ref[...]
            parts.append((y * _silu(cz_ref[:, vs].astype(F32))).astype(BF16))
        o_c = jnp.concatenate(parts, axis=-1)
        merged = None
        for n, (o, m_ref) in enumerate(((o_a, m0_ref), (o_b, m1_ref), (o_c, m2_ref))):
            term = _sigmoid(m_ref[...].astype(F32)) * jnp.dot(o, wb_ref[n], preferred_element_type=F32)
            merged = term if merged is None else merged + term
        ms = jnp.mean(out * out, axis=-1, keepdims=True)
        h = _pick(tiles_a + 1, *h_refs) if tiles_a else h_refs[0][...]
        o_ref[...] = h + out * lax.rsqrt(ms + EPS) * pg_ref[...]
        mg_write[...] = merged.astype(BF16)

    @pl.when(s % 2 == 0)
    def _():
        step(mg_odd, mg_even)

    @pl.when(s % 2 == 1)
    def _():
        step(mg_even, mg_odd)


def _merge(o_a, o_b, o_cf, o_cb, u, h_srcs, tm, wb, wo, gla_g, post_g):
    n = u.shape[0]
    n_tiles = n // tm
    tiles_a = h_srcs[0].shape[0] // tm if len(h_srcs) == 2 else None
    W = BRANCH_W

    def cur(width, col=0):
        return pl.BlockSpec((tm, width), lambda s: (jnp.minimum(s, n_tiles - 1), col))

    def prev(width):
        return pl.BlockSpec((tm, width), lambda s: (jnp.maximum(s - 1, 0), 0))

    def const(shape):
        return pl.BlockSpec(shape, lambda s: (0,) * len(shape), pipeline_mode=pl.Buffered(1))

    if tiles_a:
        h_specs = (pl.BlockSpec((tm, D_MODEL), lambda s: (jnp.clip(s - 1, 0, tiles_a - 1), 0)),
                   pl.BlockSpec((tm, D_MODEL), lambda s: (jnp.maximum(s - 1 - tiles_a, 0), 0)))
    else:
        h_specs = (prev(D_MODEL),)
    return pl.pallas_call(
        functools.partial(_merge_kernel, tiles_a),
        grid=(n_tiles + 1,),
        in_specs=[cur(W), cur(W), cur(W, B_Z // W), cur(W), cur(W), cur(W, C_Z // W),
                  cur(D_MODEL, M_LOG // D_MODEL), cur(D_MODEL, M_LOG // D_MODEL + 1),
                  cur(D_MODEL, M_LOG // D_MODEL + 2),
                  *h_specs,
                  const((3, W, D_MODEL)), const((D_MODEL, D_MODEL)),
                  const((1, GLA_DV)), const((1, D_MODEL))],
        out_specs=prev(D_MODEL),
        out_shape=jax.ShapeDtypeStruct((n, D_MODEL), F32),
        scratch_shapes=[pltpu.VMEM((tm, D_MODEL), BF16), pltpu.VMEM((tm, D_MODEL), BF16)],
        compiler_params=_cparams(("arbitrary",)),
        name="merge",
    )(o_a, o_b, u, o_cf, o_cb, u, u, u, u, *h_srcs, wb, wo, gla_g, post_g)


def _ple_kernel(tiles_a, last, h_ref, pa_ref, pb_ref, wp_ref, wg_ref, ng_ref, *out_refs):
    h = h_ref[...]
    p = _pick(tiles_a, pa_ref, pb_ref)
    e = jnp.dot(p.astype(BF16), wp_ref[...], preferred_element_type=F32)
    gate = _sigmoid(jnp.dot(h.astype(BF16), wg_ref[...], preferred_element_type=F32))
    h2 = h + gate * e
    if last:
        ya_ref, yb_ref = out_refs
        i = pl.program_id(0)

        @pl.when(i < tiles_a)
        def _():
            ya_ref[...] = h2

        @pl.when(i >= tiles_a)
        def _():
            yb_ref[...] = h2
    else:
        o_ref, xn_ref = out_refs
        o_ref[...] = h2
        ms = jnp.mean(h2 * h2, axis=-1, keepdims=True)
        xn_ref[...] = (h2 * lax.rsqrt(ms + EPS) * ng_ref[...]).astype(BF16)


def _ple(h, p_a, p_b, rows_a, layer, tm, wp, wg, next_g, last):
    n = h.shape[0]
    tiles_a = rows_a // tm
    tiles_b = (n - rows_a) // tm

    def rows(width):
        return pl.BlockSpec((tm, width), lambda i: (i, 0))

    def const(shape):
        return pl.BlockSpec(shape, lambda i: (0,) * len(shape), pipeline_mode=pl.Buffered(1))

    if last:
        out_specs = list(_split_specs(tm, D_MODEL, tiles_a))
        out_shape = [jax.ShapeDtypeStruct((rows_a, D_MODEL), F32),
                     jax.ShapeDtypeStruct((n - rows_a, D_MODEL), F32)]
    else:
        out_specs = [rows(D_MODEL), rows(D_MODEL)]
        out_shape = [jax.ShapeDtypeStruct((n, D_MODEL), F32), jax.ShapeDtypeStruct((n, D_MODEL), BF16)]
    return pl.pallas_call(
        functools.partial(_ple_kernel, tiles_a, last),
        grid=(n // tm,),
        in_specs=[rows(D_MODEL),
                  *_split_specs(tm, PLE_DIM, tiles_a, layer * tiles_a, layer * tiles_b),
                  const((PLE_DIM, D_MODEL)), const((D_MODEL, D_MODEL)), const((1, D_MODEL))],
        out_specs=out_specs,
        out_shape=out_shape,
        compiler_params=_cparams(("arbitrary",)),
        name="ple",
    )(h, p_a, p_b, wp, wg, next_g)


def _reorder_w_in(w_in):
    sizes = (1024, 1024, 1024, 1024, 256, 256, 1024, 512, 512, 1024, 2 * GLA_RANK, 1024, 3 * D_MODEL)
    names = ("a_val", "a_glu", "a_z", "b_q", "b_k", "b_v", "b_z", "c_q", "c_k", "c_v", "c_lr", "c_z", "m")
    parts, start = {}, 0
    for name, s in zip(names, sizes):
        parts[name] = w_in[..., start:start + s].astype(BF16)
        start += s
    parts["b_q"] = _permute_heads(parts["b_q"], N_Q_HEADS)
    parts["b_k"] = _permute_heads(parts["b_k"], N_KV_HEADS)
    pad = jnp.zeros(w_in.shape[:-1] + (U_COLS - C_LR - 2 * GLA_RANK,), BF16)
    order = ("a_val", "a_glu", "a_z", "b_q", "b_z", "c_v", "c_z", "c_q", "c_k", "m", "b_k", "b_v", "c_lr")
    w = jnp.concatenate([parts[k] for k in order] + [pad], axis=-1)
    assert w.shape[-1] == U_COLS
    return w


def _rope_tables(T):
    n_rows = T // GRID_W
    row_idx = jnp.repeat(jnp.arange(n_rows), GRID_W).astype(F32)
    col_idx = jnp.tile(jnp.arange(GRID_W), n_rows).astype(F32)
    inv = ROPE_THETA ** (-jnp.arange(ROPE_PAIRS, dtype=F32) / ROPE_PAIRS)
    ang_r = row_idx[:, None] * inv
    ang_c = col_idx[:, None] * inv
    cos = jnp.concatenate([jnp.cos(ang_r), jnp.cos(ang_c), jnp.cos(ang_r), jnp.cos(ang_c)], axis=-1)
    sin = jnp.concatenate([-jnp.sin(ang_r), -jnp.sin(ang_c), jnp.sin(ang_r), jnp.sin(ang_c)], axis=-1)
    return cos, sin


_ROPE_PERM_BLOCKS = (0, 2, 1, 3)


def _permute_heads(w, n_heads):
    shaped = w.reshape(w.shape[:-1] + (n_heads, HEAD_DIM))
    blocks = [shaped[..., b * ROPE_PAIRS:(b + 1) * ROPE_PAIRS] for b in _ROPE_PERM_BLOCKS]
    return jnp.concatenate(blocks, axis=-1).reshape(w.shape)


def _gla_gate_weights(gate_w2):
    depth = gate_w2.shape[0]
    w = jnp.zeros((depth, 2, 128, GLA_HEADS * GLA_DK), F32)
    for d in range(2):
        w = w.at[:, d, d * GLA_RANK:(d + 1) * GLA_RANK, :].set(gate_w2[:, d])
    return w.astype(BF16)


def kernel(x_prompt, x_sample, p_prompt, p_sample, pre_norm_g, post_norm_g, w_in, conv_w, conv_b, conv_ln_g, conv_ln_b, q_norm_g, k_norm_g, gla_gate_w2, gla_gate_b, gla_norm_g, w_branch, w_out, w_ple, w_ple_gate):
    depth = w_in.shape[0]
    T = x_prompt.shape[1]
    assert x_sample.shape[1] == T
    B = x_prompt.shape[0] + x_sample.shape[0]
    n = B * T
    x_a = x_prompt.reshape(-1, D_MODEL)
    x_b = x_sample.reshape(-1, D_MODEL)
    rows_a = x_a.shape[0]
    p_a = p_prompt.reshape(-1, PLE_DIM)
    p_b = p_sample.reshape(-1, PLE_DIM)
    tm_merge = _tile(T, 256, 16)
    tm_ple = _tile(T, 512, 16)

    w_in_r = [_reorder_w_in(w_in[l]) for l in range(depth)]
    w_branch_b = [w_branch[l].astype(BF16) for l in range(depth)]
    w_out_b = [w_out[l].astype(BF16) for l in range(depth)]
    w_ple_b = [w_ple[l].astype(BF16) for l in range(depth)]
    w_gate_b = [w_ple_gate[l].astype(BF16) for l in range(depth)]
    w2p = _gla_gate_weights(gla_gate_w2)
    conv_w_p = jnp.pad(conv_w, ((0, 0), (0, 1), (0, 0)))
    cos, sin = _rope_tables(T)
    gq_perm = _permute_heads(q_norm_g, 1)
    gk_perm = _permute_heads(k_norm_g, 1)
    tk = _tile(T // 2, 512, 128)

    xn = _prenorm(x_a, x_b, pre_norm_g[0][None], tm_ple)
    h_srcs = (x_a, x_b)
    for l in range(depth):
        u = _inproj(xn, w_in_r[l])
        u3 = u.reshape(B, T, U_COLS)
        o_a = _conv_branch(u3, conv_w_p[l], conv_b[l][None], conv_ln_g[l][None], conv_ln_b[l][None])
        q, k, vT = _qkprep(u3, cos, sin, gq_perm[l][None], gk_perm[l][None], tk)
        o_b = _attention(q, k, vT)
        o_cf, o_cb = _gla(u3, w2p[l], gla_gate_b[l][:, None, :])
        h = _merge(o_a.reshape(n, BRANCH_W), o_b.reshape(n, BRANCH_W), o_cf.reshape(n, BRANCH_W),
                   o_cb.reshape(n, BRANCH_W), u, h_srcs, tm_merge,
                   w_branch_b[l], w_out_b[l], gla_norm_g[l][None], post_norm_g[l][None])
        last = l == depth - 1
        next_g = pre_norm_g[(l + 1) % depth][None]
        outs = _ple(h, p_a, p_b, rows_a, l, tm_ple, w_ple_b[l], w_gate_b[l], next_g, last)
        if last:
            y_prompt, y_sample = outs
        else:
            h, xn = outs
            h_srcs = (h,)
    return (y_prompt.reshape(x_prompt.shape), y_sample.reshape(x_sample.shape))
```
